```python
import math
import jax, jax.numpy as jnp
from jax import lax
import numpy as np

D_MODEL = 2048
BATCH = 4
SEQ = 2048
DEPTH = 1
DEC_BATCH = 128
DEC_SEQ = 8
PAST_LEN = 16384
PAGE_SIZE = 128

D_RNN = D_MODEL // 2
RNN_BLOCKS = 8
RNN_BS = D_RNN // RNN_BLOCKS
RG_C = 8.0
CONV_W = 4
GDN_HEADS = 8
GDN_DK = 128
GDN_DV = 128
D_GDN_K = GDN_HEADS * GDN_DK
D_GDN_V = GDN_HEADS * GDN_DV
D_GDN_CONV = 2 * D_GDN_K + D_GDN_V
GDN_CHUNK = 64
MEM_LEN = 256
MEM_HEADS = 4
MEM_HD = D_MODEL // 8
D_MEM = MEM_HEADS * MEM_HD
N_BRANCH = 3
D_FF = 4 * D_MODEL
EPS = 1e-6
IN_SIZES = (D_RNN, D_RNN, D_GDN_K, D_GDN_K, D_GDN_V, D_GDN_V, GDN_HEADS, GDN_HEADS, D_MEM, N_BRANCH * D_MODEL)
N_IN = sum(IN_SIZES)
IN_SPLITS = tuple(sum(IN_SIZES[:i + 1]) for i in range(len(IN_SIZES) - 1))

kernel_name = 'hybrid_rglru_gdn_memxattn_decode_step'


def rmsnorm(x, g):
    xf = x.astype(jnp.float32)
    y = xf * lax.rsqrt(jnp.mean(xf * xf, axis=-1, keepdims=True) + EPS)
    return (y * g.astype(jnp.float32)).astype(x.dtype)


def l2norm(x):
    return x * lax.rsqrt(jnp.sum(x * x, axis=-1, keepdims=True) + EPS)


def causal_dwconv(x, buf, w):
    T = x.shape[1]
    xp = jnp.concatenate([buf.astype(x.dtype), x], axis=1)
    y = sum(xp[:, j:j + T] * w[j] for j in range(CONV_W))
    return y, xp[:, -(CONV_W - 1):]


def rglru(x, h0, wx, bx, wa, ba, L):
    B, T, _ = x.shape
    xb = x.reshape(B, T, RNN_BLOCKS, RNN_BS)
    gi = jax.nn.sigmoid(jnp.einsum('btni,nij->btnj', xb, wx).reshape(B, T, D_RNN) + bx).astype(jnp.float32)
    gr = jax.nn.sigmoid(jnp.einsum('btni,nij->btnj', xb, wa).reshape(B, T, D_RNN) + ba).astype(jnp.float32)
    log_a = RG_C * gr * jax.nn.log_sigmoid(L.astype(jnp.float32))
    a = jnp.exp(log_a)
    u = jnp.sqrt(-jnp.expm1(2.0 * log_a)) * (gi * x.astype(jnp.float32))
    u = u.at[:, 0].add(a[:, 0] * h0.astype(jnp.float32))

    def combine(left, right):
        a1, b1 = left
        a2, b2 = right
        return a1 * a2, a2 * b1 + b2

    _, h = lax.associative_scan(combine, (a, u), axis=1)
    return h, h[:, -1]


def gated_delta_chunked(q, k, v, g, beta, S0):
    B, T, H, _ = q.shape
    C = min(GDN_CHUNK, T)
    n = -(-T // C)
    pad = n * C - T

    def prep(t):
        t = jnp.pad(t.astype(jnp.float32), [(0, 0), (0, pad)] + [(0, 0)] * (t.ndim - 2))
        t = t.reshape((B, n, C) + t.shape[2:])
        return jnp.swapaxes(t, 2, 3)

    q, k, v, g, beta = prep(q), prep(k), prep(v), prep(g), prep(beta)
    gc = jnp.cumsum(g, axis=-1)
    incl = jnp.tril(jnp.ones((C, C), bool))
    strict = jnp.tril(jnp.ones((C, C), bool), -1)
    decay = jnp.exp(jnp.where(incl, gc[..., :, None] - gc[..., None, :], -jnp.inf))
    kb = k * beta[..., None]
    vb = v * beta[..., None]
    M = jnp.where(strict, jnp.einsum('bnhik,bnhjk->bnhij', kb, k) * decay, 0.0)
    eye = jnp.eye(C, dtype=jnp.float32)
    Tm = lax.linalg.triangular_solve(eye + M, jnp.broadcast_to(eye, M.shape), left_side=True, lower=True,
                                     unit_diagonal=True)
    value = jnp.einsum('bnhij,bnhjv->bnhiv', Tm, vb)
    kcd = jnp.einsum('bnhij,bnhjk->bnhik', Tm, kb * jnp.exp(gc)[..., None])

    def step(S, xs):
        q_c, k_c, val_c, kcd_c, gc_c, dec_c = xs
        v_new = val_c - jnp.einsum('bhck,bhkv->bhcv', kcd_c, S)
        attn = jnp.einsum('bhik,bhjk->bhij', q_c, k_c) * dec_c
        o = (jnp.einsum('bhck,bhkv->bhcv', q_c * jnp.exp(gc_c)[..., None], S)
             + jnp.einsum('bhij,bhjv->bhiv', attn, v_new))
        g_last = gc_c[..., -1]
        S = (S * jnp.exp(g_last)[..., None, None]
             + jnp.einsum('bhck,bhcv->bhkv', k_c * jnp.exp(g_last[..., None] - gc_c)[..., None], v_new))
        return S, o

    xs = tuple(jnp.moveaxis(t, 1, 0) for t in (q, k, value, kcd, gc, decay))
    S_final, o = lax.scan(step, S0.astype(jnp.float32), xs)
    o = jnp.transpose(o, (1, 0, 3, 2, 4)).reshape(B, n * C, H, GDN_DV)[:, :T]
    return o, S_final


def memory_kv(mem, g, w):
    B = mem.shape[0]
    kv = rmsnorm(mem, g) @ w
    k, v = jnp.split(kv, 2, axis=-1)
    return k.reshape(B, MEM_LEN, MEM_HEADS, MEM_HD), v.reshape(B, MEM_LEN, MEM_HEADS, MEM_HD)


def mix_branches(xn, mem_k, mem_v, rnn_buf, rnn_h0, gdn_buf, gdn_S0, w_in, rnn_conv_w, rnn_conv_b, rnn_wx,
                 rnn_bx, rnn_wa, rnn_ba, rnn_L, gdn_conv_w, gdn_A_log, gdn_dt_bias, gdn_norm_g, w_rnn_up,
                 w_gdn_up, w_mem_up, w_out):
    B, T, _ = xn.shape
    proj = xn @ w_in
    rx, rg, gq, gk, gv, gz, gb, ga, mq, mg = jnp.split(proj, IN_SPLITS, axis=-1)

    rx_c, rnn_buf_new = causal_dwconv(rx, rnn_buf, rnn_conv_w)
    h, h_last = rglru(rx_c + rnn_conv_b, rnn_h0, rnn_wx, rnn_bx, rnn_wa, rnn_ba, rnn_L)
    y_rnn = (h * jax.nn.gelu(rg.astype(jnp.float32))).astype(xn.dtype)

    qkv_c, gdn_buf_new = causal_dwconv(jnp.concatenate([gq, gk, gv], axis=-1), gdn_buf, gdn_conv_w)
    qkv_c = jax.nn.silu(qkv_c)
    q, k, v = jnp.split(qkv_c, (D_GDN_K, 2 * D_GDN_K), axis=-1)
    q = l2norm(q.reshape(B, T, GDN_HEADS, GDN_DK).astype(jnp.float32)) * (GDN_DK ** -0.5)
    k = l2norm(k.reshape(B, T, GDN_HEADS, GDN_DK).astype(jnp.float32))
    v = v.reshape(B, T, GDN_HEADS, GDN_DV)
    beta = jax.nn.sigmoid(gb.astype(jnp.float32))
    g = -jnp.exp(gdn_A_log.astype(jnp.float32)) * jax.nn.softplus(ga.astype(jnp.float32) + gdn_dt_bias.astype(jnp.float32))
    o, S_new = gated_delta_chunked(q, k, v, g, beta, gdn_S0)
    z = gz.reshape(B, T, GDN_HEADS, GDN_DV).astype(jnp.float32)
    o = rmsnorm(o, gdn_norm_g) * jax.nn.silu(z)
    y_gdn = o.reshape(B, T, D_GDN_V).astype(xn.dtype)

    qm = mq.reshape(B, T, MEM_HEADS, MEM_HD)
    s = jnp.einsum('bthd,bmhd->bhtm', qm, mem_k.astype(qm.dtype)).astype(jnp.float32) * (MEM_HD ** -0.5)
    p = jax.nn.softmax(s, axis=-1)
    om = jnp.einsum('bhtm,bmhd->bthd', p.astype(xn.dtype), mem_v.astype(xn.dtype)).reshape(B, T, D_MEM)

    gates = jax.nn.sigmoid(mg.reshape(B, T, N_BRANCH, D_MODEL))
    merged = (gates[:, :, 0] * (y_rnn @ w_rnn_up) + gates[:, :, 1] * (y_gdn @ w_gdn_up)
              + gates[:, :, 2] * (om @ w_mem_up))
    return merged @ w_out, (rnn_buf_new, h_last, gdn_buf_new, S_new)


def decoder_layer(x, mem_k, mem_v, rnn_buf, rnn_h0, gdn_buf, gdn_S0, norm_mix_g, mix_params, norm_mlp_g,
                  w_mlp_up, w_mlp_down):
    y_mix, states = mix_branches(rmsnorm(x, norm_mix_g), mem_k, mem_v, rnn_buf, rnn_h0, gdn_buf, gdn_S0,
                                 *mix_params)
    x = x + y_mix
    hid = rmsnorm(x, norm_mlp_g) @ w_mlp_up
    x = x + jnp.square(jax.nn.relu(hid)) @ w_mlp_down
    return x, states


def setup_inputs(seed: int = 0) -> dict:
    key = jax.random.key(seed)
    ks = iter(jax.random.split(key, 40))
    f32 = jnp.float32

    def nrm(shape, scale):
        return jax.random.normal(next(ks), shape, f32) * scale

    def gain(shape):
        return 1.0 + nrm(shape, 0.01)

    p_a = jax.random.uniform(next(ks), (DEPTH, D_RNN), f32, 0.9, 0.999) ** (1.0 / RG_C)
    dt = jax.random.uniform(next(ks), (DEPTH, GDN_HEADS), f32, 0.001, 0.1)
    return {
        'x_prompt': nrm((BATCH, SEQ, D_MODEL), 1.0),
        'x_sample': nrm((DEC_BATCH, DEC_SEQ, D_MODEL), 1.0),
        'mem_prompt': nrm((BATCH, MEM_LEN, D_MODEL), 1.0),
        'cache_mem_k': nrm((DEPTH, DEC_BATCH, MEM_LEN, MEM_HEADS, MEM_HD), 1.0),
        'cache_mem_v': nrm((DEPTH, DEC_BATCH, MEM_LEN, MEM_HEADS, MEM_HD), 1.0),
        'state_rnn_conv': nrm((DEPTH, DEC_BATCH, CONV_W - 1, D_RNN), 1.0),
        'state_rnn_h': nrm((DEPTH, DEC_BATCH, D_RNN), 0.5),
        'state_gdn_conv': nrm((DEPTH, DEC_BATCH, CONV_W - 1, D_GDN_CONV), 1.0),
        'state_gdn_S': nrm((DEPTH, DEC_BATCH, GDN_HEADS, GDN_DK, GDN_DV), 0.1),
        'norm_mix_g': gain((DEPTH, D_MODEL)),
        'w_in': nrm((DEPTH, D_MODEL, N_IN), D_MODEL ** -0.5),
        'rnn_conv_w': nrm((DEPTH, CONV_W, D_RNN), CONV_W ** -0.5),
        'rnn_conv_b': nrm((DEPTH, D_RNN), 0.01),
        'rnn_wx': nrm((DEPTH, RNN_BLOCKS, RNN_BS, RNN_BS), RNN_BS ** -0.5),
        'rnn_bx': nrm((DEPTH, D_RNN), 0.01),
        'rnn_wa': nrm((DEPTH, RNN_BLOCKS, RNN_BS, RNN_BS), RNN_BS ** -0.5),
        'rnn_ba': nrm((DEPTH, D_RNN), 0.01),
        'rnn_L': jnp.log(p_a) - jnp.log1p(-p_a),
        'gdn_conv_w': nrm((DEPTH, CONV_W, D_GDN_CONV), CONV_W ** -0.5),
        'gdn_A_log': jnp.log(jax.random.uniform(next(ks), (DEPTH, GDN_HEADS), f32, 1.0, 16.0)),
        'gdn_dt_bias': dt + jnp.log(-jnp.expm1(-dt)),
        'gdn_norm_g': gain((DEPTH, GDN_DV)),
        'mem_norm_g': gain((DEPTH, D_MODEL)),
        'w_mem_kv': nrm((DEPTH, D_MODEL, 2 * D_MEM), D_MODEL ** -0.5),
        'w_rnn_up': nrm((DEPTH, D_RNN, D_MODEL), D_RNN ** -0.5),
        'w_gdn_up': nrm((DEPTH, D_GDN_V, D_MODEL), D_GDN_V ** -0.5),
        'w_mem_up': nrm((DEPTH, D_MEM, D_MODEL), D_MEM ** -0.5),
        'w_out': nrm((DEPTH, D_MODEL, D_MODEL), D_MODEL ** -0.5),
        'norm_mlp_g': gain((DEPTH, D_MODEL)),
        'w_mlp_up': nrm((DEPTH, D_MODEL, D_FF), D_MODEL ** -0.5),
        'w_mlp_down': nrm((DEPTH, D_FF, D_MODEL), D_FF ** -0.5),
        'norm_final_g': gain((D_MODEL,)),
    }


def reference(x_prompt, x_sample, mem_prompt, cache_mem_k, cache_mem_v, state_rnn_conv, state_rnn_h,
              state_gdn_conv, state_gdn_S, norm_mix_g, w_in, rnn_conv_w, rnn_conv_b, rnn_wx, rnn_bx, rnn_wa,
              rnn_ba, rnn_L, gdn_conv_w, gdn_A_log, gdn_dt_bias, gdn_norm_g, mem_norm_g, w_mem_kv, w_rnn_up,
              w_gdn_up, w_mem_up, w_out, norm_mlp_g, w_mlp_up, w_mlp_down, norm_final_g):
    Bp = x_prompt.shape[0]
    hp, hs = x_prompt, x_sample
    mk_p, mv_p, rb_p, rh_p, gb_p, gs_p = [], [], [], [], [], []
    rb_s, rh_s, gb_s, gs_s = [], [], [], []
    for l in range(DEPTH):
        mix_params = (w_in[l], rnn_conv_w[l], rnn_conv_b[l], rnn_wx[l], rnn_bx[l], rnn_wa[l], rnn_ba[l],
                      rnn_L[l], gdn_conv_w[l], gdn_A_log[l], gdn_dt_bias[l], gdn_norm_g[l], w_rnn_up[l],
                      w_gdn_up[l], w_mem_up[l], w_out[l])
        mk, mv = memory_kv(mem_prompt, mem_norm_g[l], w_mem_kv[l])
        hp, (rb, rh, gb, gs) = decoder_layer(
            hp, mk, mv,
            jnp.zeros((Bp, CONV_W - 1, D_RNN), hp.dtype), jnp.zeros((Bp, D_RNN), jnp.float32),
            jnp.zeros((Bp, CONV_W - 1, D_GDN_CONV), hp.dtype),
            jnp.zeros((Bp, GDN_HEADS, GDN_DK, GDN_DV), jnp.float32),
            norm_mix_g[l], mix_params, norm_mlp_g[l], w_mlp_up[l], w_mlp_down[l])
        mk_p.append(mk); mv_p.append(mv); rb_p.append(rb); rh_p.append(rh); gb_p.append(gb); gs_p.append(gs)
        hs, (rb, rh, gb, gs) = decoder_layer(
            hs, cache_mem_k[l], cache_mem_v[l], state_rnn_conv[l], state_rnn_h[l], state_gdn_conv[l],
            state_gdn_S[l], norm_mix_g[l], mix_params, norm_mlp_g[l], w_mlp_up[l], w_mlp_down[l])
        rb_s.append(rb); rh_s.append(rh); gb_s.append(gb); gs_s.append(gs)
    y_prompt = rmsnorm(hp, norm_final_g)
    y_sample = rmsnorm(hs, norm_final_g)
    return (y_prompt, y_sample, jnp.stack(mk_p), jnp.stack(mv_p), jnp.stack(rb_p), jnp.stack(rh_p),
            jnp.stack(gb_p), jnp.stack(gs_p), jnp.stack(rb_s), jnp.stack(rh_s), jnp.stack(gb_s), jnp.stack(gs_s))
```

```python
import functools

import jax
import jax.numpy as jnp
from jax import lax
from jax.experimental import pallas as pl
from jax.experimental.pallas import tpu as pltpu

F32 = jnp.float32
BF16 = jnp.bfloat16

EPS = 1e-6
RG_C = 8.0
CONV_W = 4
RNN_BLOCKS = 8
GDN_HEADS = 8
GDN_CHUNK = 64
MEM_HEADS = 4
N_BRANCH = 3

SUBLANES = 8
LANES = 128
VMEM_LIMIT_BYTES = 56 * 1024 * 1024


def _params(*sem):
    return pltpu.CompilerParams(dimension_semantics=sem, vmem_limit_bytes=VMEM_LIMIT_BYTES)


def _dot(a, b):
    return jnp.dot(a.astype(BF16), b.astype(BF16), preferred_element_type=F32)


def _dot_nt(a, b):
    return lax.dot_general(a.astype(BF16), b.astype(BF16), (((1,), (1,)), ((), ())),
                           preferred_element_type=F32)


def _dot_tn(a, b):
    return lax.dot_general(a.astype(BF16), b.astype(BF16), (((0,), (0,)), ((), ())),
                           preferred_element_type=F32)


def _dot_f32(a, b):
    return jnp.dot(a, b, preferred_element_type=F32, precision=lax.Precision.HIGHEST)


def _rmsnorm(x, g):
    return (x * lax.rsqrt(jnp.mean(x * x, axis=-1, keepdims=True) + EPS)) * g


def _softplus(x):
    return jnp.maximum(x, 0.0) + jnp.log1p(jnp.exp(-jnp.abs(x)))


def _silu(x):
    return x * jax.nn.sigmoid(x)


def _norm_matmul_kernel(*refs, has_small):
    if has_small:
        x_ref, g_ref, w_ref, ws_ref, o_ref, os_ref, xn_ref = refs
    else:
        x_ref, g_ref, w_ref, o_ref, xn_ref = refs

    @pl.when(pl.program_id(1) == 0)
    def _():
        xn_ref[...] = _rmsnorm(x_ref[...], g_ref[...]).astype(BF16)
        if has_small:
            os_ref[...] = jnp.dot(xn_ref[...], ws_ref[...], preferred_element_type=F32)

    o_ref[...] = jnp.dot(xn_ref[...], w_ref[...], preferred_element_type=F32)


def norm_matmul(x, g, w, w_small=None, *, tm, tn):
    M, K = x.shape
    N = w.shape[1]
    has_small = w_small is not None
    in_specs = [pl.BlockSpec((tm, K), lambda i, j: (i, 0)),
                pl.BlockSpec((1, K), lambda i, j: (0, 0)),
                pl.BlockSpec((K, tn), lambda i, j: (0, j))]
    out_shape = [jax.ShapeDtypeStruct((M, N), F32)]
    out_specs = [pl.BlockSpec((tm, tn), lambda i, j: (i, j))]
    args = [x, g.reshape(1, K), w]
    if has_small:
        ns = w_small.shape[1]
        in_specs.append(pl.BlockSpec((K, ns), lambda i, j: (0, 0)))
        out_shape.append(jax.ShapeDtypeStruct((M, ns), F32))
        out_specs.append(pl.BlockSpec((tm, ns), lambda i, j: (i, 0)))
        args.append(w_small)
    outs = pl.pallas_call(
        functools.partial(_norm_matmul_kernel, has_small=has_small),
        grid=(M // tm, N // tn),
        in_specs=in_specs, out_specs=out_specs, out_shape=out_shape,
        scratch_shapes=[pltpu.VMEM((tm, K), BF16)],
        compiler_params=_params("parallel", "arbitrary"),
        name="norm_matmul",
    )(*args)
    return outs if has_small else outs[0]


def _causal_conv(x, xprev, w_ref):
    t_idx = lax.broadcasted_iota(jnp.int32, x.shape, 1)
    y = x * w_ref[CONV_W - 1:CONV_W, :][None]
    for k in range(1, CONV_W):
        shifted = jnp.where(t_idx >= k, pltpu.roll(x, k, 1), pltpu.roll(xprev, k, 1))
        y = y + shifted * w_ref[CONV_W - 1 - k:CONV_W - k, :][None]
    return y


def _segment_scan(a, u):
    t_idx = lax.broadcasted_iota(jnp.int32, a.shape, 1)
    s = 1
    while s < SUBLANES:
        keep = t_idx >= s
        u = jnp.where(keep, a * pltpu.roll(u, s, 1) + u, u)
        a = jnp.where(keep, a * pltpu.roll(a, s, 1), a)
        s *= 2
    return a, u


def _stage_groups(x, xprev_ref, xs_ref, first):
    G = x.shape[0]

    @pl.when(first)
    def _():
        xs_ref[G] = xprev_ref[0]

    xs_ref[0] = xs_ref[G]
    xs_ref[1:G + 1] = x
    return xs_ref[0:G]


def _rglru_kernel(rx_ref, rg_ref, xprev_ref, h0_ref, cw_ref, cb_ref, wg_ref, bx_ref, ba_ref, l_ref,
                  y_ref, xlast_ref, hlast_ref, *scratch, carry):
    R, C = rx_ref.shape
    G = R // SUBLANES
    x = rx_ref[...].reshape(G, SUBLANES, C)
    if carry:
        xs_ref, a_ref, u_ref, h_ref, hc_ref = scratch
        first = pl.program_id(1) == 0
        xprev = _stage_groups(x, xprev_ref, xs_ref, first)

        @pl.when(first)
        def _():
            hc_ref[...] = h0_ref[0]
    else:
        xprev = xprev_ref[...]

    xc = (_causal_conv(x, xprev, cw_ref) + cb_ref[...][None]).reshape(R, C)
    xb = xc.astype(BF16)
    bs = C // RNN_BLOCKS
    zi, zr = [], []
    for n in range(RNN_BLOCKS):
        z = jnp.dot(xb[:, n * bs:(n + 1) * bs], wg_ref[n], preferred_element_type=F32)
        zi.append(z[:, :bs])
        zr.append(z[:, bs:])
    gi = jax.nn.sigmoid(jnp.concatenate(zi, axis=1) + bx_ref[...])
    gr = jax.nn.sigmoid(jnp.concatenate(zr, axis=1) + ba_ref[...])
    lv = l_ref[...]
    log_sig_l = -_softplus(-lv)
    log_a = RG_C * gr * log_sig_l
    a = jnp.exp(log_a)
    u = jnp.sqrt(-jnp.tanh(log_a) * (a * a + 1.0)) * (gi * xc)
    a_cum, h_loc = _segment_scan(a.reshape(G, SUBLANES, C), u.reshape(G, SUBLANES, C))

    if carry:
        a_ref[...] = a_cum
        u_ref[...] = h_loc

        def body(g, h_prev):
            hg = u_ref[g] + a_ref[g] * h_prev
            h_ref[g] = hg
            return hg[SUBLANES - 1:SUBLANES, :]

        h_last = lax.fori_loop(0, G, body, hc_ref[...])
        hc_ref[...] = h_last
        h = h_ref[...]
        xlast_ref[0] = x[G - 1]
        hlast_ref[0] = h_last
    else:
        h = h_loc + a_cum * h0_ref[...]
        xlast_ref[...] = x
        hlast_ref[...] = h[:, SUBLANES - 1:SUBLANES, :]

    y = h.reshape(R, C) * jax.nn.gelu(rg_ref[...])
    y_ref[...] = y.astype(BF16)


def rglru_branch(proj, col_rx, col_rg, xprev, h0, cw, cb, wg, bx, ba, lam, *, n_seq, seq_len, rows):
    C = cw.shape[1]
    carry = seq_len > SUBLANES
    if carry:
        nt = seq_len // rows
        grid = (n_seq, nt)
        row_map = lambda b, t: (b * nt + t)
        nb = 1
        G = rows // SUBLANES
        scratch = [pltpu.VMEM((G + 1, SUBLANES, C), F32), pltpu.VMEM((G, SUBLANES, C), F32),
                   pltpu.VMEM((G, SUBLANES, C), F32), pltpu.VMEM((G, SUBLANES, C), F32),
                   pltpu.VMEM((1, C), F32)]
    else:
        nb = rows // SUBLANES
        grid = (n_seq // nb, 1)
        row_map = lambda b, t: b
        scratch = []
    const2 = lambda b, t: (0, 0)
    in_specs = [pl.BlockSpec((rows, C), lambda b, t: (row_map(b, t), col_rx)),
                pl.BlockSpec((rows, C), lambda b, t: (row_map(b, t), col_rg)),
                pl.BlockSpec((nb, SUBLANES, C), lambda b, t: (b, 0, 0)),
                pl.BlockSpec((nb, 1, C), lambda b, t: (b, 0, 0)),
                pl.BlockSpec((CONV_W, C), const2),
                pl.BlockSpec((1, C), const2),
                pl.BlockSpec(wg.shape, lambda b, t: (0, 0, 0)),
                pl.BlockSpec((1, C), const2), pl.BlockSpec((1, C), const2), pl.BlockSpec((1, C), const2)]
    tokens = n_seq * seq_len
    out_shape = [jax.ShapeDtypeStruct((tokens, C), BF16),
                 jax.ShapeDtypeStruct((n_seq, SUBLANES, C), F32),
                 jax.ShapeDtypeStruct((n_seq, 1, C), F32)]
    out_specs = [pl.BlockSpec((rows, C), lambda b, t: (row_map(b, t), 0)),
                 pl.BlockSpec((nb, SUBLANES, C), lambda b, t: (b, 0, 0)),
                 pl.BlockSpec((nb, 1, C), lambda b, t: (b, 0, 0))]
    return pl.pallas_call(
        functools.partial(_rglru_kernel, carry=carry),
        grid=grid, in_specs=in_specs, out_specs=out_specs, out_shape=out_shape,
        scratch_shapes=scratch,
        compiler_params=_params("parallel", "arbitrary"),
        name="rglru",
    )(proj, proj, xprev, h0, cw, cb.reshape(1, C), wg, bx.reshape(1, C), ba.reshape(1, C),
      lam.reshape(1, C))


def _l2norm_heads(x, scale):
    dh = x.shape[1] // GDN_HEADS
    outs = []
    for h in range(GDN_HEADS):
        xh = x[:, h * dh:(h + 1) * dh]
        xh = xh * lax.rsqrt(jnp.sum(xh * xh, axis=-1, keepdims=True) + EPS)
        outs.append(xh * scale if scale != 1.0 else xh)
    return jnp.concatenate(outs, axis=1)


def _gdn_prep_kernel(gq_ref, gk_ref, gv_ref, xprev_ref, cw_ref, q_ref, k_ref, v_ref, xlast_ref,
                     *scratch, carry):
    R, C = gq_ref.shape
    G = R // SUBLANES
    dk = C // GDN_HEADS
    outs = []
    for s, src in enumerate((gq_ref, gk_ref, gv_ref)):
        x = src[...].reshape(G, SUBLANES, C)
        cols = slice(s * C, (s + 1) * C)
        if carry:
            xs_ref = scratch[0].at[s]
            first = pl.program_id(1) == 0

            @pl.when(first)
            def _():
                xs_ref[G] = xprev_ref[0, :, cols]

            xs_ref[0] = xs_ref[G]
            xs_ref[1:G + 1] = x
            xprev = xs_ref[0:G]
            xlast_ref[0, :, cols] = x[G - 1]
        else:
            xprev = xprev_ref[:, :, cols]
            xlast_ref[:, :, cols] = x
        outs.append(_silu(_causal_conv(x, xprev, cw_ref.at[:, cols])).reshape(R, C))
    q_ref[...] = _l2norm_heads(outs[0], dk ** -0.5)
    k_ref[...] = _l2norm_heads(outs[1], 1.0)
    v_ref[...] = outs[2]


def gdn_prep(proj, col_q, xprev, cw, *, n_seq, seq_len, rows):
    C = cw.shape[1] // 3
    carry = seq_len > SUBLANES
    if carry:
        nt = seq_len // rows
        grid = (n_seq, nt)
        row_map = lambda b, t: (b * nt + t)
        nb = 1
        scratch = [pltpu.VMEM((3, rows // SUBLANES + 1, SUBLANES, C), F32)]
    else:
        nb = rows // SUBLANES
        grid = (n_seq // nb, 1)
        row_map = lambda b, t: b
        scratch = []
    tokens = n_seq * seq_len
    in_specs = [pl.BlockSpec((rows, C), lambda b, t, c=c: (row_map(b, t), col_q + c)) for c in range(3)]
    in_specs += [pl.BlockSpec((nb, SUBLANES, 3 * C), lambda b, t: (b, 0, 0)),
                 pl.BlockSpec((CONV_W, 3 * C), lambda b, t: (0, 0))]
    out_shape = [jax.ShapeDtypeStruct((tokens, C), F32)] * 3 + [
        jax.ShapeDtypeStruct((n_seq, SUBLANES, 3 * C), F32)]
    out_specs = [pl.BlockSpec((rows, C), lambda b, t: (row_map(b, t), 0))] * 3 + [
        pl.BlockSpec((nb, SUBLANES, 3 * C), lambda b, t: (b, 0, 0))]
    return pl.pallas_call(
        functools.partial(_gdn_prep_kernel, carry=carry),
        grid=grid, in_specs=in_specs, out_specs=out_specs, out_shape=out_shape,
        scratch_shapes=scratch,
        compiler_params=_params("parallel", "arbitrary"),
        name="gdn_prep",
    )(proj, proj, proj, xprev, cw)


def _cumsum_rows(x):
    n = x.shape[0]
    r_idx = lax.broadcasted_iota(jnp.int32, x.shape, 0)
    s = 1
    while s < n:
        x = x + jnp.where(r_idx >= s, pltpu.roll(x, s, 0), 0.0)
        s *= 2
    return x


def _unit_lower_inverse(m, eye):
    n = m.shape[0]
    x = eye - m
    p = m
    s = 2
    while s < n:
        p = _dot_f32(p, p)
        x = x + _dot_f32(x, p)
        s *= 2
    return x


def _gdn_chunk(q, k, v, beta, gc, s_prev, masks):
    incl, strict, eye = masks
    c = q.shape[0]
    gc_row = jnp.sum(jnp.where(eye > 0, gc, 0.0), axis=0, keepdims=True)
    decay = jnp.where(incl, jnp.exp(gc - gc_row), 0.0)
    kb = k * beta
    vb = v * beta
    m = jnp.where(strict, _dot_nt(kb, k) * decay, 0.0)
    tm = _unit_lower_inverse(m, eye)
    eg = jnp.exp(gc)
    value = _dot(tm, vb)
    kcd = _dot(tm, kb * eg)
    v_new = value - _dot(kcd, s_prev)
    attn = _dot_nt(q, k) * decay
    o = _dot(q * eg, s_prev) + _dot(attn, v_new)
    g_last = gc[c - 1:c, :]
    s_new = s_prev * jnp.exp(g_last) + _dot_tn(k * jnp.exp(g_last - gc), v_new)
    return o, s_new


def _gdn_kernel(q_ref, k_ref, v_ref, z_ref, sm_ref, alog_ref, dtb_ref, ng_ref, s0_ref,
                o_ref, s_ref, *, n_seq, chunk):
    dk = q_ref.shape[1] // GDN_HEADS

    @pl.when(pl.program_id(1) == 0)
    def _():
        s_ref[...] = s0_ref[...]

    r_idx = lax.broadcasted_iota(jnp.int32, (chunk, chunk), 0)
    c_idx = lax.broadcasted_iota(jnp.int32, (chunk, chunk), 1)
    masks = (r_idx >= c_idx, r_idx > c_idx, jnp.where(r_idx == c_idx, 1.0, 0.0).astype(F32))
    neg_a = -jnp.exp(alog_ref[...])
    dtb = dtb_ref[...]
    ng = ng_ref[...]

    def seq_body(s, carry):
        rows = pl.ds(pl.multiple_of(s * chunk, chunk), chunk)
        sm = sm_ref[rows, :]
        beta_all = jax.nn.sigmoid(sm)
        gc_all = _cumsum_rows(neg_a * _softplus(sm + dtb))
        for h in range(GDN_HEADS):
            cols = slice(h * dk, (h + 1) * dk)
            o, s_new = _gdn_chunk(q_ref[rows, cols], k_ref[rows, cols], v_ref[rows, cols],
                                  beta_all[:, h:h + 1], gc_all[:, GDN_HEADS + h:GDN_HEADS + h + 1],
                                  s_ref[s, h], masks)
            s_ref[s, h] = s_new
            o_ref[rows, cols] = _rmsnorm(o, ng) * _silu(z_ref[rows, cols])
        return carry

    lax.fori_loop(0, n_seq, seq_body, 0)


def gdn_core(q, k, v, proj, col_z, small, alog_row, dtb_row, ng, s0, *, n_seq, seq_len, seq_per_step):
    tokens, C = q.shape
    dk = C // GDN_HEADS
    chunk = min(GDN_CHUNK, seq_len)
    nt = seq_len // chunk
    rows = seq_per_step * chunk
    grid = (n_seq // seq_per_step, nt)
    row_map = lambda b, t: (b * nt + t)
    tok_spec = pl.BlockSpec((rows, C), lambda b, t: (row_map(b, t), 0))
    const2 = lambda b, t: (0, 0)
    s_spec = pl.BlockSpec((seq_per_step, GDN_HEADS, dk, dk), lambda b, t: (b, 0, 0, 0))
    in_specs = [tok_spec, tok_spec, tok_spec,
                pl.BlockSpec((rows, C), lambda b, t: (row_map(b, t), col_z)),
                pl.BlockSpec((rows, LANES), lambda b, t: (row_map(b, t), 0)),
                pl.BlockSpec((1, LANES), const2), pl.BlockSpec((1, LANES), const2),
                pl.BlockSpec((1, dk), const2), s_spec]
    return pl.pallas_call(
        functools.partial(_gdn_kernel, n_seq=seq_per_step, chunk=chunk),
        grid=grid, in_specs=in_specs,
        out_specs=[tok_spec, s_spec],
        out_shape=[jax.ShapeDtypeStruct((tokens, C), F32), jax.ShapeDtypeStruct(s0.shape, F32)],
        compiler_params=_params("parallel", "arbitrary"),
        name="gdn_core",
    )(q, k, v, proj, small, alog_row, dtb_row, ng.reshape(1, dk), s0)


def _mem_attn_kernel(q_ref, k_ref, v_ref, o_ref, *, n_seq, tq):
    hd = q_ref.shape[1] // MEM_HEADS
    scale = hd ** -0.5
    for s in range(n_seq):
        rows = slice(s * tq, (s + 1) * tq)
        for h in range(MEM_HEADS):
            cols = slice(h * hd, (h + 1) * hd)
            sc = _dot_nt(q_ref[rows, cols], k_ref[s, :, cols]) * scale
            e = jnp.exp(sc - jnp.max(sc, axis=-1, keepdims=True))
            p = e / jnp.sum(e, axis=-1, keepdims=True)
            o_ref[rows, cols] = _dot(p, v_ref[s, :, cols])


def mem_attention(proj, col_q, mem_k, col_k, mem_v, col_v, *, width, n_seq, seq_len, tq, seq_per_step):
    tokens = n_seq * seq_len
    mem_len = mem_k.shape[1]
    C = width
    nt = seq_len // tq
    rows = seq_per_step * tq
    grid = (n_seq // seq_per_step, nt)
    row_map = lambda b, t: (b * nt + t)
    return pl.pallas_call(
        functools.partial(_mem_attn_kernel, n_seq=seq_per_step, tq=tq),
        grid=grid,
        in_specs=[pl.BlockSpec((rows, C), lambda b, t: (row_map(b, t), col_q)),
                  pl.BlockSpec((seq_per_step, mem_len, C), lambda b, t: (b, 0, col_k)),
                  pl.BlockSpec((seq_per_step, mem_len, C), lambda b, t: (b, 0, col_v))],
        out_specs=pl.BlockSpec((rows, C), lambda b, t: (row_map(b, t), 0)),
        out_shape=jax.ShapeDtypeStruct((tokens, C), F32),
        compiler_params=_params("parallel", "arbitrary"),
        name="mem_attention",
    )(proj, mem_k, mem_v)


def _merge_kernel(yr_ref, yg_ref, ym_ref, g0_ref, g1_ref, g2_ref, w0_ref, w1_ref, w2_ref, o_ref):
    acc = jax.nn.sigmoid(g0_ref[...]) * _dot(yr_ref[...], w0_ref[...])
    acc = acc + jax.nn.sigmoid(g1_ref[...]) * _dot(yg_ref[...], w1_ref[...])
    acc = acc + jax.nn.sigmoid(g2_ref[...]) * _dot(ym_ref[...], w2_ref[...])
    o_ref[...] = acc.astype(BF16)


def merge_branches(y_rnn, y_gdn, y_mem, proj, col_gate, w_rnn_up, w_gdn_up, w_mem_up, *, tm, tn):
    M, C = y_rnn.shape
    N = w_rnn_up.shape[1]
    nj = N // tn
    y_spec = pl.BlockSpec((tm, C), lambda i, j: (i, 0))
    w_spec = pl.BlockSpec((C, tn), lambda i, j: (0, j))
    gate_specs = [pl.BlockSpec((tm, tn), lambda i, j, b=b: (i, col_gate + b * nj + j)) for b in range(N_BRANCH)]
    return pl.pallas_call(
        _merge_kernel,
        grid=(M // tm, nj),
        in_specs=[y_spec, y_spec, y_spec] + gate_specs + [w_spec, w_spec, w_spec],
        out_specs=pl.BlockSpec((tm, tn), lambda i, j: (i, j)),
        out_shape=jax.ShapeDtypeStruct((M, N), BF16),
        compiler_params=_params("parallel", "arbitrary"),
        name="merge_branches",
    )(y_rnn, y_gdn, y_mem, proj, proj, proj, w_rnn_up, w_gdn_up, w_mem_up)


def _matmul_residual_kernel(a_ref, w_ref, x_ref, o_ref):
    o_ref[...] = x_ref[...] + jnp.dot(a_ref[...], w_ref[...], preferred_element_type=F32)


def matmul_residual(a, w, x, *, tm, tn):
    M, K = a.shape
    N = w.shape[1]
    return pl.pallas_call(
        _matmul_residual_kernel,
        grid=(M // tm, N // tn),
        in_specs=[pl.BlockSpec((tm, K), lambda i, j: (i, 0)),
                  pl.BlockSpec((K, tn), lambda i, j: (0, j)),
                  pl.BlockSpec((tm, tn), lambda i, j: (i, j))],
        out_specs=pl.BlockSpec((tm, tn), lambda i, j: (i, j)),
        out_shape=jax.ShapeDtypeStruct((M, N), F32),
        compiler_params=_params("parallel", "arbitrary"),
        name="matmul_residual",
    )(a, w, x)


def _mlp_kernel(x_ref, g_ref, wu_ref, wd_ref, gf_ref, o_ref, xn_ref, acc_ref):
    j = pl.program_id(1)

    @pl.when(j == 0)
    def _():
        xn_ref[...] = _rmsnorm(x_ref[...], g_ref[...]).astype(BF16)
        acc_ref[...] = jnp.zeros_like(acc_ref)

    hid = jnp.dot(xn_ref[...], wu_ref[...], preferred_element_type=F32)
    act = jnp.square(jnp.maximum(hid, 0.0))
    acc_ref[...] += jnp.dot(act.astype(BF16), wd_ref[...], preferred_element_type=F32)

    @pl.when(j == pl.num_programs(1) - 1)
    def _():
        o_ref[...] = _rmsnorm(x_ref[...] + acc_ref[...], gf_ref[...])


def mlp_final_norm(x, g, w_up, w_down, g_final, *, tm, tf):
    M, D = x.shape
    FF = w_up.shape[1]
    return pl.pallas_call(
        _mlp_kernel,
        grid=(M // tm, FF // tf),
        in_specs=[pl.BlockSpec((tm, D), lambda i, j: (i, 0)),
                  pl.BlockSpec((1, D), lambda i, j: (0, 0)),
                  pl.BlockSpec((D, tf), lambda i, j: (0, j)),
                  pl.BlockSpec((tf, D), lambda i, j: (j, 0)),
                  pl.BlockSpec((1, D), lambda i, j: (0, 0))],
        out_specs=pl.BlockSpec((tm, D), lambda i, j: (i, 0)),
        out_shape=jax.ShapeDtypeStruct((M, D), F32),
        scratch_shapes=[pltpu.VMEM((tm, D), BF16), pltpu.VMEM((tm, D), F32)],
        compiler_params=_params("parallel", "arbitrary"),
        name="mlp_final_norm",
    )(x, g.reshape(1, D), w_up, w_down, g_final.reshape(1, D))


def _prep_layer_weights(w_in, rnn_wx, rnn_wa, gdn_A_log, gdn_dt_bias, d_model):
    half = d_model // 2
    sizes = (half, half, half, half, half, half, GDN_HEADS, GDN_HEADS, half, N_BRANCH * d_model)
    offs = [0]
    for s in sizes:
        offs.append(offs[-1] + s)
    col = lambda i: w_in[:, offs[i]:offs[i + 1]]
    rx, rg, gq, gk, gv, gz, gb, ga, mq, mg = (col(i) for i in range(len(sizes)))
    w_main = jnp.concatenate([mg, rx, rg, gq, gk, gv, gz, mq], axis=1).astype(BF16)
    w_small = jnp.pad(jnp.concatenate([gb, ga], axis=1), ((0, 0), (0, LANES - 2 * GDN_HEADS))).astype(BF16)
    w_gate = jnp.concatenate([rnn_wx, rnn_wa], axis=-1).astype(BF16)
    lane_pad = (GDN_HEADS, LANES - 2 * GDN_HEADS)
    alog_row = jnp.pad(gdn_A_log, lane_pad).reshape(1, LANES)
    dtb_row = jnp.pad(gdn_dt_bias, lane_pad).reshape(1, LANES)
    return w_main, w_small, w_gate, alog_row, dtb_row


def _pad_conv_state(buf):
    return jnp.pad(buf, ((0, 0), (SUBLANES - (CONV_W - 1), 0), (0, 0)))


def _group_layer(x, mem_k, col_k, mem_v, col_v, rnn_buf, rnn_h0, gdn_buf, gdn_s0, lw, *, n_seq, seq_len,
                 tm, row_tile, attn_tq, attn_seqs, gdn_seqs, final_g):
    D = x.shape[1]
    half = D // 2
    n_gate_blk = N_BRANCH * D // half
    c_rx, c_rg, c_gq, c_gz, c_mq = (n_gate_blk + i for i in (0, 1, 2, 5, 6))
    proj, small = norm_matmul(x, lw["norm_mix_g"], lw["w_main"], lw["w_small"], tm=tm, tn=512)

    y_rnn, rnn_last, h_last = rglru_branch(
        proj, c_rx, c_rg, _pad_conv_state(rnn_buf), rnn_h0.reshape(n_seq, 1, half),
        lw["rnn_conv_w"], lw["rnn_conv_b"], lw["w_gate"], lw["rnn_bx"], lw["rnn_ba"], lw["rnn_L"],
        n_seq=n_seq, seq_len=seq_len, rows=row_tile)

    q, k, v, gdn_last = gdn_prep(proj, c_gq, _pad_conv_state(gdn_buf), lw["gdn_conv_w"],
                                 n_seq=n_seq, seq_len=seq_len, rows=row_tile)
    y_gdn, s_new = gdn_core(q, k, v, proj, c_gz, small, lw["alog_row"], lw["dtb_row"], lw["gdn_norm_g"],
                            gdn_s0, n_seq=n_seq, seq_len=seq_len, seq_per_step=gdn_seqs)

    y_mem = mem_attention(proj, c_mq, mem_k, col_k, mem_v, col_v, width=half, n_seq=n_seq, seq_len=seq_len,
                          tq=attn_tq, seq_per_step=attn_seqs)

    merged = merge_branches(y_rnn, y_gdn, y_mem, proj, 0, lw["w_rnn_up"], lw["w_gdn_up"], lw["w_mem_up"],
                            tm=tm, tn=512)
    x1 = matmul_residual(merged, lw["w_out"], x, tm=tm, tn=512)
    x2 = mlp_final_norm(x1, lw["norm_mlp_g"], lw["w_mlp_up"], lw["w_mlp_down"], final_g, tm=min(tm, 512), tf=512)
    states = (rnn_last[:, SUBLANES - (CONV_W - 1):], h_last.reshape(n_seq, half),
              gdn_last[:, SUBLANES - (CONV_W - 1):], s_new)
    return x2, states


def kernel(x_prompt, x_sample, mem_prompt, cache_mem_k, cache_mem_v, state_rnn_conv, state_rnn_h,
           state_gdn_conv, state_gdn_S, norm_mix_g, w_in, rnn_conv_w, rnn_conv_b, rnn_wx, rnn_bx, rnn_wa,
           rnn_ba, rnn_L, gdn_conv_w, gdn_A_log, gdn_dt_bias, gdn_norm_g, mem_norm_g, w_mem_kv, w_rnn_up,
           w_gdn_up, w_mem_up, w_out, norm_mlp_g, w_mlp_up, w_mlp_down, norm_final_g):
    depth = w_in.shape[0]
    assert depth == 1, "the final norm is fused into the last layer's MLP kernel; one layer supported"
    Bp, T, D = x_prompt.shape
    Bs, Ts, _ = x_sample.shape
    half = D // 2
    mem_len = mem_prompt.shape[1]
    assert Ts == SUBLANES and T % GDN_CHUNK == 0
    l = 0
    w_main, w_small, w_gate, alog_row, dtb_row = _prep_layer_weights(
        w_in[l], rnn_wx[l], rnn_wa[l], gdn_A_log[l], gdn_dt_bias[l], D)
    lw = dict(norm_mix_g=norm_mix_g[l], w_main=w_main, w_small=w_small, w_gate=w_gate,
              rnn_conv_w=rnn_conv_w[l], rnn_conv_b=rnn_conv_b[l], rnn_bx=rnn_bx[l], rnn_ba=rnn_ba[l],
              rnn_L=rnn_L[l], gdn_conv_w=gdn_conv_w[l], alog_row=alog_row, dtb_row=dtb_row,
              gdn_norm_g=gdn_norm_g[l], w_rnn_up=w_rnn_up[l].astype(BF16), w_gdn_up=w_gdn_up[l].astype(BF16),
              w_mem_up=w_mem_up[l].astype(BF16), w_out=w_out[l].astype(BF16), norm_mlp_g=norm_mlp_g[l],
              w_mlp_up=w_mlp_up[l].astype(BF16), w_mlp_down=w_mlp_down[l].astype(BF16))

    kv = norm_matmul(mem_prompt.reshape(Bp * mem_len, D), mem_norm_g[l], w_mem_kv[l].astype(BF16),
                     tm=min(Bp * mem_len, 1024), tn=512)
    kv3 = kv.reshape(Bp, mem_len, 2 * half)
    zeros = lambda *s: jnp.zeros(s, F32)
    yp, (rb_p, rh_p, gb_p, gs_p) = _group_layer(
        x_prompt.reshape(Bp * T, D), kv3, 0, kv3, 1,
        zeros(Bp, CONV_W - 1, half), zeros(Bp, half), zeros(Bp, CONV_W - 1, 3 * half),
        zeros(Bp, GDN_HEADS, half // GDN_HEADS, half // GDN_HEADS), lw,
        n_seq=Bp, seq_len=T, tm=min(Bp * T, 1024), row_tile=256, attn_tq=512, attn_seqs=1, gdn_seqs=1,
        final_g=norm_final_g)
    mk_p = kv3[:, :, :half].reshape(1, Bp, mem_len, MEM_HEADS, half // MEM_HEADS)
    mv_p = kv3[:, :, half:].reshape(1, Bp, mem_len, MEM_HEADS, half // MEM_HEADS)

    ys, (rb_s, rh_s, gb_s, gs_s) = _group_layer(
        x_sample.reshape(Bs * Ts, D), cache_mem_k[l].reshape(Bs, mem_len, half), 0,
        cache_mem_v[l].reshape(Bs, mem_len, half), 0,
        state_rnn_conv[l], state_rnn_h[l], state_gdn_conv[l], state_gdn_S[l], lw,
        n_seq=Bs, seq_len=Ts, tm=min(Bs * Ts, 1024), row_tile=min(Bs, 16) * SUBLANES, attn_tq=Ts,
        attn_seqs=min(Bs, 4), gdn_seqs=min(Bs, 8), final_g=norm_final_g)

    return (yp.reshape(Bp, T, D), ys.reshape(Bs, Ts, D), mk_p, mv_p, rb_p[None], rh_p[None], gb_p[None],
            gs_p[None], rb_s[None], rh_s[None], gb_s[None], gs_s[None])
```

```python
import functools

import jax
import jax.numpy as jnp
from jax import lax
from jax.experimental import pallas as pl
from jax.experimental.pallas import tpu as pltpu

F32 = jnp.float32
BF16 = jnp.bfloat16

EPS = 1e-6
RG_C = 8.0
CONV_W = 4
RNN_BLOCKS = 8
GDN_HEADS = 8
GDN_CHUNK = 64
MEM_HEADS = 4
N_BRANCH = 3

SUBLANES = 8
LANES = 128
VMEM_LIMIT_BYTES = 56 * 1024 * 1024


def _params(*sem):
    return pltpu.CompilerParams(dimension_semantics=sem, vmem_limit_bytes=VMEM_LIMIT_BYTES)


def _dot(a, b):
    return jnp.dot(a.astype(BF16), b.astype(BF16), preferred_element_type=F32)


def _dot_nt(a, b):
    return lax.dot_general(a.astype(BF16), b.astype(BF16), (((1,), (1,)), ((), ())),
                           preferred_element_type=F32)


def _dot_tn(a, b):
    return lax.dot_general(a.astype(BF16), b.astype(BF16), (((0,), (0,)), ((), ())),
                           preferred_element_type=F32)


def _rmsnorm(x, g):
    return (x * lax.rsqrt(jnp.mean(x * x, axis=-1, keepdims=True) + EPS)) * g


def _softplus(x):
    return jnp.maximum(x, 0.0) + jnp.log1p(jnp.exp(-jnp.abs(x)))


def _silu(x):
    return x * jax.nn.sigmoid(x)


def _norm_matmul_kernel(*refs, has_small):
    if has_small:
        x_ref, g_ref, w_ref, ws_ref, o_ref, os_ref, xn_ref = refs
    else:
        x_ref, g_ref, w_ref, o_ref, xn_ref = refs

    @pl.when(pl.program_id(1) == 0)
    def _():
        xn_ref[...] = _rmsnorm(x_ref[...], g_ref[...]).astype(BF16)
        if has_small:
            os_ref[...] = jnp.dot(xn_ref[...], ws_ref[...], preferred_element_type=F32)

    o_ref[...] = jnp.dot(xn_ref[...], w_ref[...], preferred_element_type=F32)


def norm_matmul(x, g, w, w_small=None, *, tm, tn):
    M, K = x.shape
    N = w.shape[1]
    has_small = w_small is not None
    in_specs = [pl.BlockSpec((tm, K), lambda i, j: (i, 0)),
                pl.BlockSpec((1, K), lambda i, j: (0, 0)),
                pl.BlockSpec((K, tn), lambda i, j: (0, j))]
    out_shape = [jax.ShapeDtypeStruct((M, N), F32)]
    out_specs = [pl.BlockSpec((tm, tn), lambda i, j: (i, j))]
    args = [x, g.reshape(1, K), w]
    if has_small:
        ns = w_small.shape[1]
        in_specs.append(pl.BlockSpec((K, ns), lambda i, j: (0, 0)))
        out_shape.append(jax.ShapeDtypeStruct((M, ns), F32))
        out_specs.append(pl.BlockSpec((tm, ns), lambda i, j: (i, 0)))
        args.append(w_small)
    outs = pl.pallas_call(
        functools.partial(_norm_matmul_kernel, has_small=has_small),
        grid=(M // tm, N // tn),
        in_specs=in_specs, out_specs=out_specs, out_shape=out_shape,
        scratch_shapes=[pltpu.VMEM((tm, K), BF16)],
        compiler_params=_params("parallel", "arbitrary"),
        name="norm_matmul",
    )(*args)
    return outs if has_small else outs[0]


def _causal_conv(x, xprev, w_ref):
    t_idx = lax.broadcasted_iota(jnp.int32, x.shape, 1)
    y = x * w_ref[CONV_W - 1:CONV_W, :][None]
    for k in range(1, CONV_W):
        shifted = jnp.where(t_idx >= k, pltpu.roll(x, k, 1), pltpu.roll(xprev, k, 1))
        y = y + shifted * w_ref[CONV_W - 1 - k:CONV_W - k, :][None]
    return y


def _segment_scan(a, u):
    t_idx = lax.broadcasted_iota(jnp.int32, a.shape, 1)
    s = 1
    while s < SUBLANES:
        keep = t_idx >= s
        u = jnp.where(keep, a * pltpu.roll(u, s, 1) + u, u)
        a = jnp.where(keep, a * pltpu.roll(a, s, 1), a)
        s *= 2
    return a, u


def _stage_groups(x, xprev_ref, xs_ref, first):
    G = x.shape[0]

    @pl.when(first)
    def _():
        xs_ref[G] = xprev_ref[0]

    xs_ref[0] = xs_ref[G]
    xs_ref[1:G + 1] = x
    return xs_ref[0:G]


def _rglru_kernel(rx_ref, rg_ref, xprev_ref, h0_ref, cw_ref, cb_ref, wg_ref, bx_ref, ba_ref, l_ref,
                  y_ref, xlast_ref, hlast_ref, *scratch, carry):
    R, C = rx_ref.shape
    G = R // SUBLANES
    x = rx_ref[...].reshape(G, SUBLANES, C)
    if carry:
        xs_ref, a_ref, u_ref, h_ref, hc_ref = scratch
        first = pl.program_id(1) == 0
        xprev = _stage_groups(x, xprev_ref, xs_ref, first)

        @pl.when(first)
        def _():
            hc_ref[...] = h0_ref[0]
    else:
        xprev = xprev_ref[...]

    xc = (_causal_conv(x, xprev, cw_ref) + cb_ref[...][None]).reshape(R, C)
    xb = xc.astype(BF16)
    bs = C // RNN_BLOCKS
    zi, zr = [], []
    for n in range(RNN_BLOCKS):
        z = jnp.dot(xb[:, n * bs:(n + 1) * bs], wg_ref[n], preferred_element_type=F32)
        zi.append(z[:, :bs])
        zr.append(z[:, bs:])
    gi = jax.nn.sigmoid(jnp.concatenate(zi, axis=1) + bx_ref[...])
    gr = jax.nn.sigmoid(jnp.concatenate(zr, axis=1) + ba_ref[...])
    lv = l_ref[...]
    log_sig_l = -_softplus(-lv)
    log_a = RG_C * gr * log_sig_l
    a = jnp.exp(log_a)
    u = jnp.sqrt(-jnp.tanh(log_a) * (a * a + 1.0)) * (gi * xc)
    a_cum, h_loc = _segment_scan(a.reshape(G, SUBLANES, C), u.reshape(G, SUBLANES, C))

    if carry:
        a_ref[...] = a_cum
        u_ref[...] = h_loc

        def body(g, h_prev):
            hg = u_ref[g] + a_ref[g] * h_prev
            h_ref[g] = hg
            return hg[SUBLANES - 1:SUBLANES, :]

        h_last = lax.fori_loop(0, G, body, hc_ref[...])
        hc_ref[...] = h_last
        h = h_ref[...]
        xlast_ref[0] = x[G - 1]
        hlast_ref[0] = h_last
    else:
        h = h_loc + a_cum * h0_ref[...]
        xlast_ref[...] = x
        hlast_ref[...] = h[:, SUBLANES - 1:SUBLANES, :]

    y = h.reshape(R, C) * jax.nn.gelu(rg_ref[...])
    y_ref[...] = y.astype(BF16)


def rglru_branch(proj, col_rx, col_rg, xprev, h0, cw, cb, wg, bx, ba, lam, *, n_seq, seq_len, rows):
    C = cw.shape[1]
    carry = seq_len > SUBLANES
    if carry:
        nt = seq_len // rows
        grid = (n_seq, nt)
        row_map = lambda b, t: (b * nt + t)
        nb = 1
        G = rows // SUBLANES
        scratch = [pltpu.VMEM((G + 1, SUBLANES, C), F32), pltpu.VMEM((G, SUBLANES, C), F32),
                   pltpu.VMEM((G, SUBLANES, C), F32), pltpu.VMEM((G, SUBLANES, C), F32),
                   pltpu.VMEM((1, C), F32)]
    else:
        nb = rows // SUBLANES
        grid = (n_seq // nb, 1)
        row_map = lambda b, t: b
        scratch = []
    const2 = lambda b, t: (0, 0)
    in_specs = [pl.BlockSpec((rows, C), lambda b, t: (row_map(b, t), col_rx)),
                pl.BlockSpec((rows, C), lambda b, t: (row_map(b, t), col_rg)),
                pl.BlockSpec((nb, SUBLANES, C), lambda b, t: (b, 0, 0)),
                pl.BlockSpec((nb, 1, C), lambda b, t: (b, 0, 0)),
                pl.BlockSpec((CONV_W, C), const2),
                pl.BlockSpec((1, C), const2),
                pl.BlockSpec(wg.shape, lambda b, t: (0, 0, 0)),
                pl.BlockSpec((1, C), const2), pl.BlockSpec((1, C), const2), pl.BlockSpec((1, C), const2)]
    tokens = n_seq * seq_len
    out_shape = [jax.ShapeDtypeStruct((tokens, C), BF16),
                 jax.ShapeDtypeStruct((n_seq, SUBLANES, C), F32),
                 jax.ShapeDtypeStruct((n_seq, 1, C), F32)]
    out_specs = [pl.BlockSpec((rows, C), lambda b, t: (row_map(b, t), 0)),
                 pl.BlockSpec((nb, SUBLANES, C), lambda b, t: (b, 0, 0)),
                 pl.BlockSpec((nb, 1, C), lambda b, t: (b, 0, 0))]
    return pl.pallas_call(
        functools.partial(_rglru_kernel, carry=carry),
        grid=grid, in_specs=in_specs, out_specs=out_specs, out_shape=out_shape,
        scratch_shapes=scratch,
        compiler_params=_params("parallel", "arbitrary"),
        name="rglru",
    )(proj, proj, xprev, h0, cw, cb.reshape(1, C), wg, bx.reshape(1, C), ba.reshape(1, C),
      lam.reshape(1, C))


def _l2norm_heads(x, scale):
    dh = x.shape[1] // GDN_HEADS
    outs = []
    for h in range(GDN_HEADS):
        xh = x[:, h * dh:(h + 1) * dh]
        xh = xh * lax.rsqrt(jnp.sum(xh * xh, axis=-1, keepdims=True) + EPS)
        outs.append(xh * scale if scale != 1.0 else xh)
    return jnp.concatenate(outs, axis=1)


def _gdn_prep_kernel(gq_ref, gk_ref, gv_ref, xprev_ref, cw_ref, q_ref, k_ref, v_ref, xlast_ref,
                     *scratch, carry):
    R, C = gq_ref.shape
    G = R // SUBLANES
    dk = C // GDN_HEADS
    outs = []
    for s, src in enumerate((gq_ref, gk_ref, gv_ref)):
        x = src[...].reshape(G, SUBLANES, C)
        cols = slice(s * C, (s + 1) * C)
        if carry:
            xs_ref = scratch[0].at[s]
            first = pl.program_id(1) == 0

            @pl.when(first)
            def _():
                xs_ref[G] = xprev_ref[0, :, cols]

            xs_ref[0] = xs_ref[G]
            xs_ref[1:G + 1] = x
            xprev = xs_ref[0:G]
            xlast_ref[0, :, cols] = x[G - 1]
        else:
            xprev = xprev_ref[:, :, cols]
            xlast_ref[:, :, cols] = x
        outs.append(_silu(_causal_conv(x, xprev, cw_ref.at[:, cols])).reshape(R, C))
    q_ref[...] = _l2norm_heads(outs[0], dk ** -0.5)
    k_ref[...] = _l2norm_heads(outs[1], 1.0)
    v_ref[...] = outs[2]


def gdn_prep(proj, col_q, xprev, cw, *, n_seq, seq_len, rows):
    C = cw.shape[1] // 3
    carry = seq_len > SUBLANES
    if carry:
        nt = seq_len // rows
        grid = (n_seq, nt)
        row_map = lambda b, t: (b * nt + t)
        nb = 1
        scratch = [pltpu.VMEM((3, rows // SUBLANES + 1, SUBLANES, C), F32)]
    else:
        nb = rows // SUBLANES
        grid = (n_seq // nb, 1)
        row_map = lambda b, t: b
        scratch = []
    tokens = n_seq * seq_len
    in_specs = [pl.BlockSpec((rows, C), lambda b, t, c=c: (row_map(b, t), col_q + c)) for c in range(3)]
    in_specs += [pl.BlockSpec((nb, SUBLANES, 3 * C), lambda b, t: (b, 0, 0)),
                 pl.BlockSpec((CONV_W, 3 * C), lambda b, t: (0, 0))]
    out_shape = [jax.ShapeDtypeStruct((tokens, C), F32)] * 3 + [
        jax.ShapeDtypeStruct((n_seq, SUBLANES, 3 * C), F32)]
    out_specs = [pl.BlockSpec((rows, C), lambda b, t: (row_map(b, t), 0))] * 3 + [
        pl.BlockSpec((nb, SUBLANES, 3 * C), lambda b, t: (b, 0, 0))]
    return pl.pallas_call(
        functools.partial(_gdn_prep_kernel, carry=carry),
        grid=grid, in_specs=in_specs, out_specs=out_specs, out_shape=out_shape,
        scratch_shapes=scratch,
        compiler_params=_params("parallel", "arbitrary"),
        name="gdn_prep",
    )(proj, proj, proj, xprev, cw)


def _segment_cumsum(x, seg):
    pos = lax.broadcasted_iota(jnp.int32, x.shape, 0) & (seg - 1)
    s = 1
    while s < seg:
        x = x + jnp.where(pos >= s, pltpu.roll(x, s, 0), 0.0)
        s *= 2
    return x


def _segment_last(x, seg):
    n = x.shape[0]
    pos = lax.broadcasted_iota(jnp.int32, x.shape, 0) & (seg - 1)
    s = seg // 2
    while s >= 1:
        x = jnp.where((pos & (2 * s - 1)) < s, pltpu.roll(x, n - s, 0), x)
        s //= 2
    return x


class _TileMasks:
    def __init__(self, rows, seg):
        r = lax.broadcasted_iota(jnp.int32, (rows, rows), 0)
        c = lax.broadcasted_iota(jnp.int32, (rows, rows), 1)
        shift = seg.bit_length() - 1
        same = (r >> shift) == (c >> shift)
        self.incl = same & (r >= c)
        self.strict = same & (r > c)
        self.eye = r == c
        self.levels = []
        s = 1
        while s < seg:
            b = s.bit_length() - 1
            self.levels.append(((r >> (b + 1)) == (c >> (b + 1))) & (((r >> b) & 1) == 1) & (((c >> b) & 1) == 0))
            s *= 2


def _unit_lower_inverses(ms, masks):
    eye = jnp.where(masks.eye, 1.0, 0.0)
    xs = [eye - jnp.where(masks.levels[0], m, 0.0) for m in ms]
    for level in masks.levels[1:]:
        ts = [_dot(jnp.where(level, m, 0.0), x) for m, x in zip(ms, xs)]
        xs = [x - _dot(x, t) for x, t in zip(xs, ts)]
    return xs


def _gdn_tiles_local(chains, masks):
    r, d = chains[0][0].shape
    kbs, decays, kk_qks = [], [], []
    for q, k, v, beta, gc, g_last in chains:
        gc_row = jnp.sum(jnp.where(masks.eye, gc, 0.0), axis=0, keepdims=True)
        decays.append(jnp.where(masks.incl, jnp.exp(gc - gc_row), 0.0))
        kbs.append(k * beta)
        kk_qks.append(_dot_nt(jnp.concatenate([kbs[-1], q], axis=0), k))
    ms = [jnp.where(masks.strict, kq[:r] * dec, 0.0) for kq, dec in zip(kk_qks, decays)]
    tms = _unit_lower_inverses(ms, masks)
    out = []
    for (q, k, v, beta, gc, g_last), kb, dec, kq, tm in zip(chains, kbs, decays, kk_qks, tms):
        eg = jnp.exp(gc)
        vk = _dot(tm, jnp.concatenate([v * beta, kb * eg], axis=1))
        out.append(dict(value=vk[:, :d], kcd=vk[:, d:], attn=kq[r:] * dec, qg=q * eg,
                        kd=k * jnp.exp(g_last - gc), decay_last=jnp.exp(g_last)))
    return out


def _gdn_kernel(q_ref, k_ref, v_ref, z_ref, sm_ref, alog_ref, dtb_ref, ng_ref, s0_ref,
                o_ref, s_ref, *, tile, seg, n_tiles, carry):
    dk = q_ref.shape[1] // GDN_HEADS
    n_seg = tile // seg
    masks = _TileMasks(tile, seg)
    neg_a = -jnp.exp(alog_ref[...])
    dtb = dtb_ref[...]
    ng = ng_ref[...]
    if carry:
        @pl.when(pl.program_id(1) == 0)
        def _():
            s_ref[...] = s0_ref[...]

    chains = []
    for i in range(n_tiles):
        rows = slice(i * tile, (i + 1) * tile)
        sm = sm_ref[rows, :]
        beta_all = jax.nn.sigmoid(sm)
        gc_all = _segment_cumsum(neg_a * _softplus(sm + dtb), seg)
        gl_all = _segment_last(gc_all, seg)
        for h in range(GDN_HEADS):
            cols = slice(h * dk, (h + 1) * dk)
            lane = slice(GDN_HEADS + h, GDN_HEADS + h + 1)
            chains.append((q_ref[rows, cols], k_ref[rows, cols], v_ref[rows, cols],
                           beta_all[:, h:h + 1], gc_all[:, lane], gl_all[:, lane]))
    local = _gdn_tiles_local(chains, masks)

    heads = range(GDN_HEADS)
    for i in range(n_tiles):
        rows = slice(i * tile, (i + 1) * tile)
        loc = local[i * GDN_HEADS:(i + 1) * GDN_HEADS]
        v_new = [[] for _ in heads]
        o_state = [[] for _ in heads]
        for j in range(n_seg):
            sl = slice(j * seg, (j + 1) * seg)
            seq = 0 if carry else i * n_seg + j
            s_prev = [s_ref[seq, h] if carry else s0_ref[seq, h] for h in heads]
            rs = [_dot(jnp.concatenate([loc[h]["kcd"][sl], loc[h]["qg"][sl]], axis=0), s_prev[h])
                  for h in heads]
            vns = [loc[h]["value"][sl] - rs[h][:seg] for h in heads]
            for h in heads:
                v_new[h].append(vns[h])
                o_state[h].append(rs[h][seg:])
                s_ref[seq, h] = (s_prev[h] * loc[h]["decay_last"][j * seg:j * seg + 1, :]
                                 + _dot_tn(loc[h]["kd"][sl], vns[h]))
        for h in heads:
            cols = slice(h * dk, (h + 1) * dk)
            vn = jnp.concatenate(v_new[h], axis=0) if n_seg > 1 else v_new[h][0]
            os_ = jnp.concatenate(o_state[h], axis=0) if n_seg > 1 else o_state[h][0]
            o = os_ + _dot(loc[h]["attn"], vn)
            o_ref[rows, cols] = _rmsnorm(o, ng) * _silu(z_ref[rows, cols])


def gdn_core(q, k, v, proj, col_z, small, alog_row, dtb_row, ng, s0, *, n_seq, seq_len, tiles_per_step):
    tokens, C = q.shape
    dk = C // GDN_HEADS
    tile = GDN_CHUNK
    carry = seq_len >= GDN_CHUNK
    seg = GDN_CHUNK if carry else seq_len
    rows = tiles_per_step * tile
    if carry:
        nt = seq_len // rows
        seq_per_step = 1
        grid = (n_seq, nt)
    else:
        nt = 1
        seq_per_step = rows // seg
        grid = (n_seq // seq_per_step, 1)
    row_map = lambda b, t: (b * nt + t)
    tok_spec = pl.BlockSpec((rows, C), lambda b, t: (row_map(b, t), 0))
    const2 = lambda b, t: (0, 0)
    s_spec = pl.BlockSpec((seq_per_step, GDN_HEADS, dk, dk), lambda b, t: (b, 0, 0, 0))
    in_specs = [tok_spec, tok_spec, tok_spec,
                pl.BlockSpec((rows, C), lambda b, t: (row_map(b, t), col_z)),
                pl.BlockSpec((rows, LANES), lambda b, t: (row_map(b, t), 0)),
                pl.BlockSpec((1, LANES), const2), pl.BlockSpec((1, LANES), const2),
                pl.BlockSpec((1, dk), const2), s_spec]
    return pl.pallas_call(
        functools.partial(_gdn_kernel, tile=tile, seg=seg, n_tiles=tiles_per_step, carry=carry),
        grid=grid, in_specs=in_specs,
        out_specs=[tok_spec, s_spec],
        out_shape=[jax.ShapeDtypeStruct((tokens, C), F32), jax.ShapeDtypeStruct(s0.shape, F32)],
        compiler_params=_params("parallel", "arbitrary"),
        name="gdn_core",
    )(q, k, v, proj, small, alog_row, dtb_row, ng.reshape(1, dk), s0)


def _mem_attn_kernel(q_ref, k_ref, v_ref, o_ref, *, n_seq, tq):
    hd = q_ref.shape[1] // MEM_HEADS
    scale = hd ** -0.5
    for s in range(n_seq):
        rows = slice(s * tq, (s + 1) * tq)
        for h in range(MEM_HEADS):
            cols = slice(h * hd, (h + 1) * hd)
            sc = _dot_nt(q_ref[rows, cols], k_ref[s, :, cols]) * scale
            e = jnp.exp(sc - jnp.max(sc, axis=-1, keepdims=True))
            p = e / jnp.sum(e, axis=-1, keepdims=True)
            o_ref[rows, cols] = _dot(p, v_ref[s, :, cols])


def mem_attention(proj, col_q, mem_k, col_k, mem_v, col_v, *, width, n_seq, seq_len, tq, seq_per_step):
    tokens = n_seq * seq_len
    mem_len = mem_k.shape[1]
    C = width
    nt = seq_len // tq
    rows = seq_per_step * tq
    grid = (n_seq // seq_per_step, nt)
    row_map = lambda b, t: (b * nt + t)
    return pl.pallas_call(
        functools.partial(_mem_attn_kernel, n_seq=seq_per_step, tq=tq),
        grid=grid,
        in_specs=[pl.BlockSpec((rows, C), lambda b, t: (row_map(b, t), col_q)),
                  pl.BlockSpec((seq_per_step, mem_len, C), lambda b, t: (b, 0, col_k)),
                  pl.BlockSpec((seq_per_step, mem_len, C), lambda b, t: (b, 0, col_v))],
        out_specs=pl.BlockSpec((rows, C), lambda b, t: (row_map(b, t), 0)),
        out_shape=jax.ShapeDtypeStruct((tokens, C), F32),
        compiler_params=_params("parallel", "arbitrary"),
        name="mem_attention",
    )(proj, mem_k, mem_v)


def _merge_kernel(yr_ref, yg_ref, ym_ref, g0_ref, g1_ref, g2_ref, w0_ref, w1_ref, w2_ref, o_ref):
    acc = jax.nn.sigmoid(g0_ref[...]) * _dot(yr_ref[...], w0_ref[...])
    acc = acc + jax.nn.sigmoid(g1_ref[...]) * _dot(yg_ref[...], w1_ref[...])
    acc = acc + jax.nn.sigmoid(g2_ref[...]) * _dot(ym_ref[...], w2_ref[...])
    o_ref[...] = acc.astype(BF16)


def merge_branches(y_rnn, y_gdn, y_mem, proj, col_gate, w_rnn_up, w_gdn_up, w_mem_up, *, tm, tn):
    M, C = y_rnn.shape
    N = w_rnn_up.shape[1]
    nj = N // tn
    y_spec = pl.BlockSpec((tm, C), lambda i, j: (i, 0))
    w_spec = pl.BlockSpec((C, tn), lambda i, j: (0, j))
    gate_specs = [pl.BlockSpec((tm, tn), lambda i, j, b=b: (i, col_gate + b * nj + j)) for b in range(N_BRANCH)]
    return pl.pallas_call(
        _merge_kernel,
        grid=(M // tm, nj),
        in_specs=[y_spec, y_spec, y_spec] + gate_specs + [w_spec, w_spec, w_spec],
        out_specs=pl.BlockSpec((tm, tn), lambda i, j: (i, j)),
        out_shape=jax.ShapeDtypeStruct((M, N), BF16),
        compiler_params=_params("parallel", "arbitrary"),
        name="merge_branches",
    )(y_rnn, y_gdn, y_mem, proj, proj, proj, w_rnn_up, w_gdn_up, w_mem_up)


def _matmul_residual_kernel(a_ref, w_ref, x_ref, o_ref):
    o_ref[...] = x_ref[...] + jnp.dot(a_ref[...], w_ref[...], preferred_element_type=F32)


def matmul_residual(a, w, x, *, tm, tn):
    M, K = a.shape
    N = w.shape[1]
    return pl.pallas_call(
        _matmul_residual_kernel,
        grid=(M // tm, N // tn),
        in_specs=[pl.BlockSpec((tm, K), lambda i, j: (i, 0)),
                  pl.BlockSpec((K, tn), lambda i, j: (0, j)),
                  pl.BlockSpec((tm, tn), lambda i, j: (i, j))],
        out_specs=pl.BlockSpec((tm, tn), lambda i, j: (i, j)),
        out_shape=jax.ShapeDtypeStruct((M, N), F32),
        compiler_params=_params("parallel", "arbitrary"),
        name="matmul_residual",
    )(a, w, x)


def _mlp_kernel(x_ref, g_ref, wu_ref, wd_ref, gf_ref, o_ref, xn_ref, acc_ref):
    j = pl.program_id(1)

    @pl.when(j == 0)
    def _():
        xn_ref[...] = _rmsnorm(x_ref[...], g_ref[...]).astype(BF16)
        acc_ref[...] = jnp.zeros_like(acc_ref)

    hid = jnp.dot(xn_ref[...], wu_ref[...], preferred_element_type=F32)
    act = jnp.square(jnp.maximum(hid, 0.0))
    acc_ref[...] += jnp.dot(act.astype(BF16), wd_ref[...], preferred_element_type=F32)

    @pl.when(j == pl.num_programs(1) - 1)
    def _():
        o_ref[...] = _rmsnorm(x_ref[...] + acc_ref[...], gf_ref[...])


def mlp_final_norm(x, g, w_up, w_down, g_final, *, tm, tf):
    M, D = x.shape
    FF = w_up.shape[1]
    return pl.pallas_call(
        _mlp_kernel,
        grid=(M // tm, FF // tf),
        in_specs=[pl.BlockSpec((tm, D), lambda i, j: (i, 0)),
                  pl.BlockSpec((1, D), lambda i, j: (0, 0)),
                  pl.BlockSpec((D, tf), lambda i, j: (0, j)),
                  pl.BlockSpec((tf, D), lambda i, j: (j, 0)),
                  pl.BlockSpec((1, D), lambda i, j: (0, 0))],
        out_specs=pl.BlockSpec((tm, D), lambda i, j: (i, 0)),
        out_shape=jax.ShapeDtypeStruct((M, D), F32),
        scratch_shapes=[pltpu.VMEM((tm, D), BF16), pltpu.VMEM((tm, D), F32)],
        compiler_params=_params("parallel", "arbitrary"),
        name="mlp_final_norm",
    )(x, g.reshape(1, D), w_up, w_down, g_final.reshape(1, D))


def _prep_layer_weights(w_in, rnn_wx, rnn_wa, gdn_A_log, gdn_dt_bias, d_model):
    half = d_model // 2
    sizes = (half, half, half, half, half, half, GDN_HEADS, GDN_HEADS, half, N_BRANCH * d_model)
    offs = [0]
    for s in sizes:
        offs.append(offs[-1] + s)
    col = lambda i: w_in[:, offs[i]:offs[i + 1]]
    rx, rg, gq, gk, gv, gz, gb, ga, mq, mg = (col(i) for i in range(len(sizes)))
    w_main = jnp.concatenate([mg, rx, rg, gq, gk, gv, gz, mq], axis=1).astype(BF16)
    w_small = jnp.pad(jnp.concatenate([gb, ga], axis=1), ((0, 0), (0, LANES - 2 * GDN_HEADS))).astype(BF16)
    w_gate = jnp.concatenate([rnn_wx, rnn_wa], axis=-1).astype(BF16)
    lane_pad = (GDN_HEADS, LANES - 2 * GDN_HEADS)
    alog_row = jnp.pad(gdn_A_log, lane_pad).reshape(1, LANES)
    dtb_row = jnp.pad(gdn_dt_bias, lane_pad).reshape(1, LANES)
    return w_main, w_small, w_gate, alog_row, dtb_row


def _pad_conv_state(buf):
    return jnp.pad(buf, ((0, 0), (SUBLANES - (CONV_W - 1), 0), (0, 0)))


def _group_layer(x, mem_k, col_k, mem_v, col_v, rnn_buf, rnn_h0, gdn_buf, gdn_s0, lw, *, n_seq, seq_len,
                 tm, row_tile, attn_tq, attn_seqs, gdn_tiles, final_g):
    D = x.shape[1]
    half = D // 2
    n_gate_blk = N_BRANCH * D // half
    c_rx, c_rg, c_gq, c_gz, c_mq = (n_gate_blk + i for i in (0, 1, 2, 5, 6))
    proj, small = norm_matmul(x, lw["norm_mix_g"], lw["w_main"], lw["w_small"], tm=tm, tn=512)

    y_rnn, rnn_last, h_last = rglru_branch(
        proj, c_rx, c_rg, _pad_conv_state(rnn_buf), rnn_h0.reshape(n_seq, 1, half),
        lw["rnn_conv_w"], lw["rnn_conv_b"], lw["w_gate"], lw["rnn_bx"], lw["rnn_ba"], lw["rnn_L"],
        n_seq=n_seq, seq_len=seq_len, rows=row_tile)

    q, k, v, gdn_last = gdn_prep(proj, c_gq, _pad_conv_state(gdn_buf), lw["gdn_conv_w"],
                                 n_seq=n_seq, seq_len=seq_len, rows=row_tile)
    y_gdn, s_new = gdn_core(q, k, v, proj, c_gz, small, lw["alog_row"], lw["dtb_row"], lw["gdn_norm_g"],
                            gdn_s0, n_seq=n_seq, seq_len=seq_len, tiles_per_step=gdn_tiles)

    y_mem = mem_attention(proj, c_mq, mem_k, col_k, mem_v, col_v, width=half, n_seq=n_seq, seq_len=seq_len,
                          tq=attn_tq, seq_per_step=attn_seqs)

    merged = merge_branches(y_rnn, y_gdn, y_mem, proj, 0, lw["w_rnn_up"], lw["w_gdn_up"], lw["w_mem_up"],
                            tm=tm, tn=512)
    x1 = matmul_residual(merged, lw["w_out"], x, tm=tm, tn=512)
    x2 = mlp_final_norm(x1, lw["norm_mlp_g"], lw["w_mlp_up"], lw["w_mlp_down"], final_g, tm=min(tm, 512), tf=512)
    states = (rnn_last[:, SUBLANES - (CONV_W - 1):], h_last.reshape(n_seq, half),
              gdn_last[:, SUBLANES - (CONV_W - 1):], s_new)
    return x2, states


def kernel(x_prompt, x_sample, mem_prompt, cache_mem_k, cache_mem_v, state_rnn_conv, state_rnn_h,
           state_gdn_conv, state_gdn_S, norm_mix_g, w_in, rnn_conv_w, rnn_conv_b, rnn_wx, rnn_bx, rnn_wa,
           rnn_ba, rnn_L, gdn_conv_w, gdn_A_log, gdn_dt_bias, gdn_norm_g, mem_norm_g, w_mem_kv, w_rnn_up,
           w_gdn_up, w_mem_up, w_out, norm_mlp_g, w_mlp_up, w_mlp_down, norm_final_g):
    depth = w_in.shape[0]
    assert depth == 1, "the final norm is fused into the last layer's MLP kernel; one layer supported"
    Bp, T, D = x_prompt.shape
    Bs, Ts, _ = x_sample.shape
    half = D // 2
    mem_len = mem_prompt.shape[1]
    assert Ts == SUBLANES and T % GDN_CHUNK == 0
    l = 0
    w_main, w_small, w_gate, alog_row, dtb_row = _prep_layer_weights(
        w_in[l], rnn_wx[l], rnn_wa[l], gdn_A_log[l], gdn_dt_bias[l], D)
    lw = dict(norm_mix_g=norm_mix_g[l], w_main=w_main, w_small=w_small, w_gate=w_gate,
              rnn_conv_w=rnn_conv_w[l], rnn_conv_b=rnn_conv_b[l], rnn_bx=rnn_bx[l], rnn_ba=rnn_ba[l],
              rnn_L=rnn_L[l], gdn_conv_w=gdn_conv_w[l], alog_row=alog_row, dtb_row=dtb_row,
              gdn_norm_g=gdn_norm_g[l], w_rnn_up=w_rnn_up[l].astype(BF16), w_gdn_up=w_gdn_up[l].astype(BF16),
              w_mem_up=w_mem_up[l].astype(BF16), w_out=w_out[l].astype(BF16), norm_mlp_g=norm_mlp_g[l],
              w_mlp_up=w_mlp_up[l].astype(BF16), w_mlp_down=w_mlp_down[l].astype(BF16))

    kv = norm_matmul(mem_prompt.reshape(Bp * mem_len, D), mem_norm_g[l], w_mem_kv[l].astype(BF16),
                     tm=min(Bp * mem_len, 1024), tn=512)
    kv3 = kv.reshape(Bp, mem_len, 2 * half)
    zeros = lambda *s: jnp.zeros(s, F32)
    yp, (rb_p, rh_p, gb_p, gs_p) = _group_layer(
        x_prompt.reshape(Bp * T, D), kv3, 0, kv3, 1,
        zeros(Bp, CONV_W - 1, half), zeros(Bp, half), zeros(Bp, CONV_W - 1, 3 * half),
        zeros(Bp, GDN_HEADS, half // GDN_HEADS, half // GDN_HEADS), lw,
        n_seq=Bp, seq_len=T, tm=min(Bp * T, 1024), row_tile=256, attn_tq=512, attn_seqs=1, gdn_tiles=2,
        final_g=norm_final_g)
    mk_p = kv3[:, :, :half].reshape(1, Bp, mem_len, MEM_HEADS, half // MEM_HEADS)
    mv_p = kv3[:, :, half:].reshape(1, Bp, mem_len, MEM_HEADS, half // MEM_HEADS)

    ys, (rb_s, rh_s, gb_s, gs_s) = _group_layer(
        x_sample.reshape(Bs * Ts, D), cache_mem_k[l].reshape(Bs, mem_len, half), 0,
        cache_mem_v[l].reshape(Bs, mem_len, half), 0,
        state_rnn_conv[l], state_rnn_h[l], state_gdn_conv[l], state_gdn_S[l], lw,
        n_seq=Bs, seq_len=Ts, tm=min(Bs * Ts, 1024), row_tile=min(Bs, 16) * SUBLANES, attn_tq=Ts,
        attn_seqs=min(Bs, 4), gdn_tiles=1, final_g=norm_final_g)

    return (yp.reshape(Bp, T, D), ys.reshape(Bs, Ts, D), mk_p, mv_p, rb_p[None], rh_p[None], gb_p[None],
            gs_p[None], rb_s[None], rh_s[None], gb_s[None], gs_s[None])
```

```python
import functools

import jax
import jax.numpy as jnp
from jax import lax
from jax.experimental import pallas as pl
from jax.experimental.pallas import tpu as pltpu

F32 = jnp.float32
BF16 = jnp.bfloat16

EPS = 1e-6
RG_C = 8.0
CONV_W = 4
RNN_BLOCKS = 8
GDN_HEADS = 8
GDN_CHUNK = 64
MEM_HEADS = 4
N_BRANCH = 3

SUBLANES = 8
LANES = 128
VMEM_LIMIT_BYTES = 56 * 1024 * 1024


def _params(*sem):
    return pltpu.CompilerParams(dimension_semantics=sem, vmem_limit_bytes=VMEM_LIMIT_BYTES)


def _dot(a, b):
    return jnp.dot(a.astype(BF16), b.astype(BF16), preferred_element_type=F32)


def _dot_nt(a, b):
    return lax.dot_general(a.astype(BF16), b.astype(BF16), (((1,), (1,)), ((), ())),
                           preferred_element_type=F32)


def _dot_tn(a, b):
    return lax.dot_general(a.astype(BF16), b.astype(BF16), (((0,), (0,)), ((), ())),
                           preferred_element_type=F32)


def _rmsnorm(x, g):
    return (x * lax.rsqrt(jnp.mean(x * x, axis=-1, keepdims=True) + EPS)) * g


def _softplus(x):
    return jnp.maximum(x, 0.0) + jnp.log1p(jnp.exp(-jnp.abs(x)))


def _silu(x):
    return x * jax.nn.sigmoid(x)


def _norm_matmul_kernel(*refs, has_small):
    if has_small:
        x_ref, g_ref, w_ref, ws_ref, o_ref, os_ref, xn_ref = refs
    else:
        x_ref, g_ref, w_ref, o_ref, xn_ref = refs

    @pl.when(pl.program_id(1) == 0)
    def _():
        xn_ref[...] = _rmsnorm(x_ref[...], g_ref[...]).astype(BF16)
        if has_small:
            os_ref[...] = jnp.dot(xn_ref[...], ws_ref[...], preferred_element_type=F32)

    o_ref[...] = jnp.dot(xn_ref[...], w_ref[...], preferred_element_type=F32)


def norm_matmul(x, g, w, w_small=None, *, tm, tn):
    M, K = x.shape
    N = w.shape[1]
    has_small = w_small is not None
    in_specs = [pl.BlockSpec((tm, K), lambda i, j: (i, 0)),
                pl.BlockSpec((1, K), lambda i, j: (0, 0)),
                pl.BlockSpec((K, tn), lambda i, j: (0, j))]
    out_shape = [jax.ShapeDtypeStruct((M, N), F32)]
    out_specs = [pl.BlockSpec((tm, tn), lambda i, j: (i, j))]
    args = [x, g.reshape(1, K), w]
    if has_small:
        ns = w_small.shape[1]
        in_specs.append(pl.BlockSpec((K, ns), lambda i, j: (0, 0)))
        out_shape.append(jax.ShapeDtypeStruct((M, ns), F32))
        out_specs.append(pl.BlockSpec((tm, ns), lambda i, j: (i, 0)))
        args.append(w_small)
    outs = pl.pallas_call(
        functools.partial(_norm_matmul_kernel, has_small=has_small),
        grid=(M // tm, N // tn),
        in_specs=in_specs, out_specs=out_specs, out_shape=out_shape,
        scratch_shapes=[pltpu.VMEM((tm, K), BF16)],
        compiler_params=_params("parallel", "arbitrary"),
        name="norm_matmul",
    )(*args)
    return outs if has_small else outs[0]


def _causal_conv(x, xprev, w_ref):
    t_idx = lax.broadcasted_iota(jnp.int32, x.shape, 1)
    y = x * w_ref[CONV_W - 1:CONV_W, :][None]
    for k in range(1, CONV_W):
        shifted = jnp.where(t_idx >= k, pltpu.roll(x, k, 1), pltpu.roll(xprev, k, 1))
        y = y + shifted * w_ref[CONV_W - 1 - k:CONV_W - k, :][None]
    return y


def _segment_scan(a, u):
    t_idx = lax.broadcasted_iota(jnp.int32, a.shape, 1)
    s = 1
    while s < SUBLANES:
        keep = t_idx >= s
        u = jnp.where(keep, a * pltpu.roll(u, s, 1) + u, u)
        a = jnp.where(keep, a * pltpu.roll(a, s, 1), a)
        s *= 2
    return a, u


def _stage_groups(x, xprev_ref, xs_ref, first):
    G = x.shape[0]

    @pl.when(first)
    def _():
        xs_ref[G] = xprev_ref[0]

    xs_ref[0] = xs_ref[G]
    xs_ref[1:G + 1] = x
    return xs_ref[0:G]


def _rglru_kernel(rx_ref, rg_ref, xprev_ref, h0_ref, cw_ref, cb_ref, wg_ref, bx_ref, ba_ref, l_ref,
                  y_ref, xlast_ref, hlast_ref, *scratch, carry):
    R, C = rx_ref.shape
    G = R // SUBLANES
    x = rx_ref[...].reshape(G, SUBLANES, C)
    if carry:
        xs_ref, a_ref, u_ref, h_ref, hc_ref = scratch
        first = pl.program_id(1) == 0
        xprev = _stage_groups(x, xprev_ref, xs_ref, first)

        @pl.when(first)
        def _():
            hc_ref[...] = h0_ref[0]
    else:
        xprev = xprev_ref[...]

    xc = (_causal_conv(x, xprev, cw_ref) + cb_ref[...][None]).reshape(R, C)
    xb = xc.astype(BF16)
    bs = C // RNN_BLOCKS
    zi, zr = [], []
    for n in range(RNN_BLOCKS):
        z = jnp.dot(xb[:, n * bs:(n + 1) * bs], wg_ref[n], preferred_element_type=F32)
        zi.append(z[:, :bs])
        zr.append(z[:, bs:])
    gi = jax.nn.sigmoid(jnp.concatenate(zi, axis=1) + bx_ref[...])
    gr = jax.nn.sigmoid(jnp.concatenate(zr, axis=1) + ba_ref[...])
    lv = l_ref[...]
    log_sig_l = -_softplus(-lv)
    log_a = RG_C * gr * log_sig_l
    a = jnp.exp(log_a)
    u = jnp.sqrt(-jnp.tanh(log_a) * (a * a + 1.0)) * (gi * xc)
    a_cum, h_loc = _segment_scan(a.reshape(G, SUBLANES, C), u.reshape(G, SUBLANES, C))

    if carry:
        a_ref[...] = a_cum
        u_ref[...] = h_loc

        def body(g, h_prev):
            hg = u_ref[g] + a_ref[g] * h_prev
            h_ref[g] = hg
            return hg[SUBLANES - 1:SUBLANES, :]

        h_last = lax.fori_loop(0, G, body, hc_ref[...])
        hc_ref[...] = h_last
        h = h_ref[...]
        xlast_ref[0] = x[G - 1]
        hlast_ref[0] = h_last
    else:
        h = h_loc + a_cum * h0_ref[...]
        xlast_ref[...] = x
        hlast_ref[...] = h[:, SUBLANES - 1:SUBLANES, :]

    y = h.reshape(R, C) * jax.nn.gelu(rg_ref[...])
    y_ref[...] = y.astype(BF16)


def rglru_branch(proj, col_rx, col_rg, xprev, h0, cw, cb, wg, bx, ba, lam, *, n_seq, seq_len, rows):
    C = cw.shape[1]
    carry = seq_len > SUBLANES
    if carry:
        nt = seq_len // rows
        grid = (n_seq, nt)
        row_map = lambda b, t: (b * nt + t)
        nb = 1
        G = rows // SUBLANES
        scratch = [pltpu.VMEM((G + 1, SUBLANES, C), F32), pltpu.VMEM((G, SUBLANES, C), F32),
                   pltpu.VMEM((G, SUBLANES, C), F32), pltpu.VMEM((G, SUBLANES, C), F32),
                   pltpu.VMEM((1, C), F32)]
    else:
        nb = rows // SUBLANES
        grid = (n_seq // nb, 1)
        row_map = lambda b, t: b
        scratch = []
    const2 = lambda b, t: (0, 0)
    in_specs = [pl.BlockSpec((rows, C), lambda b, t: (row_map(b, t), col_rx)),
                pl.BlockSpec((rows, C), lambda b, t: (row_map(b, t), col_rg)),
                pl.BlockSpec((nb, SUBLANES, C), lambda b, t: (b, 0, 0)),
                pl.BlockSpec((nb, 1, C), lambda b, t: (b, 0, 0)),
                pl.BlockSpec((CONV_W, C), const2),
                pl.BlockSpec((1, C), const2),
                pl.BlockSpec(wg.shape, lambda b, t: (0, 0, 0)),
                pl.BlockSpec((1, C), const2), pl.BlockSpec((1, C), const2), pl.BlockSpec((1, C), const2)]
    tokens = n_seq * seq_len
    out_shape = [jax.ShapeDtypeStruct((tokens, C), BF16),
                 jax.ShapeDtypeStruct((n_seq, SUBLANES, C), F32),
                 jax.ShapeDtypeStruct((n_seq, 1, C), F32)]
    out_specs = [pl.BlockSpec((rows, C), lambda b, t: (row_map(b, t), 0)),
                 pl.BlockSpec((nb, SUBLANES, C), lambda b, t: (b, 0, 0)),
                 pl.BlockSpec((nb, 1, C), lambda b, t: (b, 0, 0))]
    return pl.pallas_call(
        functools.partial(_rglru_kernel, carry=carry),
        grid=grid, in_specs=in_specs, out_specs=out_specs, out_shape=out_shape,
        scratch_shapes=scratch,
        compiler_params=_params("parallel", "arbitrary"),
        name="rglru",
    )(proj, proj, xprev, h0, cw, cb.reshape(1, C), wg, bx.reshape(1, C), ba.reshape(1, C),
      lam.reshape(1, C))


def _l2norm_heads(x, scale):
    dh = x.shape[1] // GDN_HEADS
    outs = []
    for h in range(GDN_HEADS):
        xh = x[:, h * dh:(h + 1) * dh]
        xh = xh * lax.rsqrt(jnp.sum(xh * xh, axis=-1, keepdims=True) + EPS)
        outs.append(xh * scale if scale != 1.0 else xh)
    return jnp.concatenate(outs, axis=1)


def _gdn_prep_kernel(gq_ref, gk_ref, gv_ref, xprev_ref, cw_ref, q_ref, k_ref, v_ref, xlast_ref,
                     *scratch, carry):
    R, C = gq_ref.shape
    G = R // SUBLANES
    dk = C // GDN_HEADS
    outs = []
    for s, src in enumerate((gq_ref, gk_ref, gv_ref)):
        x = src[...].reshape(G, SUBLANES, C)
        cols = slice(s * C, (s + 1) * C)
        if carry:
            xs_ref = scratch[0].at[s]
            first = pl.program_id(1) == 0

            @pl.when(first)
            def _():
                xs_ref[G] = xprev_ref[0, :, cols]

            xs_ref[0] = xs_ref[G]
            xs_ref[1:G + 1] = x
            xprev = xs_ref[0:G]
            xlast_ref[0, :, cols] = x[G - 1]
        else:
            xprev = xprev_ref[:, :, cols]
            xlast_ref[:, :, cols] = x
        outs.append(_silu(_causal_conv(x, xprev, cw_ref.at[:, cols])).reshape(R, C))
    q_ref[...] = _l2norm_heads(outs[0], dk ** -0.5)
    k_ref[...] = _l2norm_heads(outs[1], 1.0)
    v_ref[...] = outs[2]


def gdn_prep(proj, col_q, xprev, cw, *, n_seq, seq_len, rows):
    C = cw.shape[1] // 3
    carry = seq_len > SUBLANES
    if carry:
        nt = seq_len // rows
        grid = (n_seq, nt)
        row_map = lambda b, t: (b * nt + t)
        nb = 1
        scratch = [pltpu.VMEM((3, rows // SUBLANES + 1, SUBLANES, C), F32)]
    else:
        nb = rows // SUBLANES
        grid = (n_seq // nb, 1)
        row_map = lambda b, t: b
        scratch = []
    tokens = n_seq * seq_len
    in_specs = [pl.BlockSpec((rows, C), lambda b, t, c=c: (row_map(b, t), col_q + c)) for c in range(3)]
    in_specs += [pl.BlockSpec((nb, SUBLANES, 3 * C), lambda b, t: (b, 0, 0)),
                 pl.BlockSpec((CONV_W, 3 * C), lambda b, t: (0, 0))]
    out_shape = [jax.ShapeDtypeStruct((tokens, C), F32)] * 3 + [
        jax.ShapeDtypeStruct((n_seq, SUBLANES, 3 * C), F32)]
    out_specs = [pl.BlockSpec((rows, C), lambda b, t: (row_map(b, t), 0))] * 3 + [
        pl.BlockSpec((nb, SUBLANES, 3 * C), lambda b, t: (b, 0, 0))]
    return pl.pallas_call(
        functools.partial(_gdn_prep_kernel, carry=carry),
        grid=grid, in_specs=in_specs, out_specs=out_specs, out_shape=out_shape,
        scratch_shapes=scratch,
        compiler_params=_params("parallel", "arbitrary"),
        name="gdn_prep",
    )(proj, proj, proj, xprev, cw)


def _segment_cumsum(x, seg):
    pos = lax.broadcasted_iota(jnp.int32, x.shape, 0) & (seg - 1)
    s = 1
    while s < seg:
        x = x + jnp.where(pos >= s, pltpu.roll(x, s, 0), 0.0)
        s *= 2
    return x


def _segment_last(x, seg):
    n = x.shape[0]
    pos = lax.broadcasted_iota(jnp.int32, x.shape, 0) & (seg - 1)
    s = seg // 2
    while s >= 1:
        x = jnp.where((pos & (2 * s - 1)) < s, pltpu.roll(x, n - s, 0), x)
        s //= 2
    return x


class _TileMasks:
    def __init__(self, rows, seg):
        r = lax.broadcasted_iota(jnp.int32, (rows, rows), 0)
        c = lax.broadcasted_iota(jnp.int32, (rows, rows), 1)
        shift = seg.bit_length() - 1
        same = (r >> shift) == (c >> shift)
        self.incl = same & (r >= c)
        self.strict = same & (r > c)
        self.eye = r == c
        self.levels = []
        s = 1
        while s < seg:
            b = s.bit_length() - 1
            self.levels.append(((r >> (b + 1)) == (c >> (b + 1))) & (((r >> b) & 1) == 1) & (((c >> b) & 1) == 0))
            s *= 2


def _unit_lower_inverses(ms, masks):
    eye = jnp.where(masks.eye, 1.0, 0.0)
    xs = [eye - jnp.where(masks.levels[0], m, 0.0) for m in ms]
    for level in masks.levels[1:]:
        ts = [_dot(jnp.where(level, m, 0.0), x) for m, x in zip(ms, xs)]
        xs = [x - _dot(x, t) for x, t in zip(xs, ts)]
    return xs


def _gdn_tiles_local(chains, masks):
    r, d = chains[0][0].shape
    kbs, decays, kk_qks = [], [], []
    for q, k, v, beta, gc, g_last in chains:
        gc_row = jnp.sum(jnp.where(masks.eye, gc, 0.0), axis=0, keepdims=True)
        decays.append(jnp.where(masks.incl, jnp.exp(gc - gc_row), 0.0))
        kbs.append(k * beta)
        kk_qks.append(_dot_nt(jnp.concatenate([kbs[-1], q], axis=0), k))
    ms = [jnp.where(masks.strict, kq[:r] * dec, 0.0) for kq, dec in zip(kk_qks, decays)]
    tms = _unit_lower_inverses(ms, masks)
    out = []
    for (q, k, v, beta, gc, g_last), kb, dec, kq, tm in zip(chains, kbs, decays, kk_qks, tms):
        eg = jnp.exp(gc)
        vk = _dot(tm, jnp.concatenate([v * beta, kb * eg], axis=1))
        out.append(dict(value=vk[:, :d], kcd=vk[:, d:], attn=kq[r:] * dec, qg=q * eg,
                        kd=k * jnp.exp(g_last - gc), decay_last=jnp.exp(g_last)))
    return out


def _gdn_kernel(q_ref, k_ref, v_ref, z_ref, sm_ref, alog_ref, dtb_ref, ng_ref, s0_ref,
                o_ref, s_ref, *, tile, seg, n_tiles, carry):
    dk = q_ref.shape[1] // GDN_HEADS
    n_seg = tile // seg
    masks = _TileMasks(tile, seg)
    neg_a = -jnp.exp(alog_ref[...])
    dtb = dtb_ref[...]
    ng = ng_ref[...]
    if carry:
        @pl.when(pl.program_id(1) == 0)
        def _():
            s_ref[...] = s0_ref[...]

    chains = []
    for i in range(n_tiles):
        rows = slice(i * tile, (i + 1) * tile)
        sm = sm_ref[rows, :]
        beta_all = jax.nn.sigmoid(sm)
        gc_all = _segment_cumsum(neg_a * _softplus(sm + dtb), seg)
        gl_all = _segment_last(gc_all, seg)
        for h in range(GDN_HEADS):
            cols = slice(h * dk, (h + 1) * dk)
            lane = slice(GDN_HEADS + h, GDN_HEADS + h + 1)
            chains.append((q_ref[rows, cols], k_ref[rows, cols], v_ref[rows, cols],
                           beta_all[:, h:h + 1], gc_all[:, lane], gl_all[:, lane]))
    local = _gdn_tiles_local(chains, masks)

    heads = range(GDN_HEADS)
    for i in range(n_tiles):
        rows = slice(i * tile, (i + 1) * tile)
        loc = local[i * GDN_HEADS:(i + 1) * GDN_HEADS]
        v_new = [[] for _ in heads]
        o_state = [[] for _ in heads]
        for j in range(n_seg):
            sl = slice(j * seg, (j + 1) * seg)
            seq = 0 if carry else i * n_seg + j
            s_prev = [s_ref[seq, h] if carry else s0_ref[seq, h] for h in heads]
            rs = [_dot(jnp.concatenate([loc[h]["kcd"][sl], loc[h]["qg"][sl]], axis=0), s_prev[h])
                  for h in heads]
            vns = [loc[h]["value"][sl] - rs[h][:seg] for h in heads]
            for h in heads:
                v_new[h].append(vns[h])
                o_state[h].append(rs[h][seg:])
                s_ref[seq, h] = (s_prev[h] * loc[h]["decay_last"][j * seg:j * seg + 1, :]
                                 + _dot_tn(loc[h]["kd"][sl], vns[h]))
        for h in heads:
            cols = slice(h * dk, (h + 1) * dk)
            vn = jnp.concatenate(v_new[h], axis=0) if n_seg > 1 else v_new[h][0]
            os_ = jnp.concatenate(o_state[h], axis=0) if n_seg > 1 else o_state[h][0]
            o = os_ + _dot(loc[h]["attn"], vn)
            o_ref[rows, cols] = _rmsnorm(o, ng) * _silu(z_ref[rows, cols])


def gdn_core(q, k, v, proj, col_z, small, alog_row, dtb_row, ng, s0, *, n_seq, seq_len, tiles_per_step):
    tokens, C = q.shape
    dk = C // GDN_HEADS
    tile = GDN_CHUNK
    carry = seq_len >= GDN_CHUNK
    seg = GDN_CHUNK if carry else seq_len
    rows = tiles_per_step * tile
    if carry:
        nt = seq_len // rows
        seq_per_step = 1
        grid = (n_seq, nt)
    else:
        nt = 1
        seq_per_step = rows // seg
        grid = (n_seq // seq_per_step, 1)
    row_map = lambda b, t: (b * nt + t)
    tok_spec = pl.BlockSpec((rows, C), lambda b, t: (row_map(b, t), 0))
    const2 = lambda b, t: (0, 0)
    s_spec = pl.BlockSpec((seq_per_step, GDN_HEADS, dk, dk), lambda b, t: (b, 0, 0, 0))
    in_specs = [tok_spec, tok_spec, tok_spec,
                pl.BlockSpec((rows, C), lambda b, t: (row_map(b, t), col_z)),
                pl.BlockSpec((rows, LANES), lambda b, t: (row_map(b, t), 0)),
                pl.BlockSpec((1, LANES), const2), pl.BlockSpec((1, LANES), const2),
                pl.BlockSpec((1, dk), const2), s_spec]
    return pl.pallas_call(
        functools.partial(_gdn_kernel, tile=tile, seg=seg, n_tiles=tiles_per_step, carry=carry),
        grid=grid, in_specs=in_specs,
        out_specs=[tok_spec, s_spec],
        out_shape=[jax.ShapeDtypeStruct((tokens, C), F32), jax.ShapeDtypeStruct(s0.shape, F32)],
        compiler_params=_params("parallel", "arbitrary"),
        name="gdn_core",
    )(q, k, v, proj, small, alog_row, dtb_row, ng.reshape(1, dk), s0)


def _mem_attn_kernel(q_ref, k_ref, v_ref, o_ref, *, n_seq, tq):
    hd = q_ref.shape[1] // MEM_HEADS
    scale = hd ** -0.5
    rows = lambda s: slice(s * tq, (s + 1) * tq)
    cols = lambda h: slice(h * hd, (h + 1) * hd)
    if len(k_ref.shape) == 4:
        mem_rows = k_ref.shape[1] * MEM_HEADS
        r_head = lax.broadcasted_iota(jnp.int32, (MEM_HEADS * tq, mem_rows), 0) // tq
        c_head = lax.broadcasted_iota(jnp.int32, (MEM_HEADS * tq, mem_rows), 1) & (MEM_HEADS - 1)
        own = r_head == c_head
        scores = []
        for s in range(n_seq):
            q_all = jnp.concatenate([q_ref[rows(s), cols(h)] for h in range(MEM_HEADS)], axis=0)
            sc = _dot_nt(q_all, k_ref[s].reshape(mem_rows, hd)) * scale
            scores.append(jnp.where(own, sc, -jnp.inf))
        probs = []
        for sc in scores:
            e = jnp.exp(sc - jnp.max(sc, axis=-1, keepdims=True))
            probs.append(e / jnp.sum(e, axis=-1, keepdims=True))
        for s, p in enumerate(probs):
            o_all = _dot(p, v_ref[s].reshape(mem_rows, hd))
            for h in range(MEM_HEADS):
                o_ref[rows(s), cols(h)] = o_all[h * tq:(h + 1) * tq]
        return
    pairs = [(s, h) for s in range(n_seq) for h in range(MEM_HEADS)]
    mem = lambda ref, s, h: ref[s, :, cols(h)]
    scores = [_dot_nt(q_ref[rows(s), cols(h)], mem(k_ref, s, h)) * scale for s, h in pairs]
    probs = []
    for sc in scores:
        e = jnp.exp(sc - jnp.max(sc, axis=-1, keepdims=True))
        probs.append(e / jnp.sum(e, axis=-1, keepdims=True))
    for (s, h), p in zip(pairs, probs):
        o_ref[rows(s), cols(h)] = _dot(p, mem(v_ref, s, h))


def mem_attention(proj, col_q, mem_k, col_k, mem_v, col_v, *, width, n_seq, seq_len, tq, seq_per_step):
    tokens = n_seq * seq_len
    mem_len = mem_k.shape[1]
    C = width
    nt = seq_len // tq
    rows = seq_per_step * tq
    grid = (n_seq // seq_per_step, nt)
    row_map = lambda b, t: (b * nt + t)
    if mem_k.ndim == 4:
        blk = (seq_per_step,) + mem_k.shape[1:]
        k_spec = v_spec = pl.BlockSpec(blk, lambda b, t: (b, 0, 0, 0))
    else:
        k_spec = pl.BlockSpec((seq_per_step, mem_len, C), lambda b, t: (b, 0, col_k))
        v_spec = pl.BlockSpec((seq_per_step, mem_len, C), lambda b, t: (b, 0, col_v))
    return pl.pallas_call(
        functools.partial(_mem_attn_kernel, n_seq=seq_per_step, tq=tq),
        grid=grid,
        in_specs=[pl.BlockSpec((rows, C), lambda b, t: (row_map(b, t), col_q)), k_spec, v_spec],
        out_specs=pl.BlockSpec((rows, C), lambda b, t: (row_map(b, t), 0)),
        out_shape=jax.ShapeDtypeStruct((tokens, C), F32),
        compiler_params=_params("parallel", "arbitrary"),
        name="mem_attention",
    )(proj, mem_k, mem_v)


def _merge_kernel(yr_ref, yg_ref, ym_ref, g0_ref, g1_ref, g2_ref, w0_ref, w1_ref, w2_ref, o_ref):
    acc = jax.nn.sigmoid(g0_ref[...]) * _dot(yr_ref[...], w0_ref[...])
    acc = acc + jax.nn.sigmoid(g1_ref[...]) * _dot(yg_ref[...], w1_ref[...])
    acc = acc + jax.nn.sigmoid(g2_ref[...]) * _dot(ym_ref[...], w2_ref[...])
    o_ref[...] = acc.astype(BF16)


def merge_branches(y_rnn, y_gdn, y_mem, proj, col_gate, w_rnn_up, w_gdn_up, w_mem_up, *, tm, tn):
    M, C = y_rnn.shape
    N = w_rnn_up.shape[1]
    nj = N // tn
    y_spec = pl.BlockSpec((tm, C), lambda i, j: (i, 0))
    w_spec = pl.BlockSpec((C, tn), lambda i, j: (0, j))
    gate_specs = [pl.BlockSpec((tm, tn), lambda i, j, b=b: (i, col_gate + b * nj + j)) for b in range(N_BRANCH)]
    return pl.pallas_call(
        _merge_kernel,
        grid=(M // tm, nj),
        in_specs=[y_spec, y_spec, y_spec] + gate_specs + [w_spec, w_spec, w_spec],
        out_specs=pl.BlockSpec((tm, tn), lambda i, j: (i, j)),
        out_shape=jax.ShapeDtypeStruct((M, N), BF16),
        compiler_params=_params("parallel", "arbitrary"),
        name="merge_branches",
    )(y_rnn, y_gdn, y_mem, proj, proj, proj, w_rnn_up, w_gdn_up, w_mem_up)


def _matmul_residual_kernel(a_ref, w_ref, x_ref, o_ref):
    o_ref[...] = x_ref[...] + jnp.dot(a_ref[...], w_ref[...], preferred_element_type=F32)


def matmul_residual(a, w, x, *, tm, tn):
    M, K = a.shape
    N = w.shape[1]
    return pl.pallas_call(
        _matmul_residual_kernel,
        grid=(M // tm, N // tn),
        in_specs=[pl.BlockSpec((tm, K), lambda i, j: (i, 0)),
                  pl.BlockSpec((K, tn), lambda i, j: (0, j)),
                  pl.BlockSpec((tm, tn), lambda i, j: (i, j))],
        out_specs=pl.BlockSpec((tm, tn), lambda i, j: (i, j)),
        out_shape=jax.ShapeDtypeStruct((M, N), F32),
        compiler_params=_params("parallel", "arbitrary"),
        name="matmul_residual",
    )(a, w, x)


def _mlp_kernel(x_ref, g_ref, wu_ref, wd_ref, gf_ref, o_ref, xn_ref, acc_ref):
    j = pl.program_id(1)

    @pl.when(j == 0)
    def _():
        xn_ref[...] = _rmsnorm(x_ref[...], g_ref[...]).astype(BF16)
        acc_ref[...] = jnp.zeros_like(acc_ref)

    hid = jnp.dot(xn_ref[...], wu_ref[...], preferred_element_type=F32)
    act = jnp.square(jnp.maximum(hid, 0.0))
    acc_ref[...] += jnp.dot(act.astype(BF16), wd_ref[...], preferred_element_type=F32)

    @pl.when(j == pl.num_programs(1) - 1)
    def _():
        o_ref[...] = _rmsnorm(x_ref[...] + acc_ref[...], gf_ref[...])


def mlp_final_norm(x, g, w_up, w_down, g_final, *, tm, tf):
    M, D = x.shape
    FF = w_up.shape[1]
    return pl.pallas_call(
        _mlp_kernel,
        grid=(M // tm, FF // tf),
        in_specs=[pl.BlockSpec((tm, D), lambda i, j: (i, 0)),
                  pl.BlockSpec((1, D), lambda i, j: (0, 0)),
                  pl.BlockSpec((D, tf), lambda i, j: (0, j)),
                  pl.BlockSpec((tf, D), lambda i, j: (j, 0)),
                  pl.BlockSpec((1, D), lambda i, j: (0, 0))],
        out_specs=pl.BlockSpec((tm, D), lambda i, j: (i, 0)),
        out_shape=jax.ShapeDtypeStruct((M, D), F32),
        scratch_shapes=[pltpu.VMEM((tm, D), BF16), pltpu.VMEM((tm, D), F32)],
        compiler_params=_params("parallel", "arbitrary"),
        name="mlp_final_norm",
    )(x, g.reshape(1, D), w_up, w_down, g_final.reshape(1, D))


def _prep_layer_weights(w_in, rnn_wx, rnn_wa, gdn_A_log, gdn_dt_bias, d_model):
    half = d_model // 2
    sizes = (half, half, half, half, half, half, GDN_HEADS, GDN_HEADS, half, N_BRANCH * d_model)
    offs = [0]
    for s in sizes:
        offs.append(offs[-1] + s)
    col = lambda i: w_in[:, offs[i]:offs[i + 1]]
    rx, rg, gq, gk, gv, gz, gb, ga, mq, mg = (col(i) for i in range(len(sizes)))
    w_main = jnp.concatenate([mg, rx, rg, gq, gk, gv, gz, mq], axis=1).astype(BF16)
    w_small = jnp.pad(jnp.concatenate([gb, ga], axis=1), ((0, 0), (0, LANES - 2 * GDN_HEADS))).astype(BF16)
    w_gate = jnp.concatenate([rnn_wx, rnn_wa], axis=-1).astype(BF16)
    lane_pad = (GDN_HEADS, LANES - 2 * GDN_HEADS)
    alog_row = jnp.pad(gdn_A_log, lane_pad).reshape(1, LANES)
    dtb_row = jnp.pad(gdn_dt_bias, lane_pad).reshape(1, LANES)
    return w_main, w_small, w_gate, alog_row, dtb_row


def _pad_conv_state(buf):
    return jnp.pad(buf, ((0, 0), (SUBLANES - (CONV_W - 1), 0), (0, 0)))


def _group_layer(x, mem_k, col_k, mem_v, col_v, rnn_buf, rnn_h0, gdn_buf, gdn_s0, lw, *, n_seq, seq_len,
                 tm, row_tile, attn_tq, attn_seqs, gdn_tiles, final_g):
    D = x.shape[1]
    half = D // 2
    n_gate_blk = N_BRANCH * D // half
    c_rx, c_rg, c_gq, c_gz, c_mq = (n_gate_blk + i for i in (0, 1, 2, 5, 6))
    proj, small = norm_matmul(x, lw["norm_mix_g"], lw["w_main"], lw["w_small"], tm=tm, tn=512)

    y_rnn, rnn_last, h_last = rglru_branch(
        proj, c_rx, c_rg, _pad_conv_state(rnn_buf), rnn_h0.reshape(n_seq, 1, half),
        lw["rnn_conv_w"], lw["rnn_conv_b"], lw["w_gate"], lw["rnn_bx"], lw["rnn_ba"], lw["rnn_L"],
        n_seq=n_seq, seq_len=seq_len, rows=row_tile)

    q, k, v, gdn_last = gdn_prep(proj, c_gq, _pad_conv_state(gdn_buf), lw["gdn_conv_w"],
                                 n_seq=n_seq, seq_len=seq_len, rows=row_tile)
    y_gdn, s_new = gdn_core(q, k, v, proj, c_gz, small, lw["alog_row"], lw["dtb_row"], lw["gdn_norm_g"],
                            gdn_s0, n_seq=n_seq, seq_len=seq_len, tiles_per_step=gdn_tiles)

    y_mem = mem_attention(proj, c_mq, mem_k, col_k, mem_v, col_v, width=half, n_seq=n_seq, seq_len=seq_len,
                          tq=attn_tq, seq_per_step=attn_seqs)

    merged = merge_branches(y_rnn, y_gdn, y_mem, proj, 0, lw["w_rnn_up"], lw["w_gdn_up"], lw["w_mem_up"],
                            tm=tm, tn=512)
    x1 = matmul_residual(merged, lw["w_out"], x, tm=tm, tn=512)
    x2 = mlp_final_norm(x1, lw["norm_mlp_g"], lw["w_mlp_up"], lw["w_mlp_down"], final_g, tm=min(tm, 512), tf=512)
    states = (rnn_last[:, SUBLANES - (CONV_W - 1):], h_last.reshape(n_seq, half),
              gdn_last[:, SUBLANES - (CONV_W - 1):], s_new)
    return x2, states


def kernel(x_prompt, x_sample, mem_prompt, cache_mem_k, cache_mem_v, state_rnn_conv, state_rnn_h,
           state_gdn_conv, state_gdn_S, norm_mix_g, w_in, rnn_conv_w, rnn_conv_b, rnn_wx, rnn_bx, rnn_wa,
           rnn_ba, rnn_L, gdn_conv_w, gdn_A_log, gdn_dt_bias, gdn_norm_g, mem_norm_g, w_mem_kv, w_rnn_up,
           w_gdn_up, w_mem_up, w_out, norm_mlp_g, w_mlp_up, w_mlp_down, norm_final_g):
    depth = w_in.shape[0]
    assert depth == 1, "the final norm is fused into the last layer's MLP kernel; one layer supported"
    Bp, T, D = x_prompt.shape
    Bs, Ts, _ = x_sample.shape
    half = D // 2
    mem_len = mem_prompt.shape[1]
    assert Ts == SUBLANES and T % GDN_CHUNK == 0
    l = 0
    w_main, w_small, w_gate, alog_row, dtb_row = _prep_layer_weights(
        w_in[l], rnn_wx[l], rnn_wa[l], gdn_A_log[l], gdn_dt_bias[l], D)
    lw = dict(norm_mix_g=norm_mix_g[l], w_main=w_main, w_small=w_small, w_gate=w_gate,
              rnn_conv_w=rnn_conv_w[l], rnn_conv_b=rnn_conv_b[l], rnn_bx=rnn_bx[l], rnn_ba=rnn_ba[l],
              rnn_L=rnn_L[l], gdn_conv_w=gdn_conv_w[l], alog_row=alog_row, dtb_row=dtb_row,
              gdn_norm_g=gdn_norm_g[l], w_rnn_up=w_rnn_up[l].astype(BF16), w_gdn_up=w_gdn_up[l].astype(BF16),
              w_mem_up=w_mem_up[l].astype(BF16), w_out=w_out[l].astype(BF16), norm_mlp_g=norm_mlp_g[l],
              w_mlp_up=w_mlp_up[l].astype(BF16), w_mlp_down=w_mlp_down[l].astype(BF16))

    kv = norm_matmul(mem_prompt.reshape(Bp * mem_len, D), mem_norm_g[l], w_mem_kv[l].astype(BF16),
                     tm=min(Bp * mem_len, 1024), tn=512)
    kv3 = kv.reshape(Bp, mem_len, 2 * half)
    zeros = lambda *s: jnp.zeros(s, F32)
    yp, (rb_p, rh_p, gb_p, gs_p) = _group_layer(
        x_prompt.reshape(Bp * T, D), kv3, 0, kv3, 1,
        zeros(Bp, CONV_W - 1, half), zeros(Bp, half), zeros(Bp, CONV_W - 1, 3 * half),
        zeros(Bp, GDN_HEADS, half // GDN_HEADS, half // GDN_HEADS), lw,
        n_seq=Bp, seq_len=T, tm=min(Bp * T, 1024), row_tile=256, attn_tq=512, attn_seqs=1, gdn_tiles=2,
        final_g=norm_final_g)
    mk_p = kv3[:, :, :half].reshape(1, Bp, mem_len, MEM_HEADS, half // MEM_HEADS)
    mv_p = kv3[:, :, half:].reshape(1, Bp, mem_len, MEM_HEADS, half // MEM_HEADS)

    ys, (rb_s, rh_s, gb_s, gs_s) = _group_layer(
        x_sample.reshape(Bs * Ts, D), cache_mem_k.reshape((depth * Bs,) + cache_mem_k.shape[2:]), 0,
        cache_mem_v.reshape((depth * Bs,) + cache_mem_v.shape[2:]), 0,
        state_rnn_conv[l], state_rnn_h[l], state_gdn_conv[l], state_gdn_S[l], lw,
        n_seq=Bs, seq_len=Ts, tm=min(Bs * Ts, 1024), row_tile=min(Bs, 16) * SUBLANES, attn_tq=Ts,
        attn_seqs=min(Bs, 4), gdn_tiles=1, final_g=norm_final_g)

    return (yp.reshape(Bp, T, D), ys.reshape(Bs, Ts, D), mk_p, mv_p, rb_p[None], rh_p[None], gb_p[None],
            gs_p[None], rb_s[None], rh_s[None], gb_s[None], gs_s[None])
```

```python
import functools

import jax
import jax.numpy as jnp
from jax import lax
from jax.experimental import pallas as pl
from jax.experimental.pallas import tpu as pltpu

F32 = jnp.float32
BF16 = jnp.bfloat16

EPS = 1e-6
RG_C = 8.0
CONV_W = 4
RNN_BLOCKS = 8
GDN_HEADS = 8
GDN_CHUNK = 64
MEM_HEADS = 4
N_BRANCH = 3

SUBLANES = 8
LANES = 128
VMEM_LIMIT_BYTES = 56 * 1024 * 1024


def _params(*sem):
    return pltpu.CompilerParams(dimension_semantics=sem, vmem_limit_bytes=VMEM_LIMIT_BYTES)


def _dot(a, b):
    return jnp.dot(a.astype(BF16), b.astype(BF16), preferred_element_type=F32)


def _dot_nt(a, b):
    return lax.dot_general(a.astype(BF16), b.astype(BF16), (((1,), (1,)), ((), ())),
                           preferred_element_type=F32)


def _dot_tn(a, b):
    return lax.dot_general(a.astype(BF16), b.astype(BF16), (((0,), (0,)), ((), ())),
                           preferred_element_type=F32)


def _rmsnorm(x, g):
    return (x * lax.rsqrt(jnp.mean(x * x, axis=-1, keepdims=True) + EPS)) * g


def _softplus(x):
    return jnp.maximum(x, 0.0) + jnp.log1p(jnp.exp(-jnp.abs(x)))


def _silu(x):
    return x * jax.nn.sigmoid(x)


def _norm_matmul_kernel(*refs, has_small):
    if has_small:
        x_ref, g_ref, w_ref, ws_ref, o_ref, os_ref, xn_ref = refs
    else:
        x_ref, g_ref, w_ref, o_ref, xn_ref = refs

    @pl.when(pl.program_id(1) == 0)
    def _():
        xn_ref[...] = _rmsnorm(x_ref[...], g_ref[...]).astype(BF16)
        if has_small:
            os_ref[...] = jnp.dot(xn_ref[...], ws_ref[...], preferred_element_type=F32)

    o_ref[...] = _dot(xn_ref[...], w_ref[...])


def norm_matmul(x, g, w, w_small=None, *, tm, tn):
    M, K = x.shape
    N = w.shape[1]
    has_small = w_small is not None
    in_specs = [pl.BlockSpec((tm, K), lambda i, j: (i, 0)),
                pl.BlockSpec((1, K), lambda i, j: (0, 0)),
                pl.BlockSpec((K, tn), lambda i, j: (0, j))]
    out_shape = [jax.ShapeDtypeStruct((M, N), F32)]
    out_specs = [pl.BlockSpec((tm, tn), lambda i, j: (i, j))]
    args = [x, g.reshape(1, K), w]
    if has_small:
        ns = w_small.shape[1]
        in_specs.append(pl.BlockSpec((K, ns), lambda i, j: (0, 0)))
        out_shape.append(jax.ShapeDtypeStruct((M, ns), F32))
        out_specs.append(pl.BlockSpec((tm, ns), lambda i, j: (i, 0)))
        args.append(w_small)
    outs = pl.pallas_call(
        functools.partial(_norm_matmul_kernel, has_small=has_small),
        grid=(M // tm, N // tn),
        in_specs=in_specs, out_specs=out_specs, out_shape=out_shape,
        scratch_shapes=[pltpu.VMEM((tm, K), BF16)],
        compiler_params=_params("parallel", "arbitrary"),
        name="norm_matmul",
    )(*args)
    return outs if has_small else outs[0]


def _causal_conv(x, xprev, w_ref):
    t_idx = lax.broadcasted_iota(jnp.int32, x.shape, 1)
    y = x * w_ref[CONV_W - 1:CONV_W, :][None]
    for k in range(1, CONV_W):
        shifted = jnp.where(t_idx >= k, pltpu.roll(x, k, 1), pltpu.roll(xprev, k, 1))
        y = y + shifted * w_ref[CONV_W - 1 - k:CONV_W - k, :][None]
    return y


def _segment_scan(a, u):
    t_idx = lax.broadcasted_iota(jnp.int32, a.shape, 1)
    s = 1
    while s < SUBLANES:
        keep = t_idx >= s
        u = jnp.where(keep, a * pltpu.roll(u, s, 1) + u, u)
        a = jnp.where(keep, a * pltpu.roll(a, s, 1), a)
        s *= 2
    return a, u


def _stage_groups(x, xprev_ref, xs_ref, first):
    G = x.shape[0]

    @pl.when(first)
    def _():
        xs_ref[G] = xprev_ref[0]

    xs_ref[0] = xs_ref[G]
    xs_ref[1:G + 1] = x
    return xs_ref[0:G]


def _rglru_kernel(rx_ref, rg_ref, xprev_ref, h0_ref, cw_ref, cb_ref, wg_ref, bx_ref, ba_ref, l_ref,
                  y_ref, xlast_ref, hlast_ref, *scratch, carry):
    R, C = rx_ref.shape
    G = R // SUBLANES
    x = rx_ref[...].reshape(G, SUBLANES, C)
    if carry:
        xs_ref, a_ref, u_ref, h_ref, hc_ref = scratch
        first = pl.program_id(1) == 0
        xprev = _stage_groups(x, xprev_ref, xs_ref, first)

        @pl.when(first)
        def _():
            hc_ref[...] = h0_ref[0]
    else:
        xprev = xprev_ref[...]

    xc = (_causal_conv(x, xprev, cw_ref) + cb_ref[...][None]).reshape(R, C)
    xb = xc.astype(BF16)
    bs = C // RNN_BLOCKS
    zi, zr = [], []
    for n in range(RNN_BLOCKS):
        z = jnp.dot(xb[:, n * bs:(n + 1) * bs], wg_ref[n], preferred_element_type=F32)
        zi.append(z[:, :bs])
        zr.append(z[:, bs:])
    gi = jax.nn.sigmoid(jnp.concatenate(zi, axis=1) + bx_ref[...])
    gr = jax.nn.sigmoid(jnp.concatenate(zr, axis=1) + ba_ref[...])
    lv = l_ref[...]
    log_sig_l = -_softplus(-lv)
    log_a = RG_C * gr * log_sig_l
    a = jnp.exp(log_a)
    u = jnp.sqrt(-jnp.tanh(log_a) * (a * a + 1.0)) * (gi * xc)
    a_cum, h_loc = _segment_scan(a.reshape(G, SUBLANES, C), u.reshape(G, SUBLANES, C))

    if carry:
        a_ref[...] = a_cum
        u_ref[...] = h_loc

        def body(g, h_prev):
            hg = u_ref[g] + a_ref[g] * h_prev
            h_ref[g] = hg
            return hg[SUBLANES - 1:SUBLANES, :]

        h_last = lax.fori_loop(0, G, body, hc_ref[...])
        hc_ref[...] = h_last
        h = h_ref[...]
        xlast_ref[0] = x[G - 1]
        hlast_ref[0] = h_last
    else:
        h = h_loc + a_cum * h0_ref[...]
        xlast_ref[...] = x
        hlast_ref[...] = h[:, SUBLANES - 1:SUBLANES, :]

    y = h.reshape(R, C) * jax.nn.gelu(rg_ref[...])
    y_ref[...] = y.astype(BF16)


def rglru_branch(proj, col_rx, col_rg, xprev, h0, cw, cb, wg, bx, ba, lam, *, n_seq, seq_len, rows):
    C = cw.shape[1]
    carry = seq_len > SUBLANES
    if carry:
        nt = seq_len // rows
        grid = (n_seq, nt)
        row_map = lambda b, t: (b * nt + t)
        nb = 1
        G = rows // SUBLANES
        scratch = [pltpu.VMEM((G + 1, SUBLANES, C), F32), pltpu.VMEM((G, SUBLANES, C), F32),
                   pltpu.VMEM((G, SUBLANES, C), F32), pltpu.VMEM((G, SUBLANES, C), F32),
                   pltpu.VMEM((1, C), F32)]
    else:
        nb = rows // SUBLANES
        grid = (n_seq // nb, 1)
        row_map = lambda b, t: b
        scratch = []
    const2 = lambda b, t: (0, 0)
    in_specs = [pl.BlockSpec((rows, C), lambda b, t: (row_map(b, t), col_rx)),
                pl.BlockSpec((rows, C), lambda b, t: (row_map(b, t), col_rg)),
                pl.BlockSpec((nb, SUBLANES, C), lambda b, t: (b, 0, 0)),
                pl.BlockSpec((nb, 1, C), lambda b, t: (b, 0, 0)),
                pl.BlockSpec((CONV_W, C), const2),
                pl.BlockSpec((1, C), const2),
                pl.BlockSpec(wg.shape, lambda b, t: (0, 0, 0)),
                pl.BlockSpec((1, C), const2), pl.BlockSpec((1, C), const2), pl.BlockSpec((1, C), const2)]
    tokens = n_seq * seq_len
    out_shape = [jax.ShapeDtypeStruct((tokens, C), BF16),
                 jax.ShapeDtypeStruct((n_seq, SUBLANES, C), F32),
                 jax.ShapeDtypeStruct((n_seq, 1, C), F32)]
    out_specs = [pl.BlockSpec((rows, C), lambda b, t: (row_map(b, t), 0)),
                 pl.BlockSpec((nb, SUBLANES, C), lambda b, t: (b, 0, 0)),
                 pl.BlockSpec((nb, 1, C), lambda b, t: (b, 0, 0))]
    return pl.pallas_call(
        functools.partial(_rglru_kernel, carry=carry),
        grid=grid, in_specs=in_specs, out_specs=out_specs, out_shape=out_shape,
        scratch_shapes=scratch,
        compiler_params=_params("parallel", "arbitrary"),
        name="rglru",
    )(proj, proj, xprev, h0, cw, cb.reshape(1, C), wg, bx.reshape(1, C), ba.reshape(1, C),
      lam.reshape(1, C))


def _l2norm_heads(x, scale):
    dh = x.shape[1] // GDN_HEADS
    outs = []
    for h in range(GDN_HEADS):
        xh = x[:, h * dh:(h + 1) * dh]
        xh = xh * lax.rsqrt(jnp.sum(xh * xh, axis=-1, keepdims=True) + EPS)
        outs.append(xh * scale if scale != 1.0 else xh)
    return jnp.concatenate(outs, axis=1)


def _gdn_prep_kernel(gq_ref, gk_ref, gv_ref, xprev_ref, cw_ref, q_ref, k_ref, v_ref, xlast_ref,
                     *scratch, carry):
    R, C = gq_ref.shape
    G = R // SUBLANES
    dk = C // GDN_HEADS
    outs = []
    for s, src in enumerate((gq_ref, gk_ref, gv_ref)):
        x = src[...].reshape(G, SUBLANES, C)
        cols = slice(s * C, (s + 1) * C)
        if carry:
            xs_ref = scratch[0].at[s]
            first = pl.program_id(1) == 0

            @pl.when(first)
            def _():
                xs_ref[G] = xprev_ref[0, :, cols]

            xs_ref[0] = xs_ref[G]
            xs_ref[1:G + 1] = x
            xprev = xs_ref[0:G]
            xlast_ref[0, :, cols] = x[G - 1]
        else:
            xprev = xprev_ref[:, :, cols]
            xlast_ref[:, :, cols] = x
        outs.append(_silu(_causal_conv(x, xprev, cw_ref.at[:, cols])).reshape(R, C))
    q_ref[...] = _l2norm_heads(outs[0], dk ** -0.5)
    k_ref[...] = _l2norm_heads(outs[1], 1.0)
    v_ref[...] = outs[2]


def gdn_prep(proj, col_q, xprev, cw, *, n_seq, seq_len, rows):
    C = cw.shape[1] // 3
    carry = seq_len > SUBLANES
    if carry:
        nt = seq_len // rows
        grid = (n_seq, nt)
        row_map = lambda b, t: (b * nt + t)
        nb = 1
        scratch = [pltpu.VMEM((3, rows // SUBLANES + 1, SUBLANES, C), F32)]
    else:
        nb = rows // SUBLANES
        grid = (n_seq // nb, 1)
        row_map = lambda b, t: b
        scratch = []
    tokens = n_seq * seq_len
    in_specs = [pl.BlockSpec((rows, C), lambda b, t, c=c: (row_map(b, t), col_q + c)) for c in range(3)]
    in_specs += [pl.BlockSpec((nb, SUBLANES, 3 * C), lambda b, t: (b, 0, 0)),
                 pl.BlockSpec((CONV_W, 3 * C), lambda b, t: (0, 0))]
    out_shape = [jax.ShapeDtypeStruct((tokens, C), F32)] * 3 + [
        jax.ShapeDtypeStruct((n_seq, SUBLANES, 3 * C), F32)]
    out_specs = [pl.BlockSpec((rows, C), lambda b, t: (row_map(b, t), 0))] * 3 + [
        pl.BlockSpec((nb, SUBLANES, 3 * C), lambda b, t: (b, 0, 0))]
    return pl.pallas_call(
        functools.partial(_gdn_prep_kernel, carry=carry),
        grid=grid, in_specs=in_specs, out_specs=out_specs, out_shape=out_shape,
        scratch_shapes=scratch,
        compiler_params=_params("parallel", "arbitrary"),
        name="gdn_prep",
    )(proj, proj, proj, xprev, cw)


def _segment_cumsum(x, seg):
    pos = lax.broadcasted_iota(jnp.int32, x.shape, 0) & (seg - 1)
    s = 1
    while s < seg:
        x = x + jnp.where(pos >= s, pltpu.roll(x, s, 0), 0.0)
        s *= 2
    return x


def _segment_last(x, seg):
    n = x.shape[0]
    pos = lax.broadcasted_iota(jnp.int32, x.shape, 0) & (seg - 1)
    s = seg // 2
    while s >= 1:
        x = jnp.where((pos & (2 * s - 1)) < s, pltpu.roll(x, n - s, 0), x)
        s //= 2
    return x


class _TileMasks:
    def __init__(self, rows, seg):
        r = lax.broadcasted_iota(jnp.int32, (rows, rows), 0)
        c = lax.broadcasted_iota(jnp.int32, (rows, rows), 1)
        shift = seg.bit_length() - 1
        same = (r >> shift) == (c >> shift)
        self.incl = same & (r >= c)
        self.strict = same & (r > c)
        self.eye = r == c
        self.levels = []
        s = 1
        while s < seg:
            b = s.bit_length() - 1
            self.levels.append(((r >> (b + 1)) == (c >> (b + 1))) & (((r >> b) & 1) == 1) & (((c >> b) & 1) == 0))
            s *= 2


def _unit_lower_inverses(ms, masks):
    eye = jnp.where(masks.eye, 1.0, 0.0)
    xs = [eye - jnp.where(masks.levels[0], m, 0.0) for m in ms]
    for level in masks.levels[1:]:
        ts = [_dot(jnp.where(level, m, 0.0), x) for m, x in zip(ms, xs)]
        xs = [x - _dot(x, t) for x, t in zip(xs, ts)]
    return xs


def _gdn_tiles_local(chains, masks):
    r, d = chains[0][0].shape
    kbs, decays, kk_qks = [], [], []
    for q, k, v, beta, gc, g_last in chains:
        gc_row = jnp.sum(jnp.where(masks.eye, gc, 0.0), axis=0, keepdims=True)
        decays.append(jnp.where(masks.incl, jnp.exp(gc - gc_row), 0.0))
        kbs.append(k * beta)
        kk_qks.append(_dot_nt(jnp.concatenate([kbs[-1], q], axis=0), k))
    ms = [jnp.where(masks.strict, kq[:r] * dec, 0.0) for kq, dec in zip(kk_qks, decays)]
    tms = _unit_lower_inverses(ms, masks)
    out = []
    for (q, k, v, beta, gc, g_last), kb, dec, kq, tm in zip(chains, kbs, decays, kk_qks, tms):
        eg = jnp.exp(gc)
        vk = _dot(tm, jnp.concatenate([v * beta, kb * eg], axis=1))
        out.append(dict(value=vk[:, :d], kcd=vk[:, d:], attn=kq[r:] * dec, qg=q * eg,
                        kd=k * jnp.exp(g_last - gc), decay_last=jnp.exp(g_last)))
    return out


def _gdn_kernel(q_ref, k_ref, v_ref, z_ref, sm_ref, alog_ref, dtb_ref, ng_ref, s0_ref,
                o_ref, s_ref, *, tile, seg, n_tiles, carry):
    dk = q_ref.shape[1] // GDN_HEADS
    n_seg = tile // seg
    masks = _TileMasks(tile, seg)
    neg_a = -jnp.exp(alog_ref[...])
    dtb = dtb_ref[...]
    ng = ng_ref[...]
    if carry:
        @pl.when(pl.program_id(1) == 0)
        def _():
            s_ref[...] = s0_ref[...]

    chains = []
    for i in range(n_tiles):
        rows = slice(i * tile, (i + 1) * tile)
        sm = sm_ref[rows, :]
        beta_all = jax.nn.sigmoid(sm)
        gc_all = _segment_cumsum(neg_a * _softplus(sm + dtb), seg)
        gl_all = _segment_last(gc_all, seg)
        for h in range(GDN_HEADS):
            cols = slice(h * dk, (h + 1) * dk)
            lane = slice(GDN_HEADS + h, GDN_HEADS + h + 1)
            chains.append((q_ref[rows, cols], k_ref[rows, cols], v_ref[rows, cols],
                           beta_all[:, h:h + 1], gc_all[:, lane], gl_all[:, lane]))
    local = _gdn_tiles_local(chains, masks)

    heads = range(GDN_HEADS)
    for i in range(n_tiles):
        rows = slice(i * tile, (i + 1) * tile)
        loc = local[i * GDN_HEADS:(i + 1) * GDN_HEADS]
        v_new = [[] for _ in heads]
        o_state = [[] for _ in heads]
        for j in range(n_seg):
            sl = slice(j * seg, (j + 1) * seg)
            seq = 0 if carry else i * n_seg + j
            s_prev = [s_ref[seq, h] if carry else s0_ref[seq, h] for h in heads]
            rs = [_dot(jnp.concatenate([loc[h]["kcd"][sl], loc[h]["qg"][sl]], axis=0), s_prev[h])
                  for h in heads]
            vns = [loc[h]["value"][sl] - rs[h][:seg] for h in heads]
            for h in heads:
                v_new[h].append(vns[h])
                o_state[h].append(rs[h][seg:])
                s_ref[seq, h] = (s_prev[h] * loc[h]["decay_last"][j * seg:j * seg + 1, :]
                                 + _dot_tn(loc[h]["kd"][sl], vns[h]))
        for h in heads:
            cols = slice(h * dk, (h + 1) * dk)
            vn = jnp.concatenate(v_new[h], axis=0) if n_seg > 1 else v_new[h][0]
            os_ = jnp.concatenate(o_state[h], axis=0) if n_seg > 1 else o_state[h][0]
            o = os_ + _dot(loc[h]["attn"], vn)
            o_ref[rows, cols] = _rmsnorm(o, ng) * _silu(z_ref[rows, cols])


def gdn_core(q, k, v, proj, col_z, small, alog_row, dtb_row, ng, s0, *, n_seq, seq_len, tiles_per_step):
    tokens, C = q.shape
    dk = C // GDN_HEADS
    tile = GDN_CHUNK
    carry = seq_len >= GDN_CHUNK
    seg = GDN_CHUNK if carry else seq_len
    rows = tiles_per_step * tile
    if carry:
        nt = seq_len // rows
        seq_per_step = 1
        grid = (n_seq, nt)
    else:
        nt = 1
        seq_per_step = rows // seg
        grid = (n_seq // seq_per_step, 1)
    row_map = lambda b, t: (b * nt + t)
    tok_spec = pl.BlockSpec((rows, C), lambda b, t: (row_map(b, t), 0))
    const2 = lambda b, t: (0, 0)
    s_spec = pl.BlockSpec((seq_per_step, GDN_HEADS, dk, dk), lambda b, t: (b, 0, 0, 0))
    in_specs = [tok_spec, tok_spec, tok_spec,
                pl.BlockSpec((rows, C), lambda b, t: (row_map(b, t), col_z)),
                pl.BlockSpec((rows, LANES), lambda b, t: (row_map(b, t), 0)),
                pl.BlockSpec((1, LANES), const2), pl.BlockSpec((1, LANES), const2),
                pl.BlockSpec((1, dk), const2), s_spec]
    return pl.pallas_call(
        functools.partial(_gdn_kernel, tile=tile, seg=seg, n_tiles=tiles_per_step, carry=carry),
        grid=grid, in_specs=in_specs,
        out_specs=[tok_spec, s_spec],
        out_shape=[jax.ShapeDtypeStruct((tokens, C), F32), jax.ShapeDtypeStruct(s0.shape, F32)],
        compiler_params=_params("parallel", "arbitrary"),
        name="gdn_core",
    )(q, k, v, proj, small, alog_row, dtb_row, ng.reshape(1, dk), s0)


def _mem_attn_kernel(q_ref, k_ref, v_ref, o_ref, *, n_seq, tq):
    hd = q_ref.shape[1] // MEM_HEADS
    scale = hd ** -0.5
    rows = lambda s: slice(s * tq, (s + 1) * tq)
    cols = lambda h: slice(h * hd, (h + 1) * hd)
    if len(k_ref.shape) == 4:
        mem_rows = k_ref.shape[1] * MEM_HEADS
        r_head = lax.broadcasted_iota(jnp.int32, (MEM_HEADS * tq, mem_rows), 0) // tq
        c_head = lax.broadcasted_iota(jnp.int32, (MEM_HEADS * tq, mem_rows), 1) & (MEM_HEADS - 1)
        own = r_head == c_head
        scores = []
        for s in range(n_seq):
            q_all = jnp.concatenate([q_ref[rows(s), cols(h)] for h in range(MEM_HEADS)], axis=0)
            sc = _dot_nt(q_all, k_ref[s].reshape(mem_rows, hd)) * scale
            scores.append(jnp.where(own, sc, -jnp.inf))
        probs = []
        for sc in scores:
            e = jnp.exp(sc - jnp.max(sc, axis=-1, keepdims=True))
            probs.append(e / jnp.sum(e, axis=-1, keepdims=True))
        for s, p in enumerate(probs):
            o_all = _dot(p, v_ref[s].reshape(mem_rows, hd))
            for h in range(MEM_HEADS):
                o_ref[rows(s), cols(h)] = o_all[h * tq:(h + 1) * tq]
        return
    pairs = [(s, h) for s in range(n_seq) for h in range(MEM_HEADS)]
    mem = lambda ref, s, h: ref[s, :, cols(h)]
    scores = [_dot_nt(q_ref[rows(s), cols(h)], mem(k_ref, s, h)) * scale for s, h in pairs]
    probs = []
    for sc in scores:
        e = jnp.exp(sc - jnp.max(sc, axis=-1, keepdims=True))
        probs.append(e / jnp.sum(e, axis=-1, keepdims=True))
    for (s, h), p in zip(pairs, probs):
        o_ref[rows(s), cols(h)] = _dot(p, mem(v_ref, s, h))


def mem_attention(proj, col_q, mem_k, col_k, mem_v, col_v, *, width, n_seq, seq_len, tq, seq_per_step):
    tokens = n_seq * seq_len
    mem_len = mem_k.shape[1]
    C = width
    nt = seq_len // tq
    rows = seq_per_step * tq
    grid = (n_seq // seq_per_step, nt)
    row_map = lambda b, t: (b * nt + t)
    if mem_k.ndim == 4:
        blk = (seq_per_step,) + mem_k.shape[1:]
        k_spec = v_spec = pl.BlockSpec(blk, lambda b, t: (b, 0, 0, 0))
    else:
        k_spec = pl.BlockSpec((seq_per_step, mem_len, C), lambda b, t: (b, 0, col_k))
        v_spec = pl.BlockSpec((seq_per_step, mem_len, C), lambda b, t: (b, 0, col_v))
    return pl.pallas_call(
        functools.partial(_mem_attn_kernel, n_seq=seq_per_step, tq=tq),
        grid=grid,
        in_specs=[pl.BlockSpec((rows, C), lambda b, t: (row_map(b, t), col_q)), k_spec, v_spec],
        out_specs=pl.BlockSpec((rows, C), lambda b, t: (row_map(b, t), 0)),
        out_shape=jax.ShapeDtypeStruct((tokens, C), F32),
        compiler_params=_params("parallel", "arbitrary"),
        name="mem_attention",
    )(proj, mem_k, mem_v)


def _merge_kernel(yr_ref, yg_ref, ym_ref, g0_ref, g1_ref, g2_ref, w0_ref, w1_ref, w2_ref, o_ref):
    acc = jax.nn.sigmoid(g0_ref[...]) * _dot(yr_ref[...], w0_ref[...])
    acc = acc + jax.nn.sigmoid(g1_ref[...]) * _dot(yg_ref[...], w1_ref[...])
    acc = acc + jax.nn.sigmoid(g2_ref[...]) * _dot(ym_ref[...], w2_ref[...])
    o_ref[...] = acc.astype(BF16)


def merge_branches(y_rnn, y_gdn, y_mem, proj, col_gate, w_rnn_up, w_gdn_up, w_mem_up, *, tm, tn):
    M, C = y_rnn.shape
    N = w_rnn_up.shape[1]
    nj = N // tn
    y_spec = pl.BlockSpec((tm, C), lambda i, j: (i, 0))
    w_spec = pl.BlockSpec((C, tn), lambda i, j: (0, j))
    gate_specs = [pl.BlockSpec((tm, tn), lambda i, j, b=b: (i, col_gate + b * nj + j)) for b in range(N_BRANCH)]
    return pl.pallas_call(
        _merge_kernel,
        grid=(M // tm, nj),
        in_specs=[y_spec, y_spec, y_spec] + gate_specs + [w_spec, w_spec, w_spec],
        out_specs=pl.BlockSpec((tm, tn), lambda i, j: (i, j)),
        out_shape=jax.ShapeDtypeStruct((M, N), BF16),
        compiler_params=_params("parallel", "arbitrary"),
        name="merge_branches",
    )(y_rnn, y_gdn, y_mem, proj, proj, proj, w_rnn_up, w_gdn_up, w_mem_up)


def _matmul_residual_kernel(a_ref, w_ref, x_ref, o_ref):
    o_ref[...] = x_ref[...] + _dot(a_ref[...], w_ref[...])


def matmul_residual(a, w, x, *, tm, tn):
    M, K = a.shape
    N = w.shape[1]
    return pl.pallas_call(
        _matmul_residual_kernel,
        grid=(M // tm, N // tn),
        in_specs=[pl.BlockSpec((tm, K), lambda i, j: (i, 0)),
                  pl.BlockSpec((K, tn), lambda i, j: (0, j)),
                  pl.BlockSpec((tm, tn), lambda i, j: (i, j))],
        out_specs=pl.BlockSpec((tm, tn), lambda i, j: (i, j)),
        out_shape=jax.ShapeDtypeStruct((M, N), F32),
        compiler_params=_params("parallel", "arbitrary"),
        name="matmul_residual",
    )(a, w, x)


def _mlp_kernel(x_ref, g_ref, wu_ref, wd_ref, gf_ref, o_ref, xn_ref):
    j = pl.program_id(1)

    @pl.when(j == 0)
    def _():
        xn_ref[...] = _rmsnorm(x_ref[...], g_ref[...]).astype(BF16)
        o_ref[...] = jnp.zeros_like(o_ref)

    hid = _dot(xn_ref[...], wu_ref[...])
    act = jnp.square(jnp.maximum(hid, 0.0))
    o_ref[...] += _dot(act, wd_ref[...])

    @pl.when(j == pl.num_programs(1) - 1)
    def _():
        o_ref[...] = _rmsnorm(x_ref[...] + o_ref[...], gf_ref[...])


def mlp_final_norm(x, g, w_up, w_down, g_final, *, tm, tf):
    M, D = x.shape
    FF = w_up.shape[1]
    return pl.pallas_call(
        _mlp_kernel,
        grid=(M // tm, FF // tf),
        in_specs=[pl.BlockSpec((tm, D), lambda i, j: (i, 0), pipeline_mode=pl.Buffered(1)),
                  pl.BlockSpec((1, D), lambda i, j: (0, 0)),
                  pl.BlockSpec((D, tf), lambda i, j: (0, j)),
                  pl.BlockSpec((tf, D), lambda i, j: (j, 0)),
                  pl.BlockSpec((1, D), lambda i, j: (0, 0))],
        out_specs=pl.BlockSpec((tm, D), lambda i, j: (i, 0)),
        out_shape=jax.ShapeDtypeStruct((M, D), F32),
        scratch_shapes=[pltpu.VMEM((tm, D), BF16)],
        compiler_params=_params("parallel", "arbitrary"),
        name="mlp_final_norm",
    )(x, g.reshape(1, D), w_up, w_down, g_final.reshape(1, D))


def _w_in_prep_kernel(w_ref, wm_ref, ws_ref, *, half, d_model):
    n_gate = N_BRANCH * d_model
    o_small = 6 * half
    o_mq = o_small + 2 * GDN_HEADS
    o_mg = o_mq + half
    wm_ref[:, 0:n_gate] = w_ref[:, o_mg:o_mg + n_gate].astype(BF16)
    wm_ref[:, n_gate:n_gate + o_small] = w_ref[:, 0:o_small].astype(BF16)
    wm_ref[:, n_gate + o_small:n_gate + o_small + half] = w_ref[:, o_mq:o_mq + half].astype(BF16)
    lane = lax.broadcasted_iota(jnp.int32, ws_ref.shape, 1)
    ws_ref[...] = jnp.where(lane < 2 * GDN_HEADS, w_ref[:, o_small:o_small + LANES], 0.0).astype(BF16)


def w_in_prep(w_in, *, rows):
    d_model, n_in = w_in.shape
    half = d_model // 2
    n_main = n_in - 2 * GDN_HEADS
    return pl.pallas_call(
        functools.partial(_w_in_prep_kernel, half=half, d_model=d_model),
        grid=(d_model // rows,),
        in_specs=[pl.BlockSpec((rows, n_in), lambda i: (i, 0))],
        out_specs=[pl.BlockSpec((rows, n_main), lambda i: (i, 0)), pl.BlockSpec((rows, LANES), lambda i: (i, 0))],
        out_shape=[jax.ShapeDtypeStruct((d_model, n_main), BF16), jax.ShapeDtypeStruct((d_model, LANES), BF16)],
        compiler_params=_params("parallel"),
        name="w_in_prep",
    )(w_in)


def _prep_layer_weights(w_in, rnn_wx, rnn_wa, gdn_A_log, gdn_dt_bias, d_model):
    w_main, w_small = w_in_prep(w_in, rows=128)
    w_gate = jnp.concatenate([rnn_wx, rnn_wa], axis=-1).astype(BF16)
    lane_pad = (GDN_HEADS, LANES - 2 * GDN_HEADS)
    alog_row = jnp.pad(gdn_A_log, lane_pad).reshape(1, LANES)
    dtb_row = jnp.pad(gdn_dt_bias, lane_pad).reshape(1, LANES)
    return w_main, w_small, w_gate, alog_row, dtb_row


def _pad_conv_state(buf):
    return jnp.pad(buf, ((0, 0), (SUBLANES - (CONV_W - 1), 0), (0, 0)))


def _group_layer(x, mem_k, col_k, mem_v, col_v, rnn_buf, rnn_h0, gdn_buf, gdn_s0, lw, *, n_seq, seq_len,
                 tm, row_tile, attn_tq, attn_seqs, gdn_tiles, final_g):
    D = x.shape[1]
    half = D // 2
    n_gate_blk = N_BRANCH * D // half
    c_rx, c_rg, c_gq, c_gz, c_mq = (n_gate_blk + i for i in (0, 1, 2, 5, 6))
    proj, small = norm_matmul(x, lw["norm_mix_g"], lw["w_main"], lw["w_small"], tm=tm, tn=512)

    y_rnn, rnn_last, h_last = rglru_branch(
        proj, c_rx, c_rg, _pad_conv_state(rnn_buf), rnn_h0.reshape(n_seq, 1, half),
        lw["rnn_conv_w"], lw["rnn_conv_b"], lw["w_gate"], lw["rnn_bx"], lw["rnn_ba"], lw["rnn_L"],
        n_seq=n_seq, seq_len=seq_len, rows=row_tile)

    q, k, v, gdn_last = gdn_prep(proj, c_gq, _pad_conv_state(gdn_buf), lw["gdn_conv_w"],
                                 n_seq=n_seq, seq_len=seq_len, rows=row_tile)
    y_gdn, s_new = gdn_core(q, k, v, proj, c_gz, small, lw["alog_row"], lw["dtb_row"], lw["gdn_norm_g"],
                            gdn_s0, n_seq=n_seq, seq_len=seq_len, tiles_per_step=gdn_tiles)

    y_mem = mem_attention(proj, c_mq, mem_k, col_k, mem_v, col_v, width=half, n_seq=n_seq, seq_len=seq_len,
                          tq=attn_tq, seq_per_step=attn_seqs)

    merged = merge_branches(y_rnn, y_gdn, y_mem, proj, 0, lw["w_rnn_up"], lw["w_gdn_up"], lw["w_mem_up"],
                            tm=tm, tn=512)
    x1 = matmul_residual(merged, lw["w_out"], x, tm=tm, tn=512)
    x2 = mlp_final_norm(x1, lw["norm_mlp_g"], lw["w_mlp_up"], lw["w_mlp_down"], final_g, tm=tm, tf=512)
    states = (rnn_last[:, SUBLANES - (CONV_W - 1):], h_last.reshape(n_seq, half),
              gdn_last[:, SUBLANES - (CONV_W - 1):], s_new)
    return x2, states


def kernel(x_prompt, x_sample, mem_prompt, cache_mem_k, cache_mem_v, state_rnn_conv, state_rnn_h,
           state_gdn_conv, state_gdn_S, norm_mix_g, w_in, rnn_conv_w, rnn_conv_b, rnn_wx, rnn_bx, rnn_wa,
           rnn_ba, rnn_L, gdn_conv_w, gdn_A_log, gdn_dt_bias, gdn_norm_g, mem_norm_g, w_mem_kv, w_rnn_up,
           w_gdn_up, w_mem_up, w_out, norm_mlp_g, w_mlp_up, w_mlp_down, norm_final_g):
    depth = w_in.shape[0]
    assert depth == 1, "the final norm is fused into the last layer's MLP kernel; one layer supported"
    Bp, T, D = x_prompt.shape
    Bs, Ts, _ = x_sample.shape
    half = D // 2
    mem_len = mem_prompt.shape[1]
    assert Ts == SUBLANES and T % GDN_CHUNK == 0
    l = 0
    w_main, w_small, w_gate, alog_row, dtb_row = _prep_layer_weights(
        w_in[l], rnn_wx[l], rnn_wa[l], gdn_A_log[l], gdn_dt_bias[l], D)
    lw = dict(norm_mix_g=norm_mix_g[l], w_main=w_main, w_small=w_small, w_gate=w_gate,
              rnn_conv_w=rnn_conv_w[l], rnn_conv_b=rnn_conv_b[l], rnn_bx=rnn_bx[l], rnn_ba=rnn_ba[l],
              rnn_L=rnn_L[l], gdn_conv_w=gdn_conv_w[l], alog_row=alog_row, dtb_row=dtb_row,
              gdn_norm_g=gdn_norm_g[l], w_rnn_up=w_rnn_up[l], w_gdn_up=w_gdn_up[l],
              w_mem_up=w_mem_up[l], w_out=w_out[l], norm_mlp_g=norm_mlp_g[l],
              w_mlp_up=w_mlp_up[l], w_mlp_down=w_mlp_down[l])

    kv = norm_matmul(mem_prompt.reshape(Bp * mem_len, D), mem_norm_g[l], w_mem_kv[l],
                     tm=min(Bp * mem_len, 1024), tn=512)
    kv3 = kv.reshape(Bp, mem_len, 2 * half)
    zeros = lambda *s: jnp.zeros(s, F32)
    yp, (rb_p, rh_p, gb_p, gs_p) = _group_layer(
        x_prompt.reshape(Bp * T, D), kv3, 0, kv3, 1,
        zeros(Bp, CONV_W - 1, half), zeros(Bp, half), zeros(Bp, CONV_W - 1, 3 * half),
        zeros(Bp, GDN_HEADS, half // GDN_HEADS, half // GDN_HEADS), lw,
        n_seq=Bp, seq_len=T, tm=min(Bp * T, 1024), row_tile=256, attn_tq=512, attn_seqs=1, gdn_tiles=2,
        final_g=norm_final_g)
    mk_p = kv3[:, :, :half].reshape(1, Bp, mem_len, MEM_HEADS, half // MEM_HEADS)
    mv_p = kv3[:, :, half:].reshape(1, Bp, mem_len, MEM_HEADS, half // MEM_HEADS)

    ys, (rb_s, rh_s, gb_s, gs_s) = _group_layer(
        x_sample.reshape(Bs * Ts, D), cache_mem_k.reshape((depth * Bs,) + cache_mem_k.shape[2:]), 0,
        cache_mem_v.reshape((depth * Bs,) + cache_mem_v.shape[2:]), 0,
        state_rnn_conv[l], state_rnn_h[l], state_gdn_conv[l], state_gdn_S[l], lw,
        n_seq=Bs, seq_len=Ts, tm=min(Bs * Ts, 1024), row_tile=min(Bs, 16) * SUBLANES, attn_tq=Ts,
        attn_seqs=min(Bs, 4), gdn_tiles=1, final_g=norm_final_g)

    return (yp.reshape(Bp, T, D), ys.reshape(Bs, Ts, D), mk_p, mv_p, rb_p[None], rh_p[None], gb_p[None],
            gs_p[None], rb_s[None], rh_s[None], gb_s[None], gs_s[None])
```

```python
import functools

import jax
import jax.numpy as jnp
from jax import lax
from jax.experimental import pallas as pl
from jax.experimental.pallas import tpu as pltpu

F32 = jnp.float32
BF16 = jnp.bfloat16

EPS = 1e-6
RG_C = 8.0
CONV_W = 4
RNN_BLOCKS = 8
GDN_HEADS = 8
GDN_CHUNK = 64
MEM_HEADS = 4
N_BRANCH = 3

SUBLANES = 8
LANES = 128
VMEM_LIMIT_BYTES = 56 * 1024 * 1024


def _params(*sem):
    return pltpu.CompilerParams(dimension_semantics=sem, vmem_limit_bytes=VMEM_LIMIT_BYTES)


def _dot(a, b):
    return jnp.dot(a.astype(BF16), b.astype(BF16), preferred_element_type=F32)


def _dot_nt(a, b):
    return lax.dot_general(a.astype(BF16), b.astype(BF16), (((1,), (1,)), ((), ())),
                           preferred_element_type=F32)


def _dot_tn(a, b):
    return lax.dot_general(a.astype(BF16), b.astype(BF16), (((0,), (0,)), ((), ())),
                           preferred_element_type=F32)


def _rmsnorm(x, g):
    return (x * lax.rsqrt(jnp.mean(x * x, axis=-1, keepdims=True) + EPS)) * g


def _softplus(x):
    return jnp.maximum(x, 0.0) + jnp.log1p(jnp.exp(-jnp.abs(x)))


def _silu(x):
    return x * jax.nn.sigmoid(x)


def _norm_matmul_kernel(*refs, has_small):
    if has_small:
        x_ref, g_ref, w_ref, ws_ref, o_ref, os_ref, xn_ref = refs
    else:
        x_ref, g_ref, w_ref, o_ref, xn_ref = refs

    @pl.when(pl.program_id(1) == 0)
    def _():
        xn_ref[...] = _rmsnorm(x_ref[...], g_ref[...]).astype(BF16)
        if has_small:
            os_ref[...] = jnp.dot(xn_ref[...], ws_ref[...], preferred_element_type=F32)

    o_ref[...] = _dot(xn_ref[...], w_ref[...])


def norm_matmul(x, g, w, w_small=None, *, tm, tn):
    M, K = x.shape
    N = w.shape[1]
    has_small = w_small is not None
    in_specs = [pl.BlockSpec((tm, K), lambda i, j: (i, 0)),
                pl.BlockSpec((1, K), lambda i, j: (0, 0)),
                pl.BlockSpec((K, tn), lambda i, j: (0, j))]
    out_shape = [jax.ShapeDtypeStruct((M, N), F32)]
    out_specs = [pl.BlockSpec((tm, tn), lambda i, j: (i, j))]
    args = [x, g.reshape(1, K), w]
    if has_small:
        ns = w_small.shape[1]
        in_specs.append(pl.BlockSpec((K, ns), lambda i, j: (0, 0)))
        out_shape.append(jax.ShapeDtypeStruct((M, ns), F32))
        out_specs.append(pl.BlockSpec((tm, ns), lambda i, j: (i, 0)))
        args.append(w_small)
    outs = pl.pallas_call(
        functools.partial(_norm_matmul_kernel, has_small=has_small),
        grid=(M // tm, N // tn),
        in_specs=in_specs, out_specs=out_specs, out_shape=out_shape,
        scratch_shapes=[pltpu.VMEM((tm, K), BF16)],
        compiler_params=_params("parallel", "arbitrary"),
        name="norm_matmul",
    )(*args)
    return outs if has_small else outs[0]


def _in_proj_kernel(x_ref, g_ref, wt_ref, wst_ref, o_ref, os_ref, xn_ref):
    @pl.when(pl.program_id(1) == 0)
    def _():
        xn_ref[...] = _rmsnorm(x_ref[...], g_ref[...]).astype(BF16)
        os_ref[...] = _dot_nt(xn_ref[...], wst_ref[...])

    o_ref[...] = _dot_nt(xn_ref[...], wt_ref[...])


def in_projection(x, g, w_t, segments, small_row, *, tm, tn):
    M, K = x.shape
    n_tiles = [n // tn for _, n in segments]
    assert all(n % tn == 0 and r % SUBLANES == 0 for r, n in segments) and small_row % LANES == 0
    N = tn * sum(n_tiles)

    def w_row(j):
        row, first = None, 0
        for (r0, _), nt in zip(segments, n_tiles):
            cand = r0 // SUBLANES + (tn // SUBLANES) * (j - first)
            row = cand if row is None else jnp.where(j >= first, cand, row)
            first += nt
        return row * SUBLANES

    outs = pl.pallas_call(
        _in_proj_kernel,
        grid=(M // tm, N // tn),
        in_specs=[pl.BlockSpec((tm, K), lambda i, j: (i, 0)),
                  pl.BlockSpec((1, K), lambda i, j: (0, 0)),
                  pl.BlockSpec((pl.Element(tn), pl.Element(K)), lambda i, j: (w_row(j), 0)),
                  pl.BlockSpec((LANES, K), lambda i, j: (small_row // LANES, 0))],
        out_specs=[pl.BlockSpec((tm, tn), lambda i, j: (i, j)), pl.BlockSpec((tm, LANES), lambda i, j: (i, 0))],
        out_shape=[jax.ShapeDtypeStruct((M, N), F32), jax.ShapeDtypeStruct((M, LANES), F32)],
        scratch_shapes=[pltpu.VMEM((tm, K), BF16)],
        compiler_params=_params("parallel", "arbitrary"),
        name="in_projection",
    )(x, g.reshape(1, K), w_t, w_t)
    return outs


def _causal_conv(x, xprev, w_ref):
    t_idx = lax.broadcasted_iota(jnp.int32, x.shape, 1)
    y = x * w_ref[CONV_W - 1:CONV_W, :][None]
    for k in range(1, CONV_W):
        shifted = jnp.where(t_idx >= k, pltpu.roll(x, k, 1), pltpu.roll(xprev, k, 1))
        y = y + shifted * w_ref[CONV_W - 1 - k:CONV_W - k, :][None]
    return y


def _segment_scan(a, u):
    t_idx = lax.broadcasted_iota(jnp.int32, a.shape, 1)
    s = 1
    while s < SUBLANES:
        keep = t_idx >= s
        u = jnp.where(keep, a * pltpu.roll(u, s, 1) + u, u)
        a = jnp.where(keep, a * pltpu.roll(a, s, 1), a)
        s *= 2
    return a, u


def _stage_groups(x, xprev_ref, xs_ref, first):
    G = x.shape[0]

    @pl.when(first)
    def _():
        xs_ref[G] = xprev_ref[0]

    xs_ref[0] = xs_ref[G]
    xs_ref[1:G + 1] = x
    return xs_ref[0:G]


def _rglru_kernel(rx_ref, rg_ref, xprev_ref, h0_ref, cw_ref, cb_ref, wg_ref, bx_ref, ba_ref, l_ref,
                  y_ref, xlast_ref, hlast_ref, *scratch, carry):
    R, C = rx_ref.shape
    G = R // SUBLANES
    x = rx_ref[...].reshape(G, SUBLANES, C)
    if carry:
        xs_ref, a_ref, u_ref, h_ref, hc_ref = scratch
        first = pl.program_id(1) == 0
        xprev = _stage_groups(x, xprev_ref, xs_ref, first)

        @pl.when(first)
        def _():
            hc_ref[...] = h0_ref[0]
    else:
        xprev = xprev_ref[...]

    xc = (_causal_conv(x, xprev, cw_ref) + cb_ref[...][None]).reshape(R, C)
    xb = xc.astype(BF16)
    bs = C // RNN_BLOCKS
    zi, zr = [], []
    for n in range(RNN_BLOCKS):
        z = jnp.dot(xb[:, n * bs:(n + 1) * bs], wg_ref[n], preferred_element_type=F32)
        zi.append(z[:, :bs])
        zr.append(z[:, bs:])
    gi = jax.nn.sigmoid(jnp.concatenate(zi, axis=1) + bx_ref[...])
    gr = jax.nn.sigmoid(jnp.concatenate(zr, axis=1) + ba_ref[...])
    lv = l_ref[...]
    log_sig_l = -_softplus(-lv)
    log_a = RG_C * gr * log_sig_l
    a = jnp.exp(log_a)
    u = jnp.sqrt(-jnp.tanh(log_a) * (a * a + 1.0)) * (gi * xc)
    a_cum, h_loc = _segment_scan(a.reshape(G, SUBLANES, C), u.reshape(G, SUBLANES, C))

    if carry:
        a_ref[...] = a_cum
        u_ref[...] = h_loc

        def body(g, h_prev):
            hg = u_ref[g] + a_ref[g] * h_prev
            h_ref[g] = hg
            return hg[SUBLANES - 1:SUBLANES, :]

        h_last = lax.fori_loop(0, G, body, hc_ref[...])
        hc_ref[...] = h_last
        h = h_ref[...]
        xlast_ref[0] = x[G - 1]
        hlast_ref[0] = h_last
    else:
        h = h_loc + a_cum * h0_ref[...]
        xlast_ref[...] = x
        hlast_ref[...] = h[:, SUBLANES - 1:SUBLANES, :]

    y = h.reshape(R, C) * jax.nn.gelu(rg_ref[...])
    y_ref[...] = y.astype(BF16)


def rglru_branch(proj, col_rx, col_rg, xprev, h0, cw, cb, wg, bx, ba, lam, *, n_seq, seq_len, rows):
    C = cw.shape[1]
    carry = seq_len > SUBLANES
    if carry:
        nt = seq_len // rows
        grid = (n_seq, nt)
        row_map = lambda b, t: (b * nt + t)
        nb = 1
        G = rows // SUBLANES
        scratch = [pltpu.VMEM((G + 1, SUBLANES, C), F32), pltpu.VMEM((G, SUBLANES, C), F32),
                   pltpu.VMEM((G, SUBLANES, C), F32), pltpu.VMEM((G, SUBLANES, C), F32),
                   pltpu.VMEM((1, C), F32)]
    else:
        nb = rows // SUBLANES
        grid = (n_seq // nb, 1)
        row_map = lambda b, t: b
        scratch = []
    const2 = lambda b, t: (0, 0)
    in_specs = [pl.BlockSpec((rows, C), lambda b, t: (row_map(b, t), col_rx)),
                pl.BlockSpec((rows, C), lambda b, t: (row_map(b, t), col_rg)),
                pl.BlockSpec((nb, SUBLANES, C), lambda b, t: (b, 0, 0)),
                pl.BlockSpec((nb, 1, C), lambda b, t: (b, 0, 0)),
                pl.BlockSpec((CONV_W, C), const2),
                pl.BlockSpec((1, C), const2),
                pl.BlockSpec(wg.shape, lambda b, t: (0, 0, 0)),
                pl.BlockSpec((1, C), const2), pl.BlockSpec((1, C), const2), pl.BlockSpec((1, C), const2)]
    tokens = n_seq * seq_len
    out_shape = [jax.ShapeDtypeStruct((tokens, C), BF16),
                 jax.ShapeDtypeStruct((n_seq, SUBLANES, C), F32),
                 jax.ShapeDtypeStruct((n_seq, 1, C), F32)]
    out_specs = [pl.BlockSpec((rows, C), lambda b, t: (row_map(b, t), 0)),
                 pl.BlockSpec((nb, SUBLANES, C), lambda b, t: (b, 0, 0)),
                 pl.BlockSpec((nb, 1, C), lambda b, t: (b, 0, 0))]
    return pl.pallas_call(
        functools.partial(_rglru_kernel, carry=carry),
        grid=grid, in_specs=in_specs, out_specs=out_specs, out_shape=out_shape,
        scratch_shapes=scratch,
        compiler_params=_params("parallel", "arbitrary"),
        name="rglru",
    )(proj, proj, xprev, h0, cw, cb.reshape(1, C), wg, bx.reshape(1, C), ba.reshape(1, C),
      lam.reshape(1, C))


def _l2norm_heads(x, scale):
    dh = x.shape[1] // GDN_HEADS
    outs = []
    for h in range(GDN_HEADS):
        xh = x[:, h * dh:(h + 1) * dh]
        xh = xh * lax.rsqrt(jnp.sum(xh * xh, axis=-1, keepdims=True) + EPS)
        outs.append(xh * scale if scale != 1.0 else xh)
    return jnp.concatenate(outs, axis=1)


def _gdn_prep_kernel(gq_ref, gk_ref, gv_ref, xprev_ref, cw_ref, q_ref, k_ref, v_ref, xlast_ref,
                     *scratch, carry):
    R, C = gq_ref.shape
    G = R // SUBLANES
    dk = C // GDN_HEADS
    outs = []
    for s, src in enumerate((gq_ref, gk_ref, gv_ref)):
        x = src[...].reshape(G, SUBLANES, C)
        cols = slice(s * C, (s + 1) * C)
        if carry:
            xs_ref = scratch[0].at[s]
            first = pl.program_id(1) == 0

            @pl.when(first)
            def _():
                xs_ref[G] = xprev_ref[0, :, cols]

            xs_ref[0] = xs_ref[G]
            xs_ref[1:G + 1] = x
            xprev = xs_ref[0:G]
            xlast_ref[0, :, cols] = x[G - 1]
        else:
            xprev = xprev_ref[:, :, cols]
            xlast_ref[:, :, cols] = x
        outs.append(_silu(_causal_conv(x, xprev, cw_ref.at[:, cols])).reshape(R, C))
    q_ref[...] = _l2norm_heads(outs[0], dk ** -0.5)
    k_ref[...] = _l2norm_heads(outs[1], 1.0)
    v_ref[...] = outs[2]


def gdn_prep(proj, col_q, xprev, cw, *, n_seq, seq_len, rows):
    C = cw.shape[1] // 3
    carry = seq_len > SUBLANES
    if carry:
        nt = seq_len // rows
        grid = (n_seq, nt)
        row_map = lambda b, t: (b * nt + t)
        nb = 1
        scratch = [pltpu.VMEM((3, rows // SUBLANES + 1, SUBLANES, C), F32)]
    else:
        nb = rows // SUBLANES
        grid = (n_seq // nb, 1)
        row_map = lambda b, t: b
        scratch = []
    tokens = n_seq * seq_len
    in_specs = [pl.BlockSpec((rows, C), lambda b, t, c=c: (row_map(b, t), col_q + c)) for c in range(3)]
    in_specs += [pl.BlockSpec((nb, SUBLANES, 3 * C), lambda b, t: (b, 0, 0)),
                 pl.BlockSpec((CONV_W, 3 * C), lambda b, t: (0, 0))]
    out_shape = [jax.ShapeDtypeStruct((tokens, C), F32)] * 3 + [
        jax.ShapeDtypeStruct((n_seq, SUBLANES, 3 * C), F32)]
    out_specs = [pl.BlockSpec((rows, C), lambda b, t: (row_map(b, t), 0))] * 3 + [
        pl.BlockSpec((nb, SUBLANES, 3 * C), lambda b, t: (b, 0, 0))]
    return pl.pallas_call(
        functools.partial(_gdn_prep_kernel, carry=carry),
        grid=grid, in_specs=in_specs, out_specs=out_specs, out_shape=out_shape,
        scratch_shapes=scratch,
        compiler_params=_params("parallel", "arbitrary"),
        name="gdn_prep",
    )(proj, proj, proj, xprev, cw)


def _segment_cumsum(x, seg):
    pos = lax.broadcasted_iota(jnp.int32, x.shape, 0) & (seg - 1)
    s = 1
    while s < seg:
        x = x + jnp.where(pos >= s, pltpu.roll(x, s, 0), 0.0)
        s *= 2
    return x


def _segment_last(x, seg):
    n = x.shape[0]
    pos = lax.broadcasted_iota(jnp.int32, x.shape, 0) & (seg - 1)
    s = seg // 2
    while s >= 1:
        x = jnp.where((pos & (2 * s - 1)) < s, pltpu.roll(x, n - s, 0), x)
        s //= 2
    return x


class _TileMasks:
    def __init__(self, rows, seg):
        r = lax.broadcasted_iota(jnp.int32, (rows, rows), 0)
        c = lax.broadcasted_iota(jnp.int32, (rows, rows), 1)
        shift = seg.bit_length() - 1
        same = (r >> shift) == (c >> shift)
        self.incl = same & (r >= c)
        self.strict = same & (r > c)
        self.eye = r == c
        self.levels = []
        s = 1
        while s < seg:
            b = s.bit_length() - 1
            self.levels.append(((r >> (b + 1)) == (c >> (b + 1))) & (((r >> b) & 1) == 1) & (((c >> b) & 1) == 0))
            s *= 2


def _unit_lower_inverses(ms, masks):
    eye = jnp.where(masks.eye, 1.0, 0.0)
    xs = [eye - jnp.where(masks.levels[0], m, 0.0) for m in ms]
    for level in masks.levels[1:]:
        ts = [_dot(jnp.where(level, m, 0.0), x) for m, x in zip(ms, xs)]
        xs = [x - _dot(x, t) for x, t in zip(xs, ts)]
    return xs


def _gdn_tiles_local(chains, masks):
    r, d = chains[0][0].shape
    kbs, decays, kk_qks = [], [], []
    for q, k, v, beta, gc, g_last in chains:
        gc_row = jnp.sum(jnp.where(masks.eye, gc, 0.0), axis=0, keepdims=True)
        decays.append(jnp.where(masks.incl, jnp.exp(gc - gc_row), 0.0))
        kbs.append(k * beta)
        kk_qks.append(_dot_nt(jnp.concatenate([kbs[-1], q], axis=0), k))
    ms = [jnp.where(masks.strict, kq[:r] * dec, 0.0) for kq, dec in zip(kk_qks, decays)]
    tms = _unit_lower_inverses(ms, masks)
    out = []
    for (q, k, v, beta, gc, g_last), kb, dec, kq, tm in zip(chains, kbs, decays, kk_qks, tms):
        eg = jnp.exp(gc)
        vk = _dot(tm, jnp.concatenate([v * beta, kb * eg], axis=1))
        out.append(dict(value=vk[:, :d], kcd=vk[:, d:], attn=kq[r:] * dec, qg=q * eg,
                        kd=k * jnp.exp(g_last - gc), decay_last=jnp.exp(g_last)))
    return out


def _gdn_kernel(q_ref, k_ref, v_ref, z_ref, sm_ref, alog_ref, dtb_ref, ng_ref, s0_ref,
                o_ref, s_ref, *, tile, seg, n_tiles, carry):
    dk = q_ref.shape[1] // GDN_HEADS
    n_seg = tile // seg
    masks = _TileMasks(tile, seg)
    neg_a = -jnp.exp(alog_ref[...])
    dtb = dtb_ref[...]
    ng = ng_ref[...]
    if carry:
        @pl.when(pl.program_id(1) == 0)
        def _():
            s_ref[...] = s0_ref[...]

    chains = []
    for i in range(n_tiles):
        rows = slice(i * tile, (i + 1) * tile)
        sm = sm_ref[rows, :]
        beta_all = jax.nn.sigmoid(sm)
        gc_all = _segment_cumsum(neg_a * _softplus(sm + dtb), seg)
        gl_all = _segment_last(gc_all, seg)
        for h in range(GDN_HEADS):
            cols = slice(h * dk, (h + 1) * dk)
            lane = slice(GDN_HEADS + h, GDN_HEADS + h + 1)
            chains.append((q_ref[rows, cols], k_ref[rows, cols], v_ref[rows, cols],
                           beta_all[:, h:h + 1], gc_all[:, lane], gl_all[:, lane]))
    local = _gdn_tiles_local(chains, masks)

    heads = range(GDN_HEADS)
    for i in range(n_tiles):
        rows = slice(i * tile, (i + 1) * tile)
        loc = local[i * GDN_HEADS:(i + 1) * GDN_HEADS]
        v_new = [[] for _ in heads]
        o_state = [[] for _ in heads]
        for j in range(n_seg):
            sl = slice(j * seg, (j + 1) * seg)
            seq = 0 if carry else i * n_seg + j
            s_prev = [s_ref[seq, h] if carry else s0_ref[seq, h] for h in heads]
            rs = [_dot(jnp.concatenate([loc[h]["kcd"][sl], loc[h]["qg"][sl]], axis=0), s_prev[h])
                  for h in heads]
            vns = [loc[h]["value"][sl] - rs[h][:seg] for h in heads]
            for h in heads:
                v_new[h].append(vns[h])
                o_state[h].append(rs[h][seg:])
                s_ref[seq, h] = (s_prev[h] * loc[h]["decay_last"][j * seg:j * seg + 1, :]
                                 + _dot_tn(loc[h]["kd"][sl], vns[h]))
        for h in heads:
            cols = slice(h * dk, (h + 1) * dk)
            vn = jnp.concatenate(v_new[h], axis=0) if n_seg > 1 else v_new[h][0]
            os_ = jnp.concatenate(o_state[h], axis=0) if n_seg > 1 else o_state[h][0]
            o = os_ + _dot(loc[h]["attn"], vn)
            o_ref[rows, cols] = _rmsnorm(o, ng) * _silu(z_ref[rows, cols])


def gdn_core(q, k, v, proj, col_z, small, alog_row, dtb_row, ng, s0, *, n_seq, seq_len, tiles_per_step):
    tokens, C = q.shape
    dk = C // GDN_HEADS
    tile = GDN_CHUNK
    carry = seq_len >= GDN_CHUNK
    seg = GDN_CHUNK if carry else seq_len
    rows = tiles_per_step * tile
    if carry:
        nt = seq_len // rows
        seq_per_step = 1
        grid = (n_seq, nt)
    else:
        nt = 1
        seq_per_step = rows // seg
        grid = (n_seq // seq_per_step, 1)
    row_map = lambda b, t: (b * nt + t)
    tok_spec = pl.BlockSpec((rows, C), lambda b, t: (row_map(b, t), 0))
    const2 = lambda b, t: (0, 0)
    s_spec = pl.BlockSpec((seq_per_step, GDN_HEADS, dk, dk), lambda b, t: (b, 0, 0, 0))
    in_specs = [tok_spec, tok_spec, tok_spec,
                pl.BlockSpec((rows, C), lambda b, t: (row_map(b, t), col_z)),
                pl.BlockSpec((rows, LANES), lambda b, t: (row_map(b, t), 0)),
                pl.BlockSpec((1, LANES), const2), pl.BlockSpec((1, LANES), const2),
                pl.BlockSpec((1, dk), const2), s_spec]
    return pl.pallas_call(
        functools.partial(_gdn_kernel, tile=tile, seg=seg, n_tiles=tiles_per_step, carry=carry),
        grid=grid, in_specs=in_specs,
        out_specs=[tok_spec, s_spec],
        out_shape=[jax.ShapeDtypeStruct((tokens, C), F32), jax.ShapeDtypeStruct(s0.shape, F32)],
        compiler_params=_params("parallel", "arbitrary"),
        name="gdn_core",
    )(q, k, v, proj, small, alog_row, dtb_row, ng.reshape(1, dk), s0)


def _mem_attn_kernel(q_ref, k_ref, v_ref, o_ref, *, n_seq, tq):
    hd = q_ref.shape[1] // MEM_HEADS
    scale = hd ** -0.5
    rows = lambda s: slice(s * tq, (s + 1) * tq)
    cols = lambda h: slice(h * hd, (h + 1) * hd)
    if len(k_ref.shape) == 4:
        mem_rows = k_ref.shape[1] * MEM_HEADS
        r_head = lax.broadcasted_iota(jnp.int32, (MEM_HEADS * tq, mem_rows), 0) // tq
        c_head = lax.broadcasted_iota(jnp.int32, (MEM_HEADS * tq, mem_rows), 1) & (MEM_HEADS - 1)
        own = r_head == c_head
        scores = []
        for s in range(n_seq):
            q_all = jnp.concatenate([q_ref[rows(s), cols(h)] for h in range(MEM_HEADS)], axis=0)
            sc = _dot_nt(q_all, k_ref[s].reshape(mem_rows, hd)) * scale
            scores.append(jnp.where(own, sc, -jnp.inf))
        probs = []
        for sc in scores:
            e = jnp.exp(sc - jnp.max(sc, axis=-1, keepdims=True))
            probs.append(e / jnp.sum(e, axis=-1, keepdims=True))
        for s, p in enumerate(probs):
            o_all = _dot(p, v_ref[s].reshape(mem_rows, hd))
            for h in range(MEM_HEADS):
                o_ref[rows(s), cols(h)] = o_all[h * tq:(h + 1) * tq]
        return
    pairs = [(s, h) for s in range(n_seq) for h in range(MEM_HEADS)]
    mem = lambda ref, s, h: ref[s, :, cols(h)]
    scores = [_dot_nt(q_ref[rows(s), cols(h)], mem(k_ref, s, h)) * scale for s, h in pairs]
    probs = []
    for sc in scores:
        e = jnp.exp(sc - jnp.max(sc, axis=-1, keepdims=True))
        probs.append(e / jnp.sum(e, axis=-1, keepdims=True))
    for (s, h), p in zip(pairs, probs):
        o_ref[rows(s), cols(h)] = _dot(p, mem(v_ref, s, h))


def mem_attention(proj, col_q, mem_k, col_k, mem_v, col_v, *, width, n_seq, seq_len, tq, seq_per_step):
    tokens = n_seq * seq_len
    mem_len = mem_k.shape[1]
    C = width
    nt = seq_len // tq
    rows = seq_per_step * tq
    grid = (n_seq // seq_per_step, nt)
    row_map = lambda b, t: (b * nt + t)
    if mem_k.ndim == 4:
        blk = (seq_per_step,) + mem_k.shape[1:]
        k_spec = v_spec = pl.BlockSpec(blk, lambda b, t: (b, 0, 0, 0))
    else:
        k_spec = pl.BlockSpec((seq_per_step, mem_len, C), lambda b, t: (b, 0, col_k))
        v_spec = pl.BlockSpec((seq_per_step, mem_len, C), lambda b, t: (b, 0, col_v))
    return pl.pallas_call(
        functools.partial(_mem_attn_kernel, n_seq=seq_per_step, tq=tq),
        grid=grid,
        in_specs=[pl.BlockSpec((rows, C), lambda b, t: (row_map(b, t), col_q)), k_spec, v_spec],
        out_specs=pl.BlockSpec((rows, C), lambda b, t: (row_map(b, t), 0)),
        out_shape=jax.ShapeDtypeStruct((tokens, C), F32),
        compiler_params=_params("parallel", "arbitrary"),
        name="mem_attention",
    )(proj, mem_k, mem_v)


def _merge_kernel(yr_ref, yg_ref, ym_ref, g0_ref, g1_ref, g2_ref, w0_ref, w1_ref, w2_ref, o_ref):
    acc = jax.nn.sigmoid(g0_ref[...]) * _dot(yr_ref[...], w0_ref[...])
    acc = acc + jax.nn.sigmoid(g1_ref[...]) * _dot(yg_ref[...], w1_ref[...])
    acc = acc + jax.nn.sigmoid(g2_ref[...]) * _dot(ym_ref[...], w2_ref[...])
    o_ref[...] = acc.astype(BF16)


def merge_branches(y_rnn, y_gdn, y_mem, proj, col_gate, w_rnn_up, w_gdn_up, w_mem_up, *, tm, tn):
    M, C = y_rnn.shape
    N = w_rnn_up.shape[1]
    nj = N // tn
    y_spec = pl.BlockSpec((tm, C), lambda i, j: (i, 0))
    w_spec = pl.BlockSpec((C, tn), lambda i, j: (0, j))
    gate_specs = [pl.BlockSpec((tm, tn), lambda i, j, b=b: (i, col_gate + b * nj + j)) for b in range(N_BRANCH)]
    return pl.pallas_call(
        _merge_kernel,
        grid=(M // tm, nj),
        in_specs=[y_spec, y_spec, y_spec] + gate_specs + [w_spec, w_spec, w_spec],
        out_specs=pl.BlockSpec((tm, tn), lambda i, j: (i, j)),
        out_shape=jax.ShapeDtypeStruct((M, N), BF16),
        compiler_params=_params("parallel", "arbitrary"),
        name="merge_branches",
    )(y_rnn, y_gdn, y_mem, proj, proj, proj, w_rnn_up, w_gdn_up, w_mem_up)


def _matmul_residual_kernel(a_ref, w_ref, x_ref, o_ref):
    o_ref[...] = x_ref[...] + _dot(a_ref[...], w_ref[...])


def matmul_residual(a, w, x, *, tm, tn):
    M, K = a.shape
    N = w.shape[1]
    return pl.pallas_call(
        _matmul_residual_kernel,
        grid=(M // tm, N // tn),
        in_specs=[pl.BlockSpec((tm, K), lambda i, j: (i, 0)),
                  pl.BlockSpec((K, tn), lambda i, j: (0, j)),
                  pl.BlockSpec((tm, tn), lambda i, j: (i, j))],
        out_specs=pl.BlockSpec((tm, tn), lambda i, j: (i, j)),
        out_shape=jax.ShapeDtypeStruct((M, N), F32),
        compiler_params=_params("parallel", "arbitrary"),
        name="matmul_residual",
    )(a, w, x)


def _mlp_kernel(x_ref, g_ref, wu_ref, wd_ref, gf_ref, o_ref, xn_ref):
    j = pl.program_id(1)

    @pl.when(j == 0)
    def _():
        xn_ref[...] = _rmsnorm(x_ref[...], g_ref[...]).astype(BF16)
        o_ref[...] = jnp.zeros_like(o_ref)

    hid = _dot(xn_ref[...], wu_ref[...])
    act = jnp.square(jnp.maximum(hid, 0.0))
    o_ref[...] += _dot(act, wd_ref[...])

    @pl.when(j == pl.num_programs(1) - 1)
    def _():
        o_ref[...] = _rmsnorm(x_ref[...] + o_ref[...], gf_ref[...])


def mlp_final_norm(x, g, w_up, w_down, g_final, *, tm, tf):
    M, D = x.shape
    FF = w_up.shape[1]
    return pl.pallas_call(
        _mlp_kernel,
        grid=(M // tm, FF // tf),
        in_specs=[pl.BlockSpec((tm, D), lambda i, j: (i, 0), pipeline_mode=pl.Buffered(1)),
                  pl.BlockSpec((1, D), lambda i, j: (0, 0)),
                  pl.BlockSpec((D, tf), lambda i, j: (0, j)),
                  pl.BlockSpec((tf, D), lambda i, j: (j, 0)),
                  pl.BlockSpec((1, D), lambda i, j: (0, 0))],
        out_specs=pl.BlockSpec((tm, D), lambda i, j: (i, 0)),
        out_shape=jax.ShapeDtypeStruct((M, D), F32),
        scratch_shapes=[pltpu.VMEM((tm, D), BF16)],
        compiler_params=_params("parallel", "arbitrary"),
        name="mlp_final_norm",
    )(x, g.reshape(1, D), w_up, w_down, g_final.reshape(1, D))


def _in_proj_segments(d_model):
    half = d_model // 2
    small_row = 6 * half
    mq_row = small_row + 2 * GDN_HEADS
    mg_row = mq_row + half
    return ((mg_row, N_BRANCH * d_model), (0, 6 * half), (mq_row, half)), small_row


def _prep_layer_weights(rnn_wx, rnn_wa, gdn_A_log, gdn_dt_bias):
    w_gate = jnp.concatenate([rnn_wx, rnn_wa], axis=-1).astype(BF16)
    lane_pad = (GDN_HEADS, LANES - 2 * GDN_HEADS)
    alog_row = jnp.pad(gdn_A_log, lane_pad).reshape(1, LANES)
    dtb_row = jnp.pad(gdn_dt_bias, lane_pad).reshape(1, LANES)
    return w_gate, alog_row, dtb_row


def _pad_conv_state(buf):
    return jnp.pad(buf, ((0, 0), (SUBLANES - (CONV_W - 1), 0), (0, 0)))


def _group_layer(x, mem_k, col_k, mem_v, col_v, rnn_buf, rnn_h0, gdn_buf, gdn_s0, lw, *, n_seq, seq_len,
                 tm, row_tile, attn_tq, attn_seqs, gdn_tiles, final_g):
    D = x.shape[1]
    half = D // 2
    n_gate_blk = N_BRANCH * D // half
    c_rx, c_rg, c_gq, c_gz, c_mq = (n_gate_blk + i for i in (0, 1, 2, 5, 6))
    segments, small_row = _in_proj_segments(D)
    proj, small = in_projection(x, lw["norm_mix_g"], lw["w_in_t"], segments, small_row, tm=tm, tn=512)

    y_rnn, rnn_last, h_last = rglru_branch(
        proj, c_rx, c_rg, _pad_conv_state(rnn_buf), rnn_h0.reshape(n_seq, 1, half),
        lw["rnn_conv_w"], lw["rnn_conv_b"], lw["w_gate"], lw["rnn_bx"], lw["rnn_ba"], lw["rnn_L"],
        n_seq=n_seq, seq_len=seq_len, rows=row_tile)

    q, k, v, gdn_last = gdn_prep(proj, c_gq, _pad_conv_state(gdn_buf), lw["gdn_conv_w"],
                                 n_seq=n_seq, seq_len=seq_len, rows=row_tile)
    y_gdn, s_new = gdn_core(q, k, v, proj, c_gz, small, lw["alog_row"], lw["dtb_row"], lw["gdn_norm_g"],
                            gdn_s0, n_seq=n_seq, seq_len=seq_len, tiles_per_step=gdn_tiles)

    y_mem = mem_attention(proj, c_mq, mem_k, col_k, mem_v, col_v, width=half, n_seq=n_seq, seq_len=seq_len,
                          tq=attn_tq, seq_per_step=attn_seqs)

    merged = merge_branches(y_rnn, y_gdn, y_mem, proj, 0, lw["w_rnn_up"], lw["w_gdn_up"], lw["w_mem_up"],
                            tm=tm, tn=512)
    x1 = matmul_residual(merged, lw["w_out"], x, tm=tm, tn=512)
    x2 = mlp_final_norm(x1, lw["norm_mlp_g"], lw["w_mlp_up"], lw["w_mlp_down"], final_g, tm=tm, tf=512)
    states = (rnn_last[:, SUBLANES - (CONV_W - 1):], h_last.reshape(n_seq, half),
              gdn_last[:, SUBLANES - (CONV_W - 1):], s_new)
    return x2, states


def kernel(x_prompt, x_sample, mem_prompt, cache_mem_k, cache_mem_v, state_rnn_conv, state_rnn_h,
           state_gdn_conv, state_gdn_S, norm_mix_g, w_in, rnn_conv_w, rnn_conv_b, rnn_wx, rnn_bx, rnn_wa,
           rnn_ba, rnn_L, gdn_conv_w, gdn_A_log, gdn_dt_bias, gdn_norm_g, mem_norm_g, w_mem_kv, w_rnn_up,
           w_gdn_up, w_mem_up, w_out, norm_mlp_g, w_mlp_up, w_mlp_down, norm_final_g):
    depth = w_in.shape[0]
    assert depth == 1, "the final norm is fused into the last layer's MLP kernel; one layer supported"
    Bp, T, D = x_prompt.shape
    Bs, Ts, _ = x_sample.shape
    half = D // 2
    mem_len = mem_prompt.shape[1]
    assert Ts == SUBLANES and T % GDN_CHUNK == 0
    l = 0
    w_gate, alog_row, dtb_row = _prep_layer_weights(rnn_wx[l], rnn_wa[l], gdn_A_log[l], gdn_dt_bias[l])
    lw = dict(norm_mix_g=norm_mix_g[l], w_in_t=jnp.swapaxes(w_in[l], 0, 1), w_gate=w_gate,
              rnn_conv_w=rnn_conv_w[l], rnn_conv_b=rnn_conv_b[l], rnn_bx=rnn_bx[l], rnn_ba=rnn_ba[l],
              rnn_L=rnn_L[l], gdn_conv_w=gdn_conv_w[l], alog_row=alog_row, dtb_row=dtb_row,
              gdn_norm_g=gdn_norm_g[l], w_rnn_up=w_rnn_up[l], w_gdn_up=w_gdn_up[l],
              w_mem_up=w_mem_up[l], w_out=w_out[l], norm_mlp_g=norm_mlp_g[l],
              w_mlp_up=w_mlp_up[l], w_mlp_down=w_mlp_down[l])

    kv = norm_matmul(mem_prompt.reshape(Bp * mem_len, D), mem_norm_g[l], w_mem_kv[l],
                     tm=min(Bp * mem_len, 1024), tn=512)
    kv3 = kv.reshape(Bp, mem_len, 2 * half)
    zeros = lambda *s: jnp.zeros(s, F32)
    yp, (rb_p, rh_p, gb_p, gs_p) = _group_layer(
        x_prompt.reshape(Bp * T, D), kv3, 0, kv3, 1,
        zeros(Bp, CONV_W - 1, half), zeros(Bp, half), zeros(Bp, CONV_W - 1, 3 * half),
        zeros(Bp, GDN_HEADS, half // GDN_HEADS, half // GDN_HEADS), lw,
        n_seq=Bp, seq_len=T, tm=min(Bp * T, 1024), row_tile=256, attn_tq=512, attn_seqs=1, gdn_tiles=2,
        final_g=norm_final_g)
    mk_p = kv3[:, :, :half].reshape(1, Bp, mem_len, MEM_HEADS, half // MEM_HEADS)
    mv_p = kv3[:, :, half:].reshape(1, Bp, mem_len, MEM_HEADS, half // MEM_HEADS)

    ys, (rb_s, rh_s, gb_s, gs_s) = _group_layer(
        x_sample.reshape(Bs * Ts, D), cache_mem_k.reshape((depth * Bs,) + cache_mem_k.shape[2:]), 0,
        cache_mem_v.reshape((depth * Bs,) + cache_mem_v.shape[2:]), 0,
        state_rnn_conv[l], state_rnn_h[l], state_gdn_conv[l], state_gdn_S[l], lw,
        n_seq=Bs, seq_len=Ts, tm=min(Bs * Ts, 1024), row_tile=min(Bs, 16) * SUBLANES, attn_tq=Ts,
        attn_seqs=min(Bs, 4), gdn_tiles=1, final_g=norm_final_g)

    return (yp.reshape(Bp, T, D), ys.reshape(Bs, Ts, D), mk_p, mv_p, rb_p[None], rh_p[None], gb_p[None],
            gs_p[None], rb_s[None], rh_s[None], gb_s[None], gs_s[None])
```

```python
import functools

import jax
import jax.numpy as jnp
from jax import lax
from jax.experimental import pallas as pl
from jax.experimental.pallas import tpu as pltpu

F32 = jnp.float32
BF16 = jnp.bfloat16

EPS = 1e-6
RG_C = 8.0
CONV_W = 4
RNN_BLOCKS = 8
GDN_HEADS = 8
GDN_CHUNK = 64
MEM_HEADS = 4
N_BRANCH = 3

SUBLANES = 8
LANES = 128
VMEM_LIMIT_BYTES = 56 * 1024 * 1024


def _params(*sem):
    return pltpu.CompilerParams(dimension_semantics=sem, vmem_limit_bytes=VMEM_LIMIT_BYTES)


def _dot(a, b):
    return jnp.dot(a.astype(BF16), b.astype(BF16), preferred_element_type=F32)


def _dot_nt(a, b):
    return lax.dot_general(a.astype(BF16), b.astype(BF16), (((1,), (1,)), ((), ())),
                           preferred_element_type=F32)


def _dot_tn(a, b):
    return lax.dot_general(a.astype(BF16), b.astype(BF16), (((0,), (0,)), ((), ())),
                           preferred_element_type=F32)


def _rmsnorm(x, g):
    return (x * lax.rsqrt(jnp.mean(x * x, axis=-1, keepdims=True) + EPS)) * g


def _softplus(x):
    return jnp.maximum(x, 0.0) + jnp.log1p(jnp.exp(-jnp.abs(x)))


_sigmoid = jax.nn.sigmoid


def _silu(x):
    return x * _sigmoid(x)


def _norm_matmul_kernel(*refs, has_small):
    if has_small:
        x_ref, g_ref, w_ref, ws_ref, o_ref, os_ref, xn_ref = refs
    else:
        x_ref, g_ref, w_ref, o_ref, xn_ref = refs

    @pl.when(pl.program_id(1) == 0)
    def _():
        xn_ref[...] = _rmsnorm(x_ref[...], g_ref[...]).astype(BF16)
        if has_small:
            os_ref[...] = jnp.dot(xn_ref[...], ws_ref[...], preferred_element_type=F32)

    o_ref[...] = _dot(xn_ref[...], w_ref[...])


def norm_matmul(x, g, w, w_small=None, *, tm, tn):
    M, K = x.shape
    N = w.shape[1]
    has_small = w_small is not None
    in_specs = [pl.BlockSpec((tm, K), lambda i, j: (i, 0)),
                pl.BlockSpec((1, K), lambda i, j: (0, 0)),
                pl.BlockSpec((K, tn), lambda i, j: (0, j))]
    out_shape = [jax.ShapeDtypeStruct((M, N), F32)]
    out_specs = [pl.BlockSpec((tm, tn), lambda i, j: (i, j))]
    args = [x, g.reshape(1, K), w]
    if has_small:
        ns = w_small.shape[1]
        in_specs.append(pl.BlockSpec((K, ns), lambda i, j: (0, 0)))
        out_shape.append(jax.ShapeDtypeStruct((M, ns), F32))
        out_specs.append(pl.BlockSpec((tm, ns), lambda i, j: (i, 0)))
        args.append(w_small)
    outs = pl.pallas_call(
        functools.partial(_norm_matmul_kernel, has_small=has_small),
        grid=(M // tm, N // tn),
        in_specs=in_specs, out_specs=out_specs, out_shape=out_shape,
        scratch_shapes=[pltpu.VMEM((tm, K), BF16)],
        compiler_params=_params("parallel", "arbitrary"),
        name="norm_matmul",
    )(*args)
    return outs if has_small else outs[0]


def _in_proj_kernel(x_ref, g_ref, wt_ref, wst_ref, o_ref, os_ref, xn_ref):
    @pl.when(pl.program_id(1) == 0)
    def _():
        xn_ref[...] = _rmsnorm(x_ref[...], g_ref[...]).astype(BF16)
        os_ref[...] = _dot_nt(xn_ref[...], wst_ref[...])

    o_ref[...] = _dot_nt(xn_ref[...], wt_ref[...])


def in_projection(x, g, w_t, segments, small_row, *, tm, tn):
    M, K = x.shape
    n_tiles = [n // tn for _, n in segments]
    assert all(n % tn == 0 and r % SUBLANES == 0 for r, n in segments) and small_row % LANES == 0
    N = tn * sum(n_tiles)

    def w_row(j):
        row, first = None, 0
        for (r0, _), nt in zip(segments, n_tiles):
            cand = r0 // SUBLANES + (tn // SUBLANES) * (j - first)
            row = cand if row is None else jnp.where(j >= first, cand, row)
            first += nt
        return row * SUBLANES

    outs = pl.pallas_call(
        _in_proj_kernel,
        grid=(M // tm, N // tn),
        in_specs=[pl.BlockSpec((tm, K), lambda i, j: (i, 0), pipeline_mode=pl.Buffered(1)),
                  pl.BlockSpec((1, K), lambda i, j: (0, 0)),
                  pl.BlockSpec((pl.Element(tn), pl.Element(K)), lambda i, j: (w_row(j), 0)),
                  pl.BlockSpec((LANES, K), lambda i, j: (small_row // LANES, 0))],
        out_specs=[pl.BlockSpec((tm, tn), lambda i, j: (i, j)), pl.BlockSpec((tm, LANES), lambda i, j: (i, 0))],
        out_shape=[jax.ShapeDtypeStruct((M, N), F32), jax.ShapeDtypeStruct((M, LANES), F32)],
        scratch_shapes=[pltpu.VMEM((tm, K), BF16)],
        compiler_params=_params("parallel", "arbitrary"),
        name="in_projection",
    )(x, g.reshape(1, K), w_t, w_t)
    return outs


def _causal_conv(x, xprev, w_ref):
    t_idx = lax.broadcasted_iota(jnp.int32, x.shape, 1)
    y = x * w_ref[CONV_W - 1:CONV_W, :][None]
    for k in range(1, CONV_W):
        shifted = jnp.where(t_idx >= k, pltpu.roll(x, k, 1), pltpu.roll(xprev, k, 1))
        y = y + shifted * w_ref[CONV_W - 1 - k:CONV_W - k, :][None]
    return y


def _segment_scan(a, u):
    t_idx = lax.broadcasted_iota(jnp.int32, a.shape, 1)
    s = 1
    while s < SUBLANES:
        keep = t_idx >= s
        u = jnp.where(keep, a * pltpu.roll(u, s, 1) + u, u)
        a = jnp.where(keep, a * pltpu.roll(a, s, 1), a)
        s *= 2
    return a, u


def _stage_groups(x, xprev_ref, xs_ref, first):
    G = x.shape[0]

    @pl.when(first)
    def _():
        xs_ref[G] = xprev_ref[0]

    xs_ref[0] = xs_ref[G]
    xs_ref[1:G + 1] = x
    return xs_ref[0:G]


def _rglru_kernel(rx_ref, rg_ref, xprev_ref, h0_ref, cw_ref, cb_ref, wg_ref, bx_ref, ba_ref, l_ref,
                  y_ref, xlast_ref, hlast_ref, *scratch, carry):
    R, C = rx_ref.shape
    G = R // SUBLANES
    x = rx_ref[...].reshape(G, SUBLANES, C)
    if carry:
        xs_ref, a_ref, u_ref, h_ref, hc_ref = scratch
        first = pl.program_id(1) == 0
        xprev = _stage_groups(x, xprev_ref, xs_ref, first)

        @pl.when(first)
        def _():
            hc_ref[...] = h0_ref[0]
    else:
        xprev = xprev_ref[...]

    xc = (_causal_conv(x, xprev, cw_ref) + cb_ref[...][None]).reshape(R, C)
    xb = xc.astype(BF16)
    bs = C // RNN_BLOCKS
    zi, zr = [], []
    for n in range(RNN_BLOCKS):
        z = jnp.dot(xb[:, n * bs:(n + 1) * bs], wg_ref[n], preferred_element_type=F32)
        zi.append(z[:, :bs])
        zr.append(z[:, bs:])
    gi = _sigmoid(jnp.concatenate(zi, axis=1) + bx_ref[...])
    gr = _sigmoid(jnp.concatenate(zr, axis=1) + ba_ref[...])
    lv = l_ref[...]
    log_sig_l = -_softplus(-lv)
    log_a = RG_C * gr * log_sig_l
    a = jnp.exp(log_a)
    u = jnp.sqrt(-jnp.tanh(log_a) * (a * a + 1.0)) * (gi * xc)
    a_cum, h_loc = _segment_scan(a.reshape(G, SUBLANES, C), u.reshape(G, SUBLANES, C))

    if carry:
        a_ref[...] = a_cum
        u_ref[...] = h_loc

        def body(g, h_prev):
            hg = u_ref[g] + a_ref[g] * h_prev
            h_ref[g] = hg
            return hg[SUBLANES - 1:SUBLANES, :]

        h_last = lax.fori_loop(0, G, body, hc_ref[...])
        hc_ref[...] = h_last
        h = h_ref[...]
        xlast_ref[0] = x[G - 1]
        hlast_ref[0] = h_last
    else:
        h = h_loc + a_cum * h0_ref[...]
        xlast_ref[...] = x
        hlast_ref[...] = h[:, SUBLANES - 1:SUBLANES, :]

    y = h.reshape(R, C) * jax.nn.gelu(rg_ref[...])
    y_ref[...] = y.astype(BF16)


def rglru_branch(proj, col_rx, col_rg, xprev, h0, cw, cb, wg, bx, ba, lam, *, n_seq, seq_len, rows):
    C = cw.shape[1]
    carry = seq_len > SUBLANES
    if carry:
        nt = seq_len // rows
        grid = (n_seq, nt)
        row_map = lambda b, t: (b * nt + t)
        nb = 1
        G = rows // SUBLANES
        scratch = [pltpu.VMEM((G + 1, SUBLANES, C), F32), pltpu.VMEM((G, SUBLANES, C), F32),
                   pltpu.VMEM((G, SUBLANES, C), F32), pltpu.VMEM((G, SUBLANES, C), F32),
                   pltpu.VMEM((1, C), F32)]
    else:
        nb = rows // SUBLANES
        grid = (n_seq // nb, 1)
        row_map = lambda b, t: b
        scratch = []
    const2 = lambda b, t: (0, 0)
    in_specs = [pl.BlockSpec((rows, C), lambda b, t: (row_map(b, t), col_rx)),
                pl.BlockSpec((rows, C), lambda b, t: (row_map(b, t), col_rg)),
                pl.BlockSpec((nb, SUBLANES, C), lambda b, t: (b, 0, 0)),
                pl.BlockSpec((nb, 1, C), lambda b, t: (b, 0, 0)),
                pl.BlockSpec((CONV_W, C), const2),
                pl.BlockSpec((1, C), const2),
                pl.BlockSpec(wg.shape, lambda b, t: (0, 0, 0)),
                pl.BlockSpec((1, C), const2), pl.BlockSpec((1, C), const2), pl.BlockSpec((1, C), const2)]
    tokens = n_seq * seq_len
    out_shape = [jax.ShapeDtypeStruct((tokens, C), BF16),
                 jax.ShapeDtypeStruct((n_seq, SUBLANES, C), F32),
                 jax.ShapeDtypeStruct((n_seq, 1, C), F32)]
    out_specs = [pl.BlockSpec((rows, C), lambda b, t: (row_map(b, t), 0)),
                 pl.BlockSpec((nb, SUBLANES, C), lambda b, t: (b, 0, 0)),
                 pl.BlockSpec((nb, 1, C), lambda b, t: (b, 0, 0))]
    return pl.pallas_call(
        functools.partial(_rglru_kernel, carry=carry),
        grid=grid, in_specs=in_specs, out_specs=out_specs, out_shape=out_shape,
        scratch_shapes=scratch,
        compiler_params=_params("parallel", "arbitrary"),
        name="rglru",
    )(proj, proj, xprev, h0, cw, cb.reshape(1, C), wg, bx.reshape(1, C), ba.reshape(1, C),
      lam.reshape(1, C))


def _l2norm_heads(x, scale):
    dh = x.shape[1] // GDN_HEADS
    outs = []
    for h in range(GDN_HEADS):
        xh = x[:, h * dh:(h + 1) * dh]
        xh = xh * lax.rsqrt(jnp.sum(xh * xh, axis=-1, keepdims=True) + EPS)
        outs.append(xh * scale if scale != 1.0 else xh)
    return jnp.concatenate(outs, axis=1)


def _gdn_conv_norm(src_refs, xprev_ref, cw_ref, dst_refs, xlast_ref, xs_ref, carry):
    R, C = src_refs[0].shape
    G = R // SUBLANES
    dk = C // GDN_HEADS
    for s, (src, dst) in enumerate(zip(src_refs, dst_refs)):
        x = src[...].reshape(G, SUBLANES, C)
        cols = slice(s * C, (s + 1) * C)
        if carry:
            xprev = _stage_groups(x, xprev_ref.at[:, :, cols], xs_ref.at[s], pl.program_id(1) == 0)
            xlast_ref[0, :, cols] = x[G - 1]
        else:
            xprev = xprev_ref[:, :, cols]
            xlast_ref[:, :, cols] = x
        y = _silu(_causal_conv(x, xprev, cw_ref.at[:, cols])).reshape(R, C)
        if s == 0:
            y = _l2norm_heads(y, dk ** -0.5)
        elif s == 1:
            y = _l2norm_heads(y, 1.0)
        dst[...] = y


def _segment_cumsum(x, seg):
    pos = lax.broadcasted_iota(jnp.int32, x.shape, 0) & (seg - 1)
    s = 1
    while s < seg:
        x = x + jnp.where(pos >= s, pltpu.roll(x, s, 0), 0.0)
        s *= 2
    return x


def _segment_last(x, seg):
    n = x.shape[0]
    pos = lax.broadcasted_iota(jnp.int32, x.shape, 0) & (seg - 1)
    s = seg // 2
    while s >= 1:
        x = jnp.where((pos & (2 * s - 1)) < s, pltpu.roll(x, n - s, 0), x)
        s //= 2
    return x


class _TileMasks:
    def __init__(self, rows, seg):
        r = lax.broadcasted_iota(jnp.int32, (rows, rows), 0)
        c = lax.broadcasted_iota(jnp.int32, (rows, rows), 1)
        shift = seg.bit_length() - 1
        same = (r >> shift) == (c >> shift)
        self.incl = same & (r >= c)
        self.strict = same & (r > c)
        self.eye = r == c
        self.levels = []
        s = 1
        while s < seg:
            b = s.bit_length() - 1
            self.levels.append(((r >> (b + 1)) == (c >> (b + 1))) & (((r >> b) & 1) == 1) & (((c >> b) & 1) == 0))
            s *= 2


def _unit_lower_inverses(ms, masks):
    eye = jnp.where(masks.eye, 1.0, 0.0)
    xs = [eye - jnp.where(masks.levels[0], m, 0.0) for m in ms]
    for level in masks.levels[1:]:
        ts = [_dot(jnp.where(level, m, 0.0), x) for m, x in zip(ms, xs)]
        xs = [x - _dot(x, t) for x, t in zip(xs, ts)]
    return xs


def _gdn_tiles_local(chains, masks):
    r, d = chains[0][0].shape
    kbs, decays, kk_qks = [], [], []
    for q, k, v, beta, gc, g_last in chains:
        gc_row = jnp.sum(jnp.where(masks.eye, gc, 0.0), axis=0, keepdims=True)
        decays.append(jnp.where(masks.incl, jnp.exp(gc - gc_row), 0.0))
        kbs.append(k * beta)
        kk_qks.append(_dot_nt(jnp.concatenate([kbs[-1], q], axis=0), k))
    ms = [jnp.where(masks.strict, kq[:r] * dec, 0.0) for kq, dec in zip(kk_qks, decays)]
    tms = _unit_lower_inverses(ms, masks)
    out = []
    for (q, k, v, beta, gc, g_last), kb, dec, kq, tm in zip(chains, kbs, decays, kk_qks, tms):
        eg = jnp.exp(gc)
        vk = _dot(tm, jnp.concatenate([v * beta, kb * eg], axis=1))
        out.append(dict(value=vk[:, :d], kcd=vk[:, d:], attn=kq[r:] * dec, qg=q * eg,
                        kd=k * jnp.exp(g_last - gc), decay_last=jnp.exp(g_last)))
    return out


def _gdn_kernel(gq_ref, gk_ref, gv_ref, z_ref, sm_ref, xprev_ref, cw_ref, alog_ref, dtb_ref, ng_ref, s0_ref,
                o_ref, s_ref, xlast_ref, q_ref, k_ref, v_ref, *scratch, tile, seg, n_tiles, carry):
    dk = q_ref.shape[1] // GDN_HEADS
    n_seg = tile // seg
    masks = _TileMasks(tile, seg)
    neg_a = -jnp.exp(alog_ref[...])
    dtb = dtb_ref[...]
    ng = ng_ref[...]
    if carry:
        @pl.when(pl.program_id(1) == 0)
        def _():
            s_ref[...] = s0_ref[...]
    _gdn_conv_norm((gq_ref, gk_ref, gv_ref), xprev_ref, cw_ref, (q_ref, k_ref, v_ref), xlast_ref,
                   scratch[0] if carry else None, carry)

    chains = []
    for i in range(n_tiles):
        rows = slice(i * tile, (i + 1) * tile)
        sm = sm_ref[rows, :]
        beta_all = _sigmoid(sm)
        gc_all = _segment_cumsum(neg_a * _softplus(sm + dtb), seg)
        gl_all = _segment_last(gc_all, seg)
        for h in range(GDN_HEADS):
            cols = slice(h * dk, (h + 1) * dk)
            lane = slice(GDN_HEADS + h, GDN_HEADS + h + 1)
            chains.append((q_ref[rows, cols], k_ref[rows, cols], v_ref[rows, cols],
                           beta_all[:, h:h + 1], gc_all[:, lane], gl_all[:, lane]))
    local = _gdn_tiles_local(chains, masks)

    heads = range(GDN_HEADS)
    for i in range(n_tiles):
        rows = slice(i * tile, (i + 1) * tile)
        loc = local[i * GDN_HEADS:(i + 1) * GDN_HEADS]
        v_new = [[] for _ in heads]
        o_state = [[] for _ in heads]
        for j in range(n_seg):
            sl = slice(j * seg, (j + 1) * seg)
            seq = 0 if carry else i * n_seg + j
            s_prev = [s_ref[seq, h] if carry else s0_ref[seq, h] for h in heads]
            rs = [_dot(jnp.concatenate([loc[h]["kcd"][sl], loc[h]["qg"][sl]], axis=0), s_prev[h])
                  for h in heads]
            vns = [loc[h]["value"][sl] - rs[h][:seg] for h in heads]
            for h in heads:
                v_new[h].append(vns[h])
                o_state[h].append(rs[h][seg:])
                s_ref[seq, h] = (s_prev[h] * loc[h]["decay_last"][j * seg:j * seg + 1, :]
                                 + _dot_tn(loc[h]["kd"][sl], vns[h]))
        for h in heads:
            cols = slice(h * dk, (h + 1) * dk)
            vn = jnp.concatenate(v_new[h], axis=0) if n_seg > 1 else v_new[h][0]
            os_ = jnp.concatenate(o_state[h], axis=0) if n_seg > 1 else o_state[h][0]
            o = os_ + _dot(loc[h]["attn"], vn)
            o_ref[rows, cols] = _rmsnorm(o, ng) * _silu(z_ref[rows, cols])


def gdn_branch(proj, col_q, col_z, small, xprev, cw, alog_row, dtb_row, ng, s0, *, n_seq, seq_len,
               tiles_per_step):
    tokens = proj.shape[0]
    C = cw.shape[1] // 3
    dk = C // GDN_HEADS
    tile = GDN_CHUNK
    carry = seq_len >= GDN_CHUNK
    seg = GDN_CHUNK if carry else seq_len
    rows = tiles_per_step * tile
    if carry:
        nt = seq_len // rows
        seq_per_step = 1
        grid = (n_seq, nt)
    else:
        nt = 1
        seq_per_step = rows // seg
        grid = (n_seq // seq_per_step, 1)
    row_map = lambda b, t: (b * nt + t)
    tok_spec = pl.BlockSpec((rows, C), lambda b, t: (row_map(b, t), 0))
    const2 = lambda b, t: (0, 0)
    s_spec = pl.BlockSpec((seq_per_step, GDN_HEADS, dk, dk), lambda b, t: (b, 0, 0, 0))
    conv_spec = pl.BlockSpec((seq_per_step, SUBLANES, 3 * C), lambda b, t: (b, 0, 0))
    in_specs = [pl.BlockSpec((rows, C), lambda b, t, c=c: (row_map(b, t), c))
                for c in (col_q, col_q + 1, col_q + 2, col_z)]
    in_specs += [pl.BlockSpec((rows, LANES), lambda b, t: (row_map(b, t), 0)),
                 conv_spec, pl.BlockSpec((CONV_W, 3 * C), const2),
                 pl.BlockSpec((1, LANES), const2), pl.BlockSpec((1, LANES), const2),
                 pl.BlockSpec((1, dk), const2), s_spec]
    scratch = [pltpu.VMEM((rows, C), F32)] * 3
    if carry:
        scratch.append(pltpu.VMEM((3, rows // SUBLANES + 1, SUBLANES, C), F32))
    return pl.pallas_call(
        functools.partial(_gdn_kernel, tile=tile, seg=seg, n_tiles=tiles_per_step, carry=carry),
        grid=grid, in_specs=in_specs,
        out_specs=[tok_spec, s_spec, conv_spec],
        out_shape=[jax.ShapeDtypeStruct((tokens, C), F32), jax.ShapeDtypeStruct(s0.shape, F32),
                   jax.ShapeDtypeStruct((n_seq, SUBLANES, 3 * C), F32)],
        scratch_shapes=scratch,
        compiler_params=_params("parallel", "arbitrary"),
        name="gdn_branch",
    )(proj, proj, proj, proj, small, xprev, cw, alog_row, dtb_row, ng.reshape(1, dk), s0)


def _mem_attn_kernel(q_ref, k_ref, v_ref, o_ref, *, n_seq, tq):
    hd = q_ref.shape[1] // MEM_HEADS
    scale = hd ** -0.5
    rows = lambda s: slice(s * tq, (s + 1) * tq)
    cols = lambda h: slice(h * hd, (h + 1) * hd)
    if len(k_ref.shape) == 4:
        mem_rows = k_ref.shape[1] * MEM_HEADS
        r_head = lax.broadcasted_iota(jnp.int32, (MEM_HEADS * tq, mem_rows), 0) // tq
        c_head = lax.broadcasted_iota(jnp.int32, (MEM_HEADS * tq, mem_rows), 1) & (MEM_HEADS - 1)
        own = r_head == c_head
        scores = []
        for s in range(n_seq):
            q_all = jnp.concatenate([q_ref[rows(s), cols(h)] for h in range(MEM_HEADS)], axis=0)
            sc = _dot_nt(q_all, k_ref[s].reshape(mem_rows, hd)) * scale
            scores.append(jnp.where(own, sc, -jnp.inf))
        probs = []
        for sc in scores:
            e = jnp.exp(sc - jnp.max(sc, axis=-1, keepdims=True))
            probs.append(e / jnp.sum(e, axis=-1, keepdims=True))
        for s, p in enumerate(probs):
            o_all = _dot(p, v_ref[s].reshape(mem_rows, hd))
            for h in range(MEM_HEADS):
                o_ref[rows(s), cols(h)] = o_all[h * tq:(h + 1) * tq]
        return
    pairs = [(s, h) for s in range(n_seq) for h in range(MEM_HEADS)]
    mem = lambda ref, s, h: ref[s, :, cols(h)]
    scores = [_dot_nt(q_ref[rows(s), cols(h)], mem(k_ref, s, h)) * scale for s, h in pairs]
    probs = []
    for sc in scores:
        e = jnp.exp(sc - jnp.max(sc, axis=-1, keepdims=True))
        probs.append(e / jnp.sum(e, axis=-1, keepdims=True))
    for (s, h), p in zip(pairs, probs):
        o_ref[rows(s), cols(h)] = _dot(p, mem(v_ref, s, h))


def mem_attention(proj, col_q, mem_k, col_k, mem_v, col_v, *, width, n_seq, seq_len, tq, seq_per_step):
    tokens = n_seq * seq_len
    mem_len = mem_k.shape[1]
    C = width
    nt = seq_len // tq
    rows = seq_per_step * tq
    grid = (n_seq // seq_per_step, nt)
    row_map = lambda b, t: (b * nt + t)
    if mem_k.ndim == 4:
        blk = (seq_per_step,) + mem_k.shape[1:]
        k_spec = v_spec = pl.BlockSpec(blk, lambda b, t: (b, 0, 0, 0))
    else:
        k_spec = pl.BlockSpec((seq_per_step, mem_len, C), lambda b, t: (b, 0, col_k))
        v_spec = pl.BlockSpec((seq_per_step, mem_len, C), lambda b, t: (b, 0, col_v))
    return pl.pallas_call(
        functools.partial(_mem_attn_kernel, n_seq=seq_per_step, tq=tq),
        grid=grid,
        in_specs=[pl.BlockSpec((rows, C), lambda b, t: (row_map(b, t), col_q)), k_spec, v_spec],
        out_specs=pl.BlockSpec((rows, C), lambda b, t: (row_map(b, t), 0)),
        out_shape=jax.ShapeDtypeStruct((tokens, C), F32),
        compiler_params=_params("parallel", "arbitrary"),
        name="mem_attention",
    )(proj, mem_k, mem_v)


def _merge_kernel(yr_ref, yg_ref, ym_ref, g0_ref, g1_ref, g2_ref, w0_ref, w1_ref, w2_ref, o_ref):
    acc = _sigmoid(g0_ref[...]) * _dot(yr_ref[...], w0_ref[...])
    acc = acc + _sigmoid(g1_ref[...]) * _dot(yg_ref[...], w1_ref[...])
    acc = acc + _sigmoid(g2_ref[...]) * _dot(ym_ref[...], w2_ref[...])
    o_ref[...] = acc.astype(BF16)


def merge_branches(y_rnn, y_gdn, y_mem, proj, col_gate, w_rnn_up, w_gdn_up, w_mem_up, *, tm, tn):
    M, C = y_rnn.shape
    N = w_rnn_up.shape[1]
    nj = N // tn
    y_spec = pl.BlockSpec((tm, C), lambda i, j: (i, 0))
    w_spec = pl.BlockSpec((C, tn), lambda i, j: (0, j))
    gate_specs = [pl.BlockSpec((tm, tn), lambda i, j, b=b: (i, col_gate + b * nj + j)) for b in range(N_BRANCH)]
    return pl.pallas_call(
        _merge_kernel,
        grid=(M // tm, nj),
        in_specs=[y_spec, y_spec, y_spec] + gate_specs + [w_spec, w_spec, w_spec],
        out_specs=pl.BlockSpec((tm, tn), lambda i, j: (i, j)),
        out_shape=jax.ShapeDtypeStruct((M, N), BF16),
        compiler_params=_params("parallel", "arbitrary"),
        name="merge_branches",
    )(y_rnn, y_gdn, y_mem, proj, proj, proj, w_rnn_up, w_gdn_up, w_mem_up)


def _matmul_residual_kernel(a_ref, w_ref, x_ref, o_ref):
    o_ref[...] = x_ref[...] + _dot(a_ref[...], w_ref[...])


def matmul_residual(a, w, x, *, tm, tn):
    M, K = a.shape
    N = w.shape[1]
    return pl.pallas_call(
        _matmul_residual_kernel,
        grid=(M // tm, N // tn),
        in_specs=[pl.BlockSpec((tm, K), lambda i, j: (i, 0)),
                  pl.BlockSpec((K, tn), lambda i, j: (0, j)),
                  pl.BlockSpec((tm, tn), lambda i, j: (i, j))],
        out_specs=pl.BlockSpec((tm, tn), lambda i, j: (i, j)),
        out_shape=jax.ShapeDtypeStruct((M, N), F32),
        compiler_params=_params("parallel", "arbitrary"),
        name="matmul_residual",
    )(a, w, x)


def _mlp_kernel(x_ref, g_ref, wu_ref, wd_ref, gf_ref, o_ref, xn_ref):
    j = pl.program_id(1)

    @pl.when(j == 0)
    def _():
        xn_ref[...] = _rmsnorm(x_ref[...], g_ref[...]).astype(BF16)
        o_ref[...] = jnp.zeros_like(o_ref)

    hid = _dot(xn_ref[...], wu_ref[...])
    act = jnp.square(jnp.maximum(hid, 0.0))
    o_ref[...] += _dot(act, wd_ref[...])

    @pl.when(j == pl.num_programs(1) - 1)
    def _():
        o_ref[...] = _rmsnorm(x_ref[...] + o_ref[...], gf_ref[...])


def mlp_final_norm(x, g, w_up, w_down, g_final, *, tm, tf):
    M, D = x.shape
    FF = w_up.shape[1]
    return pl.pallas_call(
        _mlp_kernel,
        grid=(M // tm, FF // tf),
        in_specs=[pl.BlockSpec((tm, D), lambda i, j: (i, 0), pipeline_mode=pl.Buffered(1)),
                  pl.BlockSpec((1, D), lambda i, j: (0, 0)),
                  pl.BlockSpec((D, tf), lambda i, j: (0, j)),
                  pl.BlockSpec((tf, D), lambda i, j: (j, 0)),
                  pl.BlockSpec((1, D), lambda i, j: (0, 0))],
        out_specs=pl.BlockSpec((tm, D), lambda i, j: (i, 0)),
        out_shape=jax.ShapeDtypeStruct((M, D), F32),
        scratch_shapes=[pltpu.VMEM((tm, D), BF16)],
        compiler_params=_params("parallel", "arbitrary"),
        name="mlp_final_norm",
    )(x, g.reshape(1, D), w_up, w_down, g_final.reshape(1, D))


def _in_proj_segments(d_model):
    half = d_model // 2
    small_row = 6 * half
    mq_row = small_row + 2 * GDN_HEADS
    mg_row = mq_row + half
    return ((mg_row, N_BRANCH * d_model), (0, 6 * half), (mq_row, half)), small_row


def _prep_layer_weights(rnn_wx, rnn_wa, gdn_A_log, gdn_dt_bias):
    w_gate = jnp.concatenate([rnn_wx, rnn_wa], axis=-1).astype(BF16)
    lane_pad = (GDN_HEADS, LANES - 2 * GDN_HEADS)
    alog_row = jnp.pad(gdn_A_log, lane_pad).reshape(1, LANES)
    dtb_row = jnp.pad(gdn_dt_bias, lane_pad).reshape(1, LANES)
    return w_gate, alog_row, dtb_row


def _pad_conv_state(buf):
    return jnp.pad(buf, ((0, 0), (SUBLANES - (CONV_W - 1), 0), (0, 0)))


def _group_layer(x, mem_k, col_k, mem_v, col_v, rnn_buf, rnn_h0, gdn_buf, gdn_s0, lw, *, n_seq, seq_len,
                 tm, proj_tm, row_tile, attn_tq, attn_seqs, gdn_tiles, final_g):
    D = x.shape[1]
    half = D // 2
    n_gate_blk = N_BRANCH * D // half
    c_rx, c_rg, c_gq, c_gz, c_mq = (n_gate_blk + i for i in (0, 1, 2, 5, 6))
    segments, small_row = _in_proj_segments(D)
    proj, small = in_projection(x, lw["norm_mix_g"], lw["w_in_t"], segments, small_row, tm=proj_tm, tn=512)

    y_rnn, rnn_last, h_last = rglru_branch(
        proj, c_rx, c_rg, _pad_conv_state(rnn_buf), rnn_h0.reshape(n_seq, 1, half),
        lw["rnn_conv_w"], lw["rnn_conv_b"], lw["w_gate"], lw["rnn_bx"], lw["rnn_ba"], lw["rnn_L"],
        n_seq=n_seq, seq_len=seq_len, rows=row_tile)

    y_gdn, s_new, gdn_last = gdn_branch(
        proj, c_gq, c_gz, small, _pad_conv_state(gdn_buf), lw["gdn_conv_w"], lw["alog_row"], lw["dtb_row"],
        lw["gdn_norm_g"], gdn_s0, n_seq=n_seq, seq_len=seq_len, tiles_per_step=gdn_tiles)

    y_mem = mem_attention(proj, c_mq, mem_k, col_k, mem_v, col_v, width=half, n_seq=n_seq, seq_len=seq_len,
                          tq=attn_tq, seq_per_step=attn_seqs)

    merged = merge_branches(y_rnn, y_gdn, y_mem, proj, 0, lw["w_rnn_up"], lw["w_gdn_up"], lw["w_mem_up"],
                            tm=tm, tn=512)
    x1 = matmul_residual(merged, lw["w_out"], x, tm=tm, tn=512)
    x2 = mlp_final_norm(x1, lw["norm_mlp_g"], lw["w_mlp_up"], lw["w_mlp_down"], final_g, tm=tm, tf=512)
    states = (rnn_last[:, SUBLANES - (CONV_W - 1):], h_last.reshape(n_seq, half),
              gdn_last[:, SUBLANES - (CONV_W - 1):], s_new)
    return x2, states


def kernel(x_prompt, x_sample, mem_prompt, cache_mem_k, cache_mem_v, state_rnn_conv, state_rnn_h,
           state_gdn_conv, state_gdn_S, norm_mix_g, w_in, rnn_conv_w, rnn_conv_b, rnn_wx, rnn_bx, rnn_wa,
           rnn_ba, rnn_L, gdn_conv_w, gdn_A_log, gdn_dt_bias, gdn_norm_g, mem_norm_g, w_mem_kv, w_rnn_up,
           w_gdn_up, w_mem_up, w_out, norm_mlp_g, w_mlp_up, w_mlp_down, norm_final_g):
    depth = w_in.shape[0]
    assert depth == 1, "the final norm is fused into the last layer's MLP kernel; one layer supported"
    Bp, T, D = x_prompt.shape
    Bs, Ts, _ = x_sample.shape
    half = D // 2
    mem_len = mem_prompt.shape[1]
    assert Ts == SUBLANES and T % GDN_CHUNK == 0
    l = 0
    w_gate, alog_row, dtb_row = _prep_layer_weights(rnn_wx[l], rnn_wa[l], gdn_A_log[l], gdn_dt_bias[l])
    lw = dict(norm_mix_g=norm_mix_g[l], w_in_t=jnp.swapaxes(w_in[l], 0, 1), w_gate=w_gate,
              rnn_conv_w=rnn_conv_w[l], rnn_conv_b=rnn_conv_b[l], rnn_bx=rnn_bx[l], rnn_ba=rnn_ba[l],
              rnn_L=rnn_L[l], gdn_conv_w=gdn_conv_w[l], alog_row=alog_row, dtb_row=dtb_row,
              gdn_norm_g=gdn_norm_g[l], w_rnn_up=w_rnn_up[l], w_gdn_up=w_gdn_up[l],
              w_mem_up=w_mem_up[l], w_out=w_out[l], norm_mlp_g=norm_mlp_g[l],
              w_mlp_up=w_mlp_up[l], w_mlp_down=w_mlp_down[l])

    kv = norm_matmul(mem_prompt.reshape(Bp * mem_len, D), mem_norm_g[l], w_mem_kv[l],
                     tm=min(Bp * mem_len, 1024), tn=512)
    kv3 = kv.reshape(Bp, mem_len, 2 * half)
    zeros = lambda *s: jnp.zeros(s, F32)
    yp, (rb_p, rh_p, gb_p, gs_p) = _group_layer(
        x_prompt.reshape(Bp * T, D), kv3, 0, kv3, 1,
        zeros(Bp, CONV_W - 1, half), zeros(Bp, half), zeros(Bp, CONV_W - 1, 3 * half),
        zeros(Bp, GDN_HEADS, half // GDN_HEADS, half // GDN_HEADS), lw,
        n_seq=Bp, seq_len=T, tm=min(Bp * T, 1024), proj_tm=min(Bp * T, 2048), row_tile=256, attn_tq=512, attn_seqs=1, gdn_tiles=2,
        final_g=norm_final_g)
    mk_p = kv3[:, :, :half].reshape(1, Bp, mem_len, MEM_HEADS, half // MEM_HEADS)
    mv_p = kv3[:, :, half:].reshape(1, Bp, mem_len, MEM_HEADS, half // MEM_HEADS)

    ys, (rb_s, rh_s, gb_s, gs_s) = _group_layer(
        x_sample.reshape(Bs * Ts, D), cache_mem_k.reshape((depth * Bs,) + cache_mem_k.shape[2:]), 0,
        cache_mem_v.reshape((depth * Bs,) + cache_mem_v.shape[2:]), 0,
        state_rnn_conv[l], state_rnn_h[l], state_gdn_conv[l], state_gdn_S[l], lw,
        n_seq=Bs, seq_len=Ts, tm=min(Bs * Ts, 1024), proj_tm=min(Bs * Ts, 1024), row_tile=min(Bs, 16) * SUBLANES, attn_tq=Ts,
        attn_seqs=min(Bs, 4), gdn_tiles=1, final_g=norm_final_g)

    return (yp.reshape(Bp, T, D), ys.reshape(Bs, Ts, D), mk_p, mv_p, rb_p[None], rh_p[None], gb_p[None],
            gs_p[None], rb_s[None], rh_s[None], gb_s[None], gs_s[None])
```

```python
import functools

import jax
import jax.numpy as jnp
from jax import lax
from jax.experimental import pallas as pl
from jax.experimental.pallas import tpu as pltpu

F32 = jnp.float32
BF16 = jnp.bfloat16

EPS = 1e-6
RG_C = 8.0
CONV_W = 4
RNN_BLOCKS = 8
GDN_HEADS = 8
GDN_CHUNK = 64
MEM_HEADS = 4
N_BRANCH = 3

SUBLANES = 8
BF16_SUBLANES = 16
LANES = 128
VMEM_LIMIT_BYTES = 56 * 1024 * 1024


def _lhs_dtype(slab_rows):
    return BF16 if slab_rows % BF16_SUBLANES == 0 else F32


def _params(*sem):
    return pltpu.CompilerParams(dimension_semantics=sem, vmem_limit_bytes=VMEM_LIMIT_BYTES)


def _dot(a, b):
    return jnp.dot(a.astype(BF16), b.astype(BF16), preferred_element_type=F32)


def _dot_nt(a, b):
    return lax.dot_general(a.astype(BF16), b.astype(BF16), (((1,), (1,)), ((), ())),
                           preferred_element_type=F32)


def _dot_tn(a, b):
    return lax.dot_general(a.astype(BF16), b.astype(BF16), (((0,), (0,)), ((), ())),
                           preferred_element_type=F32)


def _rmsnorm(x, g):
    return (x * lax.rsqrt(jnp.mean(x * x, axis=-1, keepdims=True) + EPS)) * g


def _softplus(x):
    return jnp.maximum(x, 0.0) + jnp.log1p(jnp.exp(-jnp.abs(x)))


_sigmoid = jax.nn.sigmoid


def _silu(x):
    return x * _sigmoid(x)


def _norm_matmul_kernel(*refs, has_small):
    if has_small:
        x_ref, g_ref, w_ref, ws_ref, o_ref, os_ref, xn_ref = refs
    else:
        x_ref, g_ref, w_ref, o_ref, xn_ref = refs

    @pl.when(pl.program_id(1) == 0)
    def _():
        xn_ref[...] = _rmsnorm(x_ref[...], g_ref[...]).astype(BF16)
        if has_small:
            os_ref[...] = jnp.dot(xn_ref[...], ws_ref[...], preferred_element_type=F32)

    o_ref[...] = _dot(xn_ref[...], w_ref[...])


def norm_matmul(x, g, w, w_small=None, *, tm, tn):
    M, K = x.shape
    N = w.shape[1]
    has_small = w_small is not None
    in_specs = [pl.BlockSpec((tm, K), lambda i, j: (i, 0)),
                pl.BlockSpec((1, K), lambda i, j: (0, 0)),
                pl.BlockSpec((K, tn), lambda i, j: (0, j))]
    out_shape = [jax.ShapeDtypeStruct((M, N), F32)]
    out_specs = [pl.BlockSpec((tm, tn), lambda i, j: (i, j))]
    args = [x, g.reshape(1, K), w]
    if has_small:
        ns = w_small.shape[1]
        in_specs.append(pl.BlockSpec((K, ns), lambda i, j: (0, 0)))
        out_shape.append(jax.ShapeDtypeStruct((M, ns), F32))
        out_specs.append(pl.BlockSpec((tm, ns), lambda i, j: (i, 0)))
        args.append(w_small)
    outs = pl.pallas_call(
        functools.partial(_norm_matmul_kernel, has_small=has_small),
        grid=(M // tm, N // tn),
        in_specs=in_specs, out_specs=out_specs, out_shape=out_shape,
        scratch_shapes=[pltpu.VMEM((tm, K), BF16)],
        compiler_params=_params("parallel", "arbitrary"),
        name="norm_matmul",
    )(*args)
    return outs if has_small else outs[0]


def _in_proj_kernel(x_ref, g_ref, wt_ref, wst_ref, o_ref, os_ref, xn_ref):
    @pl.when(pl.program_id(1) == 0)
    def _():
        xn_ref[...] = _rmsnorm(x_ref[...], g_ref[...]).astype(BF16)
        os_ref[...] = _dot_nt(xn_ref[...], wst_ref[...])

    o_ref[...] = _dot_nt(xn_ref[...], wt_ref[...])


def in_projection(x, g, w_t, segments, small_row, *, tm, tn):
    M, K = x.shape
    n_tiles = [n // tn for _, n in segments]
    assert all(n % tn == 0 and r % SUBLANES == 0 for r, n in segments) and small_row % LANES == 0
    N = tn * sum(n_tiles)

    def w_row(j):
        row, first = None, 0
        for (r0, _), nt in zip(segments, n_tiles):
            cand = r0 // SUBLANES + (tn // SUBLANES) * (j - first)
            row = cand if row is None else jnp.where(j >= first, cand, row)
            first += nt
        return row * SUBLANES

    outs = pl.pallas_call(
        _in_proj_kernel,
        grid=(M // tm, N // tn),
        in_specs=[pl.BlockSpec((tm, K), lambda i, j: (i, 0), pipeline_mode=pl.Buffered(1)),
                  pl.BlockSpec((1, K), lambda i, j: (0, 0)),
                  pl.BlockSpec((pl.Element(tn), pl.Element(K)), lambda i, j: (w_row(j), 0)),
                  pl.BlockSpec((LANES, K), lambda i, j: (small_row // LANES, 0))],
        out_specs=[pl.BlockSpec((tm, tn), lambda i, j: (i, j)), pl.BlockSpec((tm, LANES), lambda i, j: (i, 0))],
        out_shape=[jax.ShapeDtypeStruct((M, N), F32), jax.ShapeDtypeStruct((M, LANES), F32)],
        scratch_shapes=[pltpu.VMEM((tm, K), BF16)],
        compiler_params=_params("parallel", "arbitrary"),
        name="in_projection",
    )(x, g.reshape(1, K), w_t, w_t)
    return outs


def _causal_conv(x, xprev, w_ref):
    t_idx = lax.broadcasted_iota(jnp.int32, x.shape, 1)
    y = x * w_ref[CONV_W - 1:CONV_W, :][None]
    for k in range(1, CONV_W):
        shifted = jnp.where(t_idx >= k, pltpu.roll(x, k, 1), pltpu.roll(xprev, k, 1))
        y = y + shifted * w_ref[CONV_W - 1 - k:CONV_W - k, :][None]
    return y


def _segment_scan(a, u):
    t_idx = lax.broadcasted_iota(jnp.int32, a.shape, 1)
    s = 1
    while s < SUBLANES:
        keep = t_idx >= s
        u = jnp.where(keep, a * pltpu.roll(u, s, 1) + u, u)
        a = jnp.where(keep, a * pltpu.roll(a, s, 1), a)
        s *= 2
    return a, u


def _stage_groups(x, xprev_ref, xs_ref, first):
    G = x.shape[0]

    @pl.when(first)
    def _():
        xs_ref[G] = xprev_ref[0]

    xs_ref[0] = xs_ref[G]
    xs_ref[1:G + 1] = x
    return xs_ref[0:G]


def _rglru_kernel(rx_ref, rg_ref, xprev_ref, h0_ref, cw_ref, cb_ref, wg_ref, bx_ref, ba_ref, l_ref,
                  y_ref, xlast_ref, hlast_ref, *scratch, carry):
    R, C = rx_ref.shape
    G = R // SUBLANES
    x = rx_ref[...].reshape(G, SUBLANES, C)
    if carry:
        xs_ref, a_ref, u_ref, h_ref, hc_ref = scratch
        first = pl.program_id(1) == 0
        xprev = _stage_groups(x, xprev_ref, xs_ref, first)

        @pl.when(first)
        def _():
            hc_ref[...] = h0_ref[0]
    else:
        xprev = xprev_ref[...]

    xc = (_causal_conv(x, xprev, cw_ref) + cb_ref[...][None]).reshape(R, C)
    xb = xc.astype(BF16)
    bs = C // RNN_BLOCKS
    zi, zr = [], []
    for n in range(RNN_BLOCKS):
        z = jnp.dot(xb[:, n * bs:(n + 1) * bs], wg_ref[n], preferred_element_type=F32)
        zi.append(z[:, :bs])
        zr.append(z[:, bs:])
    gi = _sigmoid(jnp.concatenate(zi, axis=1) + bx_ref[...])
    gr = _sigmoid(jnp.concatenate(zr, axis=1) + ba_ref[...])
    lv = l_ref[...]
    log_sig_l = -_softplus(-lv)
    log_a = RG_C * gr * log_sig_l
    a = jnp.exp(log_a)
    u = jnp.sqrt(-jnp.tanh(log_a) * (a * a + 1.0)) * (gi * xc)
    a_cum, h_loc = _segment_scan(a.reshape(G, SUBLANES, C), u.reshape(G, SUBLANES, C))

    if carry:
        a_ref[...] = a_cum
        u_ref[...] = h_loc

        def body(g, h_prev):
            hg = u_ref[g] + a_ref[g] * h_prev
            h_ref[g] = hg
            return hg[SUBLANES - 1:SUBLANES, :]

        h_last = lax.fori_loop(0, G, body, hc_ref[...])
        hc_ref[...] = h_last
        h = h_ref[...]
        xlast_ref[0] = x[G - 1]
        hlast_ref[0] = h_last
    else:
        h = h_loc + a_cum * h0_ref[...]
        xlast_ref[...] = x
        hlast_ref[...] = h[:, SUBLANES - 1:SUBLANES, :]

    y = h.reshape(R, C) * jax.nn.gelu(rg_ref[...])
    y_ref[...] = y.astype(BF16)


def rglru_branch(proj, col_rx, col_rg, xprev, h0, cw, cb, wg, bx, ba, lam, *, n_seq, seq_len, rows):
    C = cw.shape[1]
    carry = seq_len > SUBLANES
    if carry:
        nt = seq_len // rows
        grid = (n_seq, nt)
        row_map = lambda b, t: (b * nt + t)
        nb = 1
        G = rows // SUBLANES
        scratch = [pltpu.VMEM((G + 1, SUBLANES, C), F32), pltpu.VMEM((G, SUBLANES, C), F32),
                   pltpu.VMEM((G, SUBLANES, C), F32), pltpu.VMEM((G, SUBLANES, C), F32),
                   pltpu.VMEM((1, C), F32)]
    else:
        nb = rows // SUBLANES
        grid = (n_seq // nb, 1)
        row_map = lambda b, t: b
        scratch = []
    const2 = lambda b, t: (0, 0)
    in_specs = [pl.BlockSpec((rows, C), lambda b, t: (row_map(b, t), col_rx)),
                pl.BlockSpec((rows, C), lambda b, t: (row_map(b, t), col_rg)),
                pl.BlockSpec((nb, SUBLANES, C), lambda b, t: (b, 0, 0)),
                pl.BlockSpec((nb, 1, C), lambda b, t: (b, 0, 0)),
                pl.BlockSpec((CONV_W, C), const2),
                pl.BlockSpec((1, C), const2),
                pl.BlockSpec(wg.shape, lambda b, t: (0, 0, 0)),
                pl.BlockSpec((1, C), const2), pl.BlockSpec((1, C), const2), pl.BlockSpec((1, C), const2)]
    tokens = n_seq * seq_len
    out_shape = [jax.ShapeDtypeStruct((tokens, C), BF16),
                 jax.ShapeDtypeStruct((n_seq, SUBLANES, C), F32),
                 jax.ShapeDtypeStruct((n_seq, 1, C), F32)]
    out_specs = [pl.BlockSpec((rows, C), lambda b, t: (row_map(b, t), 0)),
                 pl.BlockSpec((nb, SUBLANES, C), lambda b, t: (b, 0, 0)),
                 pl.BlockSpec((nb, 1, C), lambda b, t: (b, 0, 0))]
    return pl.pallas_call(
        functools.partial(_rglru_kernel, carry=carry),
        grid=grid, in_specs=in_specs, out_specs=out_specs, out_shape=out_shape,
        scratch_shapes=scratch,
        compiler_params=_params("parallel", "arbitrary"),
        name="rglru",
    )(proj, proj, xprev, h0, cw, cb.reshape(1, C), wg, bx.reshape(1, C), ba.reshape(1, C),
      lam.reshape(1, C))


def _l2norm_heads(x, scale):
    dh = x.shape[1] // GDN_HEADS
    outs = []
    for h in range(GDN_HEADS):
        xh = x[:, h * dh:(h + 1) * dh]
        xh = xh * lax.rsqrt(jnp.sum(xh * xh, axis=-1, keepdims=True) + EPS)
        outs.append(xh * scale if scale != 1.0 else xh)
    return jnp.concatenate(outs, axis=1)


def _gdn_conv_norm(src_refs, xprev_ref, cw_ref, dst_refs, xlast_ref, xs_ref, carry):
    R, C = src_refs[0].shape
    G = R // SUBLANES
    dk = C // GDN_HEADS
    for s, (src, dst) in enumerate(zip(src_refs, dst_refs)):
        x = src[...].reshape(G, SUBLANES, C)
        cols = slice(s * C, (s + 1) * C)
        if carry:
            xprev = _stage_groups(x, xprev_ref.at[:, :, cols], xs_ref.at[s], pl.program_id(1) == 0)
            xlast_ref[0, :, cols] = x[G - 1]
        else:
            xprev = xprev_ref[:, :, cols]
            xlast_ref[:, :, cols] = x
        y = _silu(_causal_conv(x, xprev, cw_ref.at[:, cols])).reshape(R, C)
        if s == 0:
            y = _l2norm_heads(y, dk ** -0.5)
        elif s == 1:
            y = _l2norm_heads(y, 1.0)
        dst[...] = y


def _segment_cumsum(x, seg):
    pos = lax.broadcasted_iota(jnp.int32, x.shape, 0) & (seg - 1)
    s = 1
    while s < seg:
        x = x + jnp.where(pos >= s, pltpu.roll(x, s, 0), 0.0)
        s *= 2
    return x


def _segment_last(x, seg):
    n = x.shape[0]
    pos = lax.broadcasted_iota(jnp.int32, x.shape, 0) & (seg - 1)
    s = seg // 2
    while s >= 1:
        x = jnp.where((pos & (2 * s - 1)) < s, pltpu.roll(x, n - s, 0), x)
        s //= 2
    return x


class _TileMasks:
    def __init__(self, rows, seg):
        r = lax.broadcasted_iota(jnp.int32, (rows, rows), 0)
        c = lax.broadcasted_iota(jnp.int32, (rows, rows), 1)
        shift = seg.bit_length() - 1
        same = (r >> shift) == (c >> shift)
        self.incl = same & (r >= c)
        self.strict = same & (r > c)
        self.eye = r == c
        self.levels = []
        s = 1
        while s < seg:
            b = s.bit_length() - 1
            self.levels.append(((r >> (b + 1)) == (c >> (b + 1))) & (((r >> b) & 1) == 1) & (((c >> b) & 1) == 0))
            s *= 2


def _unit_lower_inverses(ms, masks):
    eye = jnp.where(masks.eye, 1.0, 0.0)
    xs = [eye - jnp.where(masks.levels[0], m, 0.0) for m in ms]
    for level in masks.levels[1:]:
        ts = [_dot(jnp.where(level, m, 0.0), x) for m, x in zip(ms, xs)]
        xs = [x - _dot(x, t) for x, t in zip(xs, ts)]
    return xs


def _gdn_tiles_local(chains, masks):
    r, d = chains[0][0].shape
    kbs, decays, kk_qks = [], [], []
    for q, k, v, beta, gc, g_last in chains:
        gc_row = jnp.sum(jnp.where(masks.eye, gc, 0.0), axis=0, keepdims=True)
        decays.append(jnp.where(masks.incl, jnp.exp(gc - gc_row), 0.0))
        kbs.append(k * beta)
        kk_qks.append(_dot_nt(jnp.concatenate([kbs[-1], q], axis=0), k))
    ms = [jnp.where(masks.strict, kq[:r] * dec, 0.0) for kq, dec in zip(kk_qks, decays)]
    tms = _unit_lower_inverses(ms, masks)
    out = []
    for (q, k, v, beta, gc, g_last), kb, dec, kq, tm in zip(chains, kbs, decays, kk_qks, tms):
        eg = jnp.exp(gc)
        vk = _dot(tm, jnp.concatenate([v * beta, kb * eg], axis=1))
        out.append(dict(value=vk[:, :d], kcd=vk[:, d:], attn=kq[r:] * dec, qg=q * eg,
                        kd=k * jnp.exp(g_last - gc), decay_last=jnp.exp(g_last)))
    return out


def _gdn_kernel(gq_ref, gk_ref, gv_ref, z_ref, sm_ref, xprev_ref, cw_ref, alog_ref, dtb_ref, ng_ref, s0_ref,
                o_ref, s_ref, xlast_ref, q_ref, k_ref, v_ref, *scratch, tile, seg, n_tiles, carry):
    dk = q_ref.shape[1] // GDN_HEADS
    n_seg = tile // seg
    masks = _TileMasks(tile, seg)
    neg_a = -jnp.exp(alog_ref[...])
    dtb = dtb_ref[...]
    ng = ng_ref[...]
    if carry:
        @pl.when(pl.program_id(1) == 0)
        def _():
            s_ref[...] = s0_ref[...]
    _gdn_conv_norm((gq_ref, gk_ref, gv_ref), xprev_ref, cw_ref, (q_ref, k_ref, v_ref), xlast_ref,
                   scratch[0] if carry else None, carry)

    chains = []
    for i in range(n_tiles):
        rows = slice(i * tile, (i + 1) * tile)
        sm = sm_ref[rows, :]
        beta_all = _sigmoid(sm)
        gc_all = _segment_cumsum(neg_a * _softplus(sm + dtb), seg)
        gl_all = _segment_last(gc_all, seg)
        for h in range(GDN_HEADS):
            cols = slice(h * dk, (h + 1) * dk)
            lane = slice(GDN_HEADS + h, GDN_HEADS + h + 1)
            chains.append((q_ref[rows, cols], k_ref[rows, cols], v_ref[rows, cols],
                           beta_all[:, h:h + 1], gc_all[:, lane], gl_all[:, lane]))
    local = _gdn_tiles_local(chains, masks)

    heads = range(GDN_HEADS)
    for i in range(n_tiles):
        rows = slice(i * tile, (i + 1) * tile)
        loc = local[i * GDN_HEADS:(i + 1) * GDN_HEADS]
        v_new = [[] for _ in heads]
        o_state = [[] for _ in heads]
        for j in range(n_seg):
            sl = slice(j * seg, (j + 1) * seg)
            seq = 0 if carry else i * n_seg + j
            s_prev = [s_ref[seq, h] if carry else s0_ref[seq, h] for h in heads]
            rs = [_dot(jnp.concatenate([loc[h]["kcd"][sl], loc[h]["qg"][sl]], axis=0), s_prev[h])
                  for h in heads]
            vns = [loc[h]["value"][sl] - rs[h][:seg] for h in heads]
            for h in heads:
                v_new[h].append(vns[h])
                o_state[h].append(rs[h][seg:])
                s_ref[seq, h] = (s_prev[h] * loc[h]["decay_last"][j * seg:j * seg + 1, :]
                                 + _dot_tn(loc[h]["kd"][sl], vns[h]))
        for h in heads:
            cols = slice(h * dk, (h + 1) * dk)
            vn = jnp.concatenate(v_new[h], axis=0) if n_seg > 1 else v_new[h][0]
            os_ = jnp.concatenate(o_state[h], axis=0) if n_seg > 1 else o_state[h][0]
            o = os_ + _dot(loc[h]["attn"], vn)
            o_ref[rows, cols] = (_rmsnorm(o, ng) * _silu(z_ref[rows, cols])).astype(o_ref.dtype)


def gdn_branch(proj, col_q, col_z, small, xprev, cw, alog_row, dtb_row, ng, s0, *, n_seq, seq_len,
               tiles_per_step):
    tokens = proj.shape[0]
    C = cw.shape[1] // 3
    dk = C // GDN_HEADS
    tile = GDN_CHUNK
    carry = seq_len >= GDN_CHUNK
    seg = GDN_CHUNK if carry else seq_len
    rows = tiles_per_step * tile
    if carry:
        nt = seq_len // rows
        seq_per_step = 1
        grid = (n_seq, nt)
    else:
        nt = 1
        seq_per_step = rows // seg
        grid = (n_seq // seq_per_step, 1)
    row_map = lambda b, t: (b * nt + t)
    tok_spec = pl.BlockSpec((rows, C), lambda b, t: (row_map(b, t), 0))
    const2 = lambda b, t: (0, 0)
    s_spec = pl.BlockSpec((seq_per_step, GDN_HEADS, dk, dk), lambda b, t: (b, 0, 0, 0))
    conv_spec = pl.BlockSpec((seq_per_step, SUBLANES, 3 * C), lambda b, t: (b, 0, 0))
    in_specs = [pl.BlockSpec((rows, C), lambda b, t, c=c: (row_map(b, t), c))
                for c in (col_q, col_q + 1, col_q + 2, col_z)]
    in_specs += [pl.BlockSpec((rows, LANES), lambda b, t: (row_map(b, t), 0)),
                 conv_spec, pl.BlockSpec((CONV_W, 3 * C), const2),
                 pl.BlockSpec((1, LANES), const2), pl.BlockSpec((1, LANES), const2),
                 pl.BlockSpec((1, dk), const2), s_spec]
    scratch = [pltpu.VMEM((rows, C), F32)] * 3
    if carry:
        scratch.append(pltpu.VMEM((3, rows // SUBLANES + 1, SUBLANES, C), F32))
    return pl.pallas_call(
        functools.partial(_gdn_kernel, tile=tile, seg=seg, n_tiles=tiles_per_step, carry=carry),
        grid=grid, in_specs=in_specs,
        out_specs=[tok_spec, s_spec, conv_spec],
        out_shape=[jax.ShapeDtypeStruct((tokens, C), _lhs_dtype(tile)), jax.ShapeDtypeStruct(s0.shape, F32),
                   jax.ShapeDtypeStruct((n_seq, SUBLANES, 3 * C), F32)],
        scratch_shapes=scratch,
        compiler_params=_params("parallel", "arbitrary"),
        name="gdn_branch",
    )(proj, proj, proj, proj, small, xprev, cw, alog_row, dtb_row, ng.reshape(1, dk), s0)


def _mem_attn_kernel(q_ref, k_ref, v_ref, o_ref, *, n_seq, tq):
    hd = q_ref.shape[1] // MEM_HEADS
    scale = hd ** -0.5
    rows = lambda s: slice(s * tq, (s + 1) * tq)
    cols = lambda h: slice(h * hd, (h + 1) * hd)
    if len(k_ref.shape) == 4:
        mem_rows = k_ref.shape[1] * MEM_HEADS
        r_head = lax.broadcasted_iota(jnp.int32, (MEM_HEADS * tq, mem_rows), 0) // tq
        c_head = lax.broadcasted_iota(jnp.int32, (MEM_HEADS * tq, mem_rows), 1) & (MEM_HEADS - 1)
        own = r_head == c_head
        scores = []
        for s in range(n_seq):
            q_all = jnp.concatenate([q_ref[rows(s), cols(h)] for h in range(MEM_HEADS)], axis=0)
            sc = _dot_nt(q_all, k_ref[s].reshape(mem_rows, hd)) * scale
            scores.append(jnp.where(own, sc, -jnp.inf))
        probs = []
        for sc in scores:
            e = jnp.exp(sc - jnp.max(sc, axis=-1, keepdims=True))
            probs.append(e / jnp.sum(e, axis=-1, keepdims=True))
        for s, p in enumerate(probs):
            o_all = _dot(p, v_ref[s].reshape(mem_rows, hd))
            for h in range(MEM_HEADS):
                o_ref[rows(s), cols(h)] = o_all[h * tq:(h + 1) * tq].astype(o_ref.dtype)
        return
    pairs = [(s, h) for s in range(n_seq) for h in range(MEM_HEADS)]
    mem = lambda ref, s, h: ref[s, :, cols(h)]
    scores = [_dot_nt(q_ref[rows(s), cols(h)], mem(k_ref, s, h)) * scale for s, h in pairs]
    probs = []
    for sc in scores:
        e = jnp.exp(sc - jnp.max(sc, axis=-1, keepdims=True))
        probs.append(e / jnp.sum(e, axis=-1, keepdims=True))
    for (s, h), p in zip(pairs, probs):
        o_ref[rows(s), cols(h)] = _dot(p, mem(v_ref, s, h)).astype(o_ref.dtype)


def mem_attention(proj, col_q, mem_k, col_k, mem_v, col_v, *, width, n_seq, seq_len, tq, seq_per_step):
    tokens = n_seq * seq_len
    mem_len = mem_k.shape[1]
    C = width
    nt = seq_len // tq
    rows = seq_per_step * tq
    grid = (n_seq // seq_per_step, nt)
    row_map = lambda b, t: (b * nt + t)
    if mem_k.ndim == 4:
        blk = (seq_per_step,) + mem_k.shape[1:]
        k_spec = v_spec = pl.BlockSpec(blk, lambda b, t: (b, 0, 0, 0))
    else:
        k_spec = pl.BlockSpec((seq_per_step, mem_len, C), lambda b, t: (b, 0, col_k))
        v_spec = pl.BlockSpec((seq_per_step, mem_len, C), lambda b, t: (b, 0, col_v))
    return pl.pallas_call(
        functools.partial(_mem_attn_kernel, n_seq=seq_per_step, tq=tq),
        grid=grid,
        in_specs=[pl.BlockSpec((rows, C), lambda b, t: (row_map(b, t), col_q)), k_spec, v_spec],
        out_specs=pl.BlockSpec((rows, C), lambda b, t: (row_map(b, t), 0)),
        out_shape=jax.ShapeDtypeStruct((tokens, C), _lhs_dtype(tq)),
        compiler_params=_params("parallel", "arbitrary"),
        name="mem_attention",
    )(proj, mem_k, mem_v)


def _merge_kernel(yr_ref, yg_ref, ym_ref, g0_ref, g1_ref, g2_ref, w0_ref, w1_ref, w2_ref, o_ref):
    acc = _sigmoid(g0_ref[...]) * _dot(yr_ref[...], w0_ref[...])
    acc = acc + _sigmoid(g1_ref[...]) * _dot(yg_ref[...], w1_ref[...])
    acc = acc + _sigmoid(g2_ref[...]) * _dot(ym_ref[...], w2_ref[...])
    o_ref[...] = acc.astype(BF16)


def merge_branches(y_rnn, y_gdn, y_mem, proj, col_gate, w_rnn_up, w_gdn_up, w_mem_up, *, tm, tn):
    M, C = y_rnn.shape
    N = w_rnn_up.shape[1]
    nj = N // tn
    y_spec = pl.BlockSpec((tm, C), lambda i, j: (i, 0), pipeline_mode=pl.Buffered(1))
    w_spec = pl.BlockSpec((C, tn), lambda i, j: (0, j))
    gate_specs = [pl.BlockSpec((tm, tn), lambda i, j, b=b: (i, col_gate + b * nj + j)) for b in range(N_BRANCH)]
    return pl.pallas_call(
        _merge_kernel,
        grid=(M // tm, nj),
        in_specs=[y_spec, y_spec, y_spec] + gate_specs + [w_spec, w_spec, w_spec],
        out_specs=pl.BlockSpec((tm, tn), lambda i, j: (i, j)),
        out_shape=jax.ShapeDtypeStruct((M, N), BF16),
        compiler_params=_params("parallel", "arbitrary"),
        name="merge_branches",
    )(y_rnn, y_gdn, y_mem, proj, proj, proj, w_rnn_up, w_gdn_up, w_mem_up)


def _matmul_residual_kernel(a_ref, w_ref, x_ref, o_ref):
    o_ref[...] = x_ref[...] + _dot(a_ref[...], w_ref[...])


def matmul_residual(a, w, x, *, tm, tn):
    M, K = a.shape
    N = w.shape[1]
    return pl.pallas_call(
        _matmul_residual_kernel,
        grid=(M // tm, N // tn),
        in_specs=[pl.BlockSpec((tm, K), lambda i, j: (i, 0), pipeline_mode=pl.Buffered(1)),
                  pl.BlockSpec((K, tn), lambda i, j: (0, j)),
                  pl.BlockSpec((tm, tn), lambda i, j: (i, j))],
        out_specs=pl.BlockSpec((tm, tn), lambda i, j: (i, j)),
        out_shape=jax.ShapeDtypeStruct((M, N), F32),
        compiler_params=_params("parallel", "arbitrary"),
        name="matmul_residual",
    )(a, w, x)


def _mlp_kernel(x_ref, g_ref, wu_ref, wd_ref, gf_ref, o_ref, xn_ref):
    j = pl.program_id(1)

    @pl.when(j == 0)
    def _():
        xn_ref[...] = _rmsnorm(x_ref[...], g_ref[...]).astype(BF16)
        o_ref[...] = jnp.zeros_like(o_ref)

    hid = _dot(xn_ref[...], wu_ref[...])
    act = jnp.square(jnp.maximum(hid, 0.0))
    o_ref[...] += _dot(act, wd_ref[...])

    @pl.when(j == pl.num_programs(1) - 1)
    def _():
        o_ref[...] = _rmsnorm(x_ref[...] + o_ref[...], gf_ref[...])


def mlp_final_norm(x, g, w_up, w_down, g_final, *, tm, tf):
    M, D = x.shape
    FF = w_up.shape[1]
    return pl.pallas_call(
        _mlp_kernel,
        grid=(M // tm, FF // tf),
        in_specs=[pl.BlockSpec((tm, D), lambda i, j: (i, 0), pipeline_mode=pl.Buffered(1)),
                  pl.BlockSpec((1, D), lambda i, j: (0, 0)),
                  pl.BlockSpec((D, tf), lambda i, j: (0, j)),
                  pl.BlockSpec((tf, D), lambda i, j: (j, 0)),
                  pl.BlockSpec((1, D), lambda i, j: (0, 0))],
        out_specs=pl.BlockSpec((tm, D), lambda i, j: (i, 0)),
        out_shape=jax.ShapeDtypeStruct((M, D), F32),
        scratch_shapes=[pltpu.VMEM((tm, D), BF16)],
        compiler_params=_params("parallel", "arbitrary"),
        name="mlp_final_norm",
    )(x, g.reshape(1, D), w_up, w_down, g_final.reshape(1, D))


def _in_proj_segments(d_model):
    half = d_model // 2
    small_row = 6 * half
    mq_row = small_row + 2 * GDN_HEADS
    mg_row = mq_row + half
    return ((mg_row, N_BRANCH * d_model), (0, 6 * half), (mq_row, half)), small_row


def _prep_layer_weights(rnn_wx, rnn_wa, gdn_A_log, gdn_dt_bias):
    w_gate = jnp.concatenate([rnn_wx, rnn_wa], axis=-1).astype(BF16)
    lane_pad = (GDN_HEADS, LANES - 2 * GDN_HEADS)
    alog_row = jnp.pad(gdn_A_log, lane_pad).reshape(1, LANES)
    dtb_row = jnp.pad(gdn_dt_bias, lane_pad).reshape(1, LANES)
    return w_gate, alog_row, dtb_row


def _pad_conv_state(buf):
    return jnp.pad(buf, ((0, 0), (SUBLANES - (CONV_W - 1), 0), (0, 0)))


def _group_layer(x, mem_k, col_k, mem_v, col_v, rnn_buf, rnn_h0, gdn_buf, gdn_s0, lw, *, n_seq, seq_len,
                 tm, proj_tm, row_tile, attn_tq, attn_seqs, gdn_tiles, final_g):
    D = x.shape[1]
    half = D // 2
    n_gate_blk = N_BRANCH * D // half
    c_rx, c_rg, c_gq, c_gz, c_mq = (n_gate_blk + i for i in (0, 1, 2, 5, 6))
    segments, small_row = _in_proj_segments(D)
    proj, small = in_projection(x, lw["norm_mix_g"], lw["w_in_t"], segments, small_row, tm=proj_tm, tn=512)

    y_rnn, rnn_last, h_last = rglru_branch(
        proj, c_rx, c_rg, _pad_conv_state(rnn_buf), rnn_h0.reshape(n_seq, 1, half),
        lw["rnn_conv_w"], lw["rnn_conv_b"], lw["w_gate"], lw["rnn_bx"], lw["rnn_ba"], lw["rnn_L"],
        n_seq=n_seq, seq_len=seq_len, rows=row_tile)

    y_gdn, s_new, gdn_last = gdn_branch(
        proj, c_gq, c_gz, small, _pad_conv_state(gdn_buf), lw["gdn_conv_w"], lw["alog_row"], lw["dtb_row"],
        lw["gdn_norm_g"], gdn_s0, n_seq=n_seq, seq_len=seq_len, tiles_per_step=gdn_tiles)

    y_mem = mem_attention(proj, c_mq, mem_k, col_k, mem_v, col_v, width=half, n_seq=n_seq, seq_len=seq_len,
                          tq=attn_tq, seq_per_step=attn_seqs)

    merged = merge_branches(y_rnn, y_gdn, y_mem, proj, 0, lw["w_rnn_up"], lw["w_gdn_up"], lw["w_mem_up"],
                            tm=proj_tm, tn=256)
    x1 = matmul_residual(merged, lw["w_out"], x, tm=proj_tm, tn=512)
    x2 = mlp_final_norm(x1, lw["norm_mlp_g"], lw["w_mlp_up"], lw["w_mlp_down"], final_g, tm=tm, tf=512)
    states = (rnn_last[:, SUBLANES - (CONV_W - 1):], h_last.reshape(n_seq, half),
              gdn_last[:, SUBLANES - (CONV_W - 1):], s_new)
    return x2, states


def kernel(x_prompt, x_sample, mem_prompt, cache_mem_k, cache_mem_v, state_rnn_conv, state_rnn_h,
           state_gdn_conv, state_gdn_S, norm_mix_g, w_in, rnn_conv_w, rnn_conv_b, rnn_wx, rnn_bx, rnn_wa,
           rnn_ba, rnn_L, gdn_conv_w, gdn_A_log, gdn_dt_bias, gdn_norm_g, mem_norm_g, w_mem_kv, w_rnn_up,
           w_gdn_up, w_mem_up, w_out, norm_mlp_g, w_mlp_up, w_mlp_down, norm_final_g):
    depth = w_in.shape[0]
    assert depth == 1, "the final norm is fused into the last layer's MLP kernel; one layer supported"
    Bp, T, D = x_prompt.shape
    Bs, Ts, _ = x_sample.shape
    half = D // 2
    mem_len = mem_prompt.shape[1]
    assert Ts == SUBLANES and T % GDN_CHUNK == 0
    l = 0
    w_gate, alog_row, dtb_row = _prep_layer_weights(rnn_wx[l], rnn_wa[l], gdn_A_log[l], gdn_dt_bias[l])
    lw = dict(norm_mix_g=norm_mix_g[l], w_in_t=jnp.swapaxes(w_in[l], 0, 1), w_gate=w_gate,
              rnn_conv_w=rnn_conv_w[l], rnn_conv_b=rnn_conv_b[l], rnn_bx=rnn_bx[l], rnn_ba=rnn_ba[l],
              rnn_L=rnn_L[l], gdn_conv_w=gdn_conv_w[l], alog_row=alog_row, dtb_row=dtb_row,
              gdn_norm_g=gdn_norm_g[l], w_rnn_up=w_rnn_up[l], w_gdn_up=w_gdn_up[l],
              w_mem_up=w_mem_up[l], w_out=w_out[l], norm_mlp_g=norm_mlp_g[l],
              w_mlp_up=w_mlp_up[l], w_mlp_down=w_mlp_down[l])

    kv = norm_matmul(mem_prompt.reshape(Bp * mem_len, D), mem_norm_g[l], w_mem_kv[l],
                     tm=min(Bp * mem_len, 1024), tn=512)
    kv3 = kv.reshape(Bp, mem_len, 2 * half)
    zeros = lambda *s: jnp.zeros(s, F32)
    yp, (rb_p, rh_p, gb_p, gs_p) = _group_layer(
        x_prompt.reshape(Bp * T, D), kv3, 0, kv3, 1,
        zeros(Bp, CONV_W - 1, half), zeros(Bp, half), zeros(Bp, CONV_W - 1, 3 * half),
        zeros(Bp, GDN_HEADS, half // GDN_HEADS, half // GDN_HEADS), lw,
        n_seq=Bp, seq_len=T, tm=min(Bp * T, 1024), proj_tm=min(Bp * T, 2048), row_tile=256, attn_tq=512, attn_seqs=1, gdn_tiles=2,
        final_g=norm_final_g)
    mk_p = kv3[:, :, :half].reshape(1, Bp, mem_len, MEM_HEADS, half // MEM_HEADS)
    mv_p = kv3[:, :, half:].reshape(1, Bp, mem_len, MEM_HEADS, half // MEM_HEADS)

    ys, (rb_s, rh_s, gb_s, gs_s) = _group_layer(
        x_sample.reshape(Bs * Ts, D), cache_mem_k.reshape((depth * Bs,) + cache_mem_k.shape[2:]), 0,
        cache_mem_v.reshape((depth * Bs,) + cache_mem_v.shape[2:]), 0,
        state_rnn_conv[l], state_rnn_h[l], state_gdn_conv[l], state_gdn_S[l], lw,
        n_seq=Bs, seq_len=Ts, tm=min(Bs * Ts, 1024), proj_tm=min(Bs * Ts, 1024), row_tile=min(Bs, 16) * SUBLANES, attn_tq=Ts,
        attn_seqs=min(Bs, 4), gdn_tiles=1, final_g=norm_final_g)

    return (yp.reshape(Bp, T, D), ys.reshape(Bs, Ts, D), mk_p, mv_p, rb_p[None], rh_p[None], gb_p[None],
            gs_p[None], rb_s[None], rh_s[None], gb_s[None], gs_s[None])
```

```python
import functools

import jax
import jax.numpy as jnp
from jax import lax
from jax.experimental import pallas as pl
from jax.experimental.pallas import tpu as pltpu

F32 = jnp.float32
BF16 = jnp.bfloat16

EPS = 1e-6
RG_C = 8.0
CONV_W = 4
RNN_BLOCKS = 8
GDN_HEADS = 8
GDN_CHUNK = 64
MEM_HEADS = 4
N_BRANCH = 3

SUBLANES = 8
BF16_SUBLANES = 16
LANES = 128
VMEM_LIMIT_BYTES = 56 * 1024 * 1024


def _lhs_dtype(slab_rows):
    return BF16 if slab_rows % BF16_SUBLANES == 0 else F32


def _params(*sem):
    return pltpu.CompilerParams(dimension_semantics=sem, vmem_limit_bytes=VMEM_LIMIT_BYTES)


def _dot(a, b):
    return jnp.dot(a.astype(BF16), b.astype(BF16), preferred_element_type=F32)


def _dot_nt(a, b):
    return lax.dot_general(a.astype(BF16), b.astype(BF16), (((1,), (1,)), ((), ())),
                           preferred_element_type=F32)


def _dot_tn(a, b):
    return lax.dot_general(a.astype(BF16), b.astype(BF16), (((0,), (0,)), ((), ())),
                           preferred_element_type=F32)


def _rmsnorm(x, g):
    return (x * lax.rsqrt(jnp.mean(x * x, axis=-1, keepdims=True) + EPS)) * g


def _softplus(x):
    return jnp.maximum(x, 0.0) + jnp.log1p(jnp.exp(-jnp.abs(x)))


_sigmoid = jax.nn.sigmoid


def _silu(x):
    return x * _sigmoid(x)


def _norm_matmul_kernel(*refs, has_small):
    if has_small:
        x_ref, g_ref, w_ref, ws_ref, o_ref, os_ref, xn_ref = refs
    else:
        x_ref, g_ref, w_ref, o_ref, xn_ref = refs

    @pl.when(pl.program_id(1) == 0)
    def _():
        xn_ref[...] = _rmsnorm(x_ref[...], g_ref[...]).astype(BF16)
        if has_small:
            os_ref[...] = jnp.dot(xn_ref[...], ws_ref[...], preferred_element_type=F32)

    o_ref[...] = _dot(xn_ref[...], w_ref[...])


def norm_matmul(x, g, w, w_small=None, *, tm, tn):
    M, K = x.shape
    N = w.shape[1]
    has_small = w_small is not None
    in_specs = [pl.BlockSpec((tm, K), lambda i, j: (i, 0)),
                pl.BlockSpec((1, K), lambda i, j: (0, 0)),
                pl.BlockSpec((K, tn), lambda i, j: (0, j))]
    out_shape = [jax.ShapeDtypeStruct((M, N), F32)]
    out_specs = [pl.BlockSpec((tm, tn), lambda i, j: (i, j))]
    args = [x, g.reshape(1, K), w]
    if has_small:
        ns = w_small.shape[1]
        in_specs.append(pl.BlockSpec((K, ns), lambda i, j: (0, 0)))
        out_shape.append(jax.ShapeDtypeStruct((M, ns), F32))
        out_specs.append(pl.BlockSpec((tm, ns), lambda i, j: (i, 0)))
        args.append(w_small)
    outs = pl.pallas_call(
        functools.partial(_norm_matmul_kernel, has_small=has_small),
        grid=(M // tm, N // tn),
        in_specs=in_specs, out_specs=out_specs, out_shape=out_shape,
        scratch_shapes=[pltpu.VMEM((tm, K), BF16)],
        compiler_params=_params("parallel", "arbitrary"),
        name="norm_matmul",
    )(*args)
    return outs if has_small else outs[0]


def _in_proj_kernel(x_ref, g_ref, wt_ref, wst_ref, o_ref, os_ref, xn_ref):
    @pl.when(pl.program_id(1) == 0)
    def _():
        xn_ref[...] = _rmsnorm(x_ref[...], g_ref[...]).astype(BF16)
        os_ref[...] = _dot_nt(xn_ref[...], wst_ref[...])

    o_ref[...] = _dot_nt(xn_ref[...], wt_ref[...])


def in_projection(x, g, w_t, segments, small_row, *, tm, tn):
    M, K = x.shape
    n_tiles = [n // tn for _, n in segments]
    assert all(n % tn == 0 and r % SUBLANES == 0 for r, n in segments) and small_row % LANES == 0
    N = tn * sum(n_tiles)

    def w_row(j):
        row, first = None, 0
        for (r0, _), nt in zip(segments, n_tiles):
            cand = r0 // SUBLANES + (tn // SUBLANES) * (j - first)
            row = cand if row is None else jnp.where(j >= first, cand, row)
            first += nt
        return row * SUBLANES

    outs = pl.pallas_call(
        _in_proj_kernel,
        grid=(M // tm, N // tn),
        in_specs=[pl.BlockSpec((tm, K), lambda i, j: (i, 0), pipeline_mode=pl.Buffered(1)),
                  pl.BlockSpec((1, K), lambda i, j: (0, 0)),
                  pl.BlockSpec((pl.Element(tn), pl.Element(K)), lambda i, j: (w_row(j), 0)),
                  pl.BlockSpec((LANES, K), lambda i, j: (small_row // LANES, 0))],
        out_specs=[pl.BlockSpec((tm, tn), lambda i, j: (i, j)), pl.BlockSpec((tm, LANES), lambda i, j: (i, 0))],
        out_shape=[jax.ShapeDtypeStruct((M, N), F32), jax.ShapeDtypeStruct((M, LANES), F32)],
        scratch_shapes=[pltpu.VMEM((tm, K), BF16)],
        compiler_params=_params("parallel", "arbitrary"),
        name="in_projection",
    )(x, g.reshape(1, K), w_t, w_t)
    return outs


def _causal_conv(x, xprev, w_ref):
    t_idx = lax.broadcasted_iota(jnp.int32, x.shape, 1)
    y = x * w_ref[CONV_W - 1:CONV_W, :][None]
    for k in range(1, CONV_W):
        shifted = jnp.where(t_idx >= k, pltpu.roll(x, k, 1), pltpu.roll(xprev, k, 1))
        y = y + shifted * w_ref[CONV_W - 1 - k:CONV_W - k, :][None]
    return y


def _segment_scan(a, u):
    t_idx = lax.broadcasted_iota(jnp.int32, a.shape, 1)
    s = 1
    while s < SUBLANES:
        keep = t_idx >= s
        u = jnp.where(keep, a * pltpu.roll(u, s, 1) + u, u)
        a = jnp.where(keep, a * pltpu.roll(a, s, 1), a)
        s *= 2
    return a, u


def _stage_groups(x, xprev_ref, xs_ref, first):
    G = x.shape[0]

    @pl.when(first)
    def _():
        xs_ref[G] = xprev_ref[0]

    xs_ref[0] = xs_ref[G]
    xs_ref[1:G + 1] = x
    return xs_ref[0:G]


def _rglru_kernel(rx_ref, rg_ref, xprev_ref, h0_ref, cw_ref, cb_ref, wg_ref, bx_ref, ba_ref, l_ref,
                  y_ref, xlast_ref, hlast_ref, *scratch, carry):
    R, C = rx_ref.shape
    G = R // SUBLANES
    x = rx_ref[...].reshape(G, SUBLANES, C)
    if carry:
        xs_ref, a_ref, u_ref, h_ref, hc_ref = scratch
        first = pl.program_id(1) == 0
        xprev = _stage_groups(x, xprev_ref, xs_ref, first)

        @pl.when(first)
        def _():
            hc_ref[...] = h0_ref[0]
    else:
        xprev = xprev_ref[...]

    xc = (_causal_conv(x, xprev, cw_ref) + cb_ref[...][None]).reshape(R, C)
    xb = xc.astype(BF16)
    bs = C // RNN_BLOCKS
    zi, zr = [], []
    for n in range(RNN_BLOCKS):
        z = jnp.dot(xb[:, n * bs:(n + 1) * bs], wg_ref[n], preferred_element_type=F32)
        zi.append(z[:, :bs])
        zr.append(z[:, bs:])
    gi = _sigmoid(jnp.concatenate(zi, axis=1) + bx_ref[...])
    gr = _sigmoid(jnp.concatenate(zr, axis=1) + ba_ref[...])
    lv = l_ref[...]
    log_sig_l = -_softplus(-lv)
    log_a = RG_C * gr * log_sig_l
    a = jnp.exp(log_a)
    u = jnp.sqrt(-jnp.tanh(log_a) * (a * a + 1.0)) * (gi * xc)
    a_cum, h_loc = _segment_scan(a.reshape(G, SUBLANES, C), u.reshape(G, SUBLANES, C))

    if carry:
        a_ref[...] = a_cum
        u_ref[...] = h_loc

        def body(g, h_prev):
            hg = u_ref[g] + a_ref[g] * h_prev
            h_ref[g] = hg
            return hg[SUBLANES - 1:SUBLANES, :]

        h_last = lax.fori_loop(0, G, body, hc_ref[...])
        hc_ref[...] = h_last
        h = h_ref[...]
        xlast_ref[0] = x[G - 1]
        hlast_ref[0] = h_last
    else:
        h = h_loc + a_cum * h0_ref[...]
        xlast_ref[...] = x
        hlast_ref[...] = h[:, SUBLANES - 1:SUBLANES, :]

    y = h.reshape(R, C) * jax.nn.gelu(rg_ref[...])
    y_ref[...] = y.astype(BF16)


def rglru_branch(proj, col_rx, col_rg, xprev, h0, cw, cb, wg, bx, ba, lam, *, n_seq, seq_len, rows):
    C = cw.shape[1]
    carry = seq_len > SUBLANES
    if carry:
        nt = seq_len // rows
        grid = (n_seq, nt)
        row_map = lambda b, t: (b * nt + t)
        nb = 1
        G = rows // SUBLANES
        scratch = [pltpu.VMEM((G + 1, SUBLANES, C), F32), pltpu.VMEM((G, SUBLANES, C), F32),
                   pltpu.VMEM((G, SUBLANES, C), F32), pltpu.VMEM((G, SUBLANES, C), F32),
                   pltpu.VMEM((1, C), F32)]
    else:
        nb = rows // SUBLANES
        grid = (n_seq // nb, 1)
        row_map = lambda b, t: b
        scratch = []
    const2 = lambda b, t: (0, 0)
    in_specs = [pl.BlockSpec((rows, C), lambda b, t: (row_map(b, t), col_rx)),
                pl.BlockSpec((rows, C), lambda b, t: (row_map(b, t), col_rg)),
                pl.BlockSpec((nb, SUBLANES, C), lambda b, t: (b, 0, 0)),
                pl.BlockSpec((nb, 1, C), lambda b, t: (b, 0, 0)),
                pl.BlockSpec((CONV_W, C), const2),
                pl.BlockSpec((1, C), const2),
                pl.BlockSpec(wg.shape, lambda b, t: (0, 0, 0)),
                pl.BlockSpec((1, C), const2), pl.BlockSpec((1, C), const2), pl.BlockSpec((1, C), const2)]
    tokens = n_seq * seq_len
    out_shape = [jax.ShapeDtypeStruct((tokens, C), BF16),
                 jax.ShapeDtypeStruct((n_seq, SUBLANES, C), F32),
                 jax.ShapeDtypeStruct((n_seq, 1, C), F32)]
    out_specs = [pl.BlockSpec((rows, C), lambda b, t: (row_map(b, t), 0)),
                 pl.BlockSpec((nb, SUBLANES, C), lambda b, t: (b, 0, 0)),
                 pl.BlockSpec((nb, 1, C), lambda b, t: (b, 0, 0))]
    return pl.pallas_call(
        functools.partial(_rglru_kernel, carry=carry),
        grid=grid, in_specs=in_specs, out_specs=out_specs, out_shape=out_shape,
        scratch_shapes=scratch,
        compiler_params=_params("parallel", "arbitrary"),
        name="rglru",
    )(proj, proj, xprev, h0, cw, cb.reshape(1, C), wg, bx.reshape(1, C), ba.reshape(1, C),
      lam.reshape(1, C))


def _l2norm_heads(x, scale):
    dh = x.shape[1] // GDN_HEADS
    outs = []
    for h in range(GDN_HEADS):
        xh = x[:, h * dh:(h + 1) * dh]
        xh = xh * lax.rsqrt(jnp.sum(xh * xh, axis=-1, keepdims=True) + EPS)
        outs.append(xh * scale if scale != 1.0 else xh)
    return jnp.concatenate(outs, axis=1)


def _gdn_conv_norm(src_refs, xprev_ref, cw_ref, dst_refs, xlast_ref, xs_ref, carry):
    R, C = src_refs[0].shape
    G = R // SUBLANES
    dk = C // GDN_HEADS
    for s, (src, dst) in enumerate(zip(src_refs, dst_refs)):
        x = src[...].reshape(G, SUBLANES, C)
        cols = slice(s * C, (s + 1) * C)
        if carry:
            xprev = _stage_groups(x, xprev_ref.at[:, :, cols], xs_ref.at[s], pl.program_id(1) == 0)
            xlast_ref[0, :, cols] = x[G - 1]
        else:
            xprev = xprev_ref[:, :, cols]
            xlast_ref[:, :, cols] = x
        y = _silu(_causal_conv(x, xprev, cw_ref.at[:, cols])).reshape(R, C)
        if s == 0:
            y = _l2norm_heads(y, dk ** -0.5)
        elif s == 1:
            y = _l2norm_heads(y, 1.0)
        dst[...] = y


def _segment_cumsum(x, seg):
    pos = lax.broadcasted_iota(jnp.int32, x.shape, 0) & (seg - 1)
    s = 1
    while s < seg:
        x = x + jnp.where(pos >= s, pltpu.roll(x, s, 0), 0.0)
        s *= 2
    return x


def _segment_last(x, seg):
    n = x.shape[0]
    pos = lax.broadcasted_iota(jnp.int32, x.shape, 0) & (seg - 1)
    s = seg // 2
    while s >= 1:
        x = jnp.where((pos & (2 * s - 1)) < s, pltpu.roll(x, n - s, 0), x)
        s //= 2
    return x


class _TileMasks:
    def __init__(self, rows, seg):
        r = lax.broadcasted_iota(jnp.int32, (rows, rows), 0)
        c = lax.broadcasted_iota(jnp.int32, (rows, rows), 1)
        shift = seg.bit_length() - 1
        same = (r >> shift) == (c >> shift)
        self.incl = same & (r >= c)
        self.strict = same & (r > c)
        self.eye = r == c
        self.levels = []
        s = 1
        while s < seg:
            b = s.bit_length() - 1
            self.levels.append(((r >> (b + 1)) == (c >> (b + 1))) & (((r >> b) & 1) == 1) & (((c >> b) & 1) == 0))
            s *= 2


def _unit_lower_inverses(ms, masks):
    eye = jnp.where(masks.eye, 1.0, 0.0)
    xs = [eye - jnp.where(masks.levels[0], m, 0.0) for m in ms]
    for level in masks.levels[1:]:
        ts = [_dot(jnp.where(level, m, 0.0), x) for m, x in zip(ms, xs)]
        xs = [x - _dot(x, t) for x, t in zip(xs, ts)]
    return xs


def _gdn_tiles_local(chains, masks):
    r, d = chains[0][0].shape
    kbs, decays, kk_qks = [], [], []
    for q, k, v, beta, gc, g_last in chains:
        gc_row = jnp.sum(jnp.where(masks.eye, gc, 0.0), axis=0, keepdims=True)
        decays.append(jnp.where(masks.incl, jnp.exp(gc - gc_row), 0.0))
        kbs.append(k * beta)
        kk_qks.append(_dot_nt(jnp.concatenate([kbs[-1], q], axis=0), k))
    ms = [jnp.where(masks.strict, kq[:r] * dec, 0.0) for kq, dec in zip(kk_qks, decays)]
    tms = _unit_lower_inverses(ms, masks)
    out = []
    for (q, k, v, beta, gc, g_last), kb, dec, kq, tm in zip(chains, kbs, decays, kk_qks, tms):
        eg = jnp.exp(gc)
        vk = _dot(tm, jnp.concatenate([v * beta, kb * eg], axis=1))
        out.append(dict(value=vk[:, :d], kcd=vk[:, d:], attn=kq[r:] * dec, qg=q * eg,
                        kd=k * jnp.exp(g_last - gc), decay_last=jnp.exp(g_last)))
    return out


def _gdn_kernel(gq_ref, gk_ref, gv_ref, z_ref, sm_ref, xprev_ref, cw_ref, alog_ref, dtb_ref, ng_ref, s0_ref,
                o_ref, s_ref, xlast_ref, q_ref, k_ref, v_ref, *scratch, tile, seg, n_tiles, carry):
    dk = q_ref.shape[1] // GDN_HEADS
    n_seg = tile // seg
    masks = _TileMasks(tile, seg)
    neg_a = -jnp.exp(alog_ref[...])
    dtb = dtb_ref[...]
    ng = ng_ref[...]
    if carry:
        @pl.when(pl.program_id(1) == 0)
        def _():
            s_ref[...] = s0_ref[...]
    _gdn_conv_norm((gq_ref, gk_ref, gv_ref), xprev_ref, cw_ref, (q_ref, k_ref, v_ref), xlast_ref,
                   scratch[0] if carry else None, carry)

    chains = []
    for i in range(n_tiles):
        rows = slice(i * tile, (i + 1) * tile)
        sm = sm_ref[rows, :]
        beta_all = _sigmoid(sm)
        gc_all = _segment_cumsum(neg_a * _softplus(sm + dtb), seg)
        gl_all = _segment_last(gc_all, seg)
        for h in range(GDN_HEADS):
            cols = slice(h * dk, (h + 1) * dk)
            lane = slice(GDN_HEADS + h, GDN_HEADS + h + 1)
            chains.append((q_ref[rows, cols], k_ref[rows, cols], v_ref[rows, cols],
                           beta_all[:, h:h + 1], gc_all[:, lane], gl_all[:, lane]))
    local = _gdn_tiles_local(chains, masks)

    heads = range(GDN_HEADS)
    for i in range(n_tiles):
        rows = slice(i * tile, (i + 1) * tile)
        loc = local[i * GDN_HEADS:(i + 1) * GDN_HEADS]
        v_new = [[] for _ in heads]
        o_state = [[] for _ in heads]
        for j in range(n_seg):
            sl = slice(j * seg, (j + 1) * seg)
            seq = 0 if carry else i * n_seg + j
            s_prev = [s_ref[seq, h] if carry else s0_ref[seq, h] for h in heads]
            rs = [_dot(jnp.concatenate([loc[h]["kcd"][sl], loc[h]["qg"][sl]], axis=0), s_prev[h])
                  for h in heads]
            vns = [loc[h]["value"][sl] - rs[h][:seg] for h in heads]
            for h in heads:
                v_new[h].append(vns[h])
                o_state[h].append(rs[h][seg:])
                s_ref[seq, h] = (s_prev[h] * loc[h]["decay_last"][j * seg:j * seg + 1, :]
                                 + _dot_tn(loc[h]["kd"][sl], vns[h]))
        for h in heads:
            cols = slice(h * dk, (h + 1) * dk)
            vn = jnp.concatenate(v_new[h], axis=0) if n_seg > 1 else v_new[h][0]
            os_ = jnp.concatenate(o_state[h], axis=0) if n_seg > 1 else o_state[h][0]
            o = os_ + _dot(loc[h]["attn"], vn)
            o_ref[rows, cols] = (_rmsnorm(o, ng) * _silu(z_ref[rows, cols])).astype(o_ref.dtype)


def gdn_branch(proj, col_q, col_z, small, xprev, cw, alog_row, dtb_row, ng, s0, *, n_seq, seq_len,
               tiles_per_step):
    tokens = proj.shape[0]
    C = cw.shape[1] // 3
    dk = C // GDN_HEADS
    tile = GDN_CHUNK
    carry = seq_len >= GDN_CHUNK
    seg = GDN_CHUNK if carry else seq_len
    rows = tiles_per_step * tile
    if carry:
        nt = seq_len // rows
        seq_per_step = 1
        grid = (n_seq, nt)
    else:
        nt = 1
        seq_per_step = rows // seg
        grid = (n_seq // seq_per_step, 1)
    row_map = lambda b, t: (b * nt + t)
    tok_spec = pl.BlockSpec((rows, C), lambda b, t: (row_map(b, t), 0))
    const2 = lambda b, t: (0, 0)
    s_spec = pl.BlockSpec((seq_per_step, GDN_HEADS, dk, dk), lambda b, t: (b, 0, 0, 0))
    conv_spec = pl.BlockSpec((seq_per_step, SUBLANES, 3 * C), lambda b, t: (b, 0, 0))
    in_specs = [pl.BlockSpec((rows, C), lambda b, t, c=c: (row_map(b, t), c))
                for c in (col_q, col_q + 1, col_q + 2, col_z)]
    in_specs += [pl.BlockSpec((rows, LANES), lambda b, t: (row_map(b, t), 0)),
                 conv_spec, pl.BlockSpec((CONV_W, 3 * C), const2),
                 pl.BlockSpec((1, LANES), const2), pl.BlockSpec((1, LANES), const2),
                 pl.BlockSpec((1, dk), const2), s_spec]
    scratch = [pltpu.VMEM((rows, C), F32)] * 3
    if carry:
        scratch.append(pltpu.VMEM((3, rows // SUBLANES + 1, SUBLANES, C), F32))
    return pl.pallas_call(
        functools.partial(_gdn_kernel, tile=tile, seg=seg, n_tiles=tiles_per_step, carry=carry),
        grid=grid, in_specs=in_specs,
        out_specs=[tok_spec, s_spec, conv_spec],
        out_shape=[jax.ShapeDtypeStruct((tokens, C), _lhs_dtype(tile)), jax.ShapeDtypeStruct(s0.shape, F32),
                   jax.ShapeDtypeStruct((n_seq, SUBLANES, 3 * C), F32)],
        scratch_shapes=scratch,
        compiler_params=_params("parallel", "arbitrary"),
        name="gdn_branch",
    )(proj, proj, proj, proj, small, xprev, cw, alog_row, dtb_row, ng.reshape(1, dk), s0)


def _mem_attn_kernel(q_ref, k_ref, v_ref, o_ref, *, n_seq, tq):
    hd = q_ref.shape[1] // MEM_HEADS
    scale = hd ** -0.5
    rows = lambda s: slice(s * tq, (s + 1) * tq)
    cols = lambda h: slice(h * hd, (h + 1) * hd)
    if len(k_ref.shape) == 4:
        mem_rows = k_ref.shape[1] * MEM_HEADS
        r_head = lax.broadcasted_iota(jnp.int32, (MEM_HEADS * tq, mem_rows), 0) // tq
        c_head = lax.broadcasted_iota(jnp.int32, (MEM_HEADS * tq, mem_rows), 1) & (MEM_HEADS - 1)
        own = r_head == c_head
        scores = []
        for s in range(n_seq):
            q_all = jnp.concatenate([q_ref[rows(s), cols(h)] for h in range(MEM_HEADS)], axis=0)
            sc = _dot_nt(q_all, k_ref[s].reshape(mem_rows, hd)) * scale
            scores.append(jnp.where(own, sc, -jnp.inf))
        probs = []
        for sc in scores:
            e = jnp.exp(sc - jnp.max(sc, axis=-1, keepdims=True))
            probs.append(e / jnp.sum(e, axis=-1, keepdims=True))
        for s, p in enumerate(probs):
            o_all = _dot(p, v_ref[s].reshape(mem_rows, hd))
            for h in range(MEM_HEADS):
                o_ref[rows(s), cols(h)] = o_all[h * tq:(h + 1) * tq].astype(o_ref.dtype)
        return
    pairs = [(s, h) for s in range(n_seq) for h in range(MEM_HEADS)]
    mem = lambda ref, s, h: ref[s, :, cols(h)]
    scores = [_dot_nt(q_ref[rows(s), cols(h)], mem(k_ref, s, h)) * scale for s, h in pairs]
    probs = []
    for sc in scores:
        e = jnp.exp(sc - jnp.max(sc, axis=-1, keepdims=True))
        probs.append(e / jnp.sum(e, axis=-1, keepdims=True))
    for (s, h), p in zip(pairs, probs):
        o_ref[rows(s), cols(h)] = _dot(p, mem(v_ref, s, h)).astype(o_ref.dtype)


def mem_attention(proj, col_q, mem_k, col_k, mem_v, col_v, *, width, n_seq, seq_len, tq, seq_per_step):
    tokens = n_seq * seq_len
    mem_len = mem_k.shape[1]
    C = width
    nt = seq_len // tq
    rows = seq_per_step * tq
    grid = (n_seq // seq_per_step, nt)
    row_map = lambda b, t: (b * nt + t)
    if mem_k.ndim == 4:
        blk = (seq_per_step,) + mem_k.shape[1:]
        k_spec = v_spec = pl.BlockSpec(blk, lambda b, t: (b, 0, 0, 0))
    else:
        k_spec = pl.BlockSpec((seq_per_step, mem_len, C), lambda b, t: (b, 0, col_k))
        v_spec = pl.BlockSpec((seq_per_step, mem_len, C), lambda b, t: (b, 0, col_v))
    return pl.pallas_call(
        functools.partial(_mem_attn_kernel, n_seq=seq_per_step, tq=tq),
        grid=grid,
        in_specs=[pl.BlockSpec((rows, C), lambda b, t: (row_map(b, t), col_q)), k_spec, v_spec],
        out_specs=pl.BlockSpec((rows, C), lambda b, t: (row_map(b, t), 0)),
        out_shape=jax.ShapeDtypeStruct((tokens, C), _lhs_dtype(tq)),
        compiler_params=_params("parallel", "arbitrary"),
        name="mem_attention",
    )(proj, mem_k, mem_v)


def _merge_kernel(yr_ref, yg_ref, ym_ref, g0_ref, g1_ref, g2_ref, w0_ref, w1_ref, w2_ref, o_ref):
    acc = _sigmoid(g0_ref[...]) * _dot(yr_ref[...], w0_ref[...])
    acc = acc + _sigmoid(g1_ref[...]) * _dot(yg_ref[...], w1_ref[...])
    acc = acc + _sigmoid(g2_ref[...]) * _dot(ym_ref[...], w2_ref[...])
    o_ref[...] = acc.astype(BF16)


def merge_branches(y_rnn, y_gdn, y_mem, proj, col_gate, w_rnn_up, w_gdn_up, w_mem_up, *, tm, tn):
    M, C = y_rnn.shape
    N = w_rnn_up.shape[1]
    nj = N // tn
    y_spec = pl.BlockSpec((tm, C), lambda i, j: (i, 0), pipeline_mode=pl.Buffered(1))
    w_spec = pl.BlockSpec((C, tn), lambda i, j: (0, j))
    gate_specs = [pl.BlockSpec((tm, tn), lambda i, j, b=b: (i, col_gate + b * nj + j)) for b in range(N_BRANCH)]
    return pl.pallas_call(
        _merge_kernel,
        grid=(M // tm, nj),
        in_specs=[y_spec, y_spec, y_spec] + gate_specs + [w_spec, w_spec, w_spec],
        out_specs=pl.BlockSpec((tm, tn), lambda i, j: (i, j)),
        out_shape=jax.ShapeDtypeStruct((M, N), BF16),
        compiler_params=_params("parallel", "arbitrary"),
        name="merge_branches",
    )(y_rnn, y_gdn, y_mem, proj, proj, proj, w_rnn_up, w_gdn_up, w_mem_up)


def _matmul_residual_kernel(a_ref, w_ref, x_ref, o_ref):
    o_ref[...] = x_ref[...] + _dot(a_ref[...], w_ref[...])


def matmul_residual(a, w, x, *, tm, tn):
    M, K = a.shape
    N = w.shape[1]
    return pl.pallas_call(
        _matmul_residual_kernel,
        grid=(M // tm, N // tn),
        in_specs=[pl.BlockSpec((tm, K), lambda i, j: (i, 0), pipeline_mode=pl.Buffered(1)),
                  pl.BlockSpec((K, tn), lambda i, j: (0, j)),
                  pl.BlockSpec((tm, tn), lambda i, j: (i, j))],
        out_specs=pl.BlockSpec((tm, tn), lambda i, j: (i, j)),
        out_shape=jax.ShapeDtypeStruct((M, N), F32),
        compiler_params=_params("parallel", "arbitrary"),
        name="matmul_residual",
    )(a, w, x)


def _mlp_kernel(x_ref, g_ref, wu_ref, wd_ref, gf_ref, o_ref, xn_ref):
    j = pl.program_id(1)

    @pl.when(j == 0)
    def _():
        xn_ref[...] = _rmsnorm(x_ref[...], g_ref[...]).astype(BF16)
        o_ref[...] = jnp.zeros_like(o_ref)

    hid = _dot(xn_ref[...], wu_ref[...])
    act = jnp.square(jnp.maximum(hid, 0.0))
    o_ref[...] += _dot(act, wd_ref[...])

    @pl.when(j == pl.num_programs(1) - 1)
    def _():
        o_ref[...] = _rmsnorm(x_ref[...] + o_ref[...], gf_ref[...])


def mlp_final_norm(x, g, w_up, w_down, g_final, *, tm, tf):
    M, D = x.shape
    FF = w_up.shape[1]
    return pl.pallas_call(
        _mlp_kernel,
        grid=(M // tm, FF // tf),
        in_specs=[pl.BlockSpec((tm, D), lambda i, j: (i, 0), pipeline_mode=pl.Buffered(1)),
                  pl.BlockSpec((1, D), lambda i, j: (0, 0)),
                  pl.BlockSpec((D, tf), lambda i, j: (0, j)),
                  pl.BlockSpec((tf, D), lambda i, j: (j, 0)),
                  pl.BlockSpec((1, D), lambda i, j: (0, 0))],
        out_specs=pl.BlockSpec((tm, D), lambda i, j: (i, 0)),
        out_shape=jax.ShapeDtypeStruct((M, D), F32),
        scratch_shapes=[pltpu.VMEM((tm, D), BF16)],
        compiler_params=_params("parallel", "arbitrary"),
        name="mlp_final_norm",
    )(x, g.reshape(1, D), w_up, w_down, g_final.reshape(1, D))


def _in_proj_segments(d_model):
    half = d_model // 2
    small_row = 6 * half
    mq_row = small_row + 2 * GDN_HEADS
    mg_row = mq_row + half
    return ((mg_row, N_BRANCH * d_model), (0, 6 * half), (mq_row, half)), small_row


def _prep_layer_weights(rnn_wx, rnn_wa, gdn_A_log, gdn_dt_bias):
    w_gate = jnp.concatenate([rnn_wx, rnn_wa], axis=-1).astype(BF16)
    lane_pad = (GDN_HEADS, LANES - 2 * GDN_HEADS)
    alog_row = jnp.pad(gdn_A_log, lane_pad).reshape(1, LANES)
    dtb_row = jnp.pad(gdn_dt_bias, lane_pad).reshape(1, LANES)
    return w_gate, alog_row, dtb_row


def _pad_conv_state(buf):
    return jnp.pad(buf, ((0, 0), (SUBLANES - (CONV_W - 1), 0), (0, 0)))


def _group_layer(x, mem_k, col_k, mem_v, col_v, rnn_buf, rnn_h0, gdn_buf, gdn_s0, lw, *, n_seq, seq_len,
                 tm, proj_tm, row_tile, attn_tq, attn_seqs, gdn_tiles, final_g):
    D = x.shape[1]
    half = D // 2
    n_gate_blk = N_BRANCH * D // half
    c_rx, c_rg, c_gq, c_gz, c_mq = (n_gate_blk + i for i in (0, 1, 2, 5, 6))
    segments, small_row = _in_proj_segments(D)
    proj, small = in_projection(x, lw["norm_mix_g"], lw["w_in_t"], segments, small_row, tm=proj_tm, tn=512)

    y_rnn, rnn_last, h_last = rglru_branch(
        proj, c_rx, c_rg, _pad_conv_state(rnn_buf), rnn_h0.reshape(n_seq, 1, half),
        lw["rnn_conv_w"], lw["rnn_conv_b"], lw["w_gate"], lw["rnn_bx"], lw["rnn_ba"], lw["rnn_L"],
        n_seq=n_seq, seq_len=seq_len, rows=row_tile)

    y_gdn, s_new, gdn_last = gdn_branch(
        proj, c_gq, c_gz, small, _pad_conv_state(gdn_buf), lw["gdn_conv_w"], lw["alog_row"], lw["dtb_row"],
        lw["gdn_norm_g"], gdn_s0, n_seq=n_seq, seq_len=seq_len, tiles_per_step=gdn_tiles)

    y_mem = mem_attention(proj, c_mq, mem_k, col_k, mem_v, col_v, width=half, n_seq=n_seq, seq_len=seq_len,
                          tq=attn_tq, seq_per_step=attn_seqs)

    merged = merge_branches(y_rnn, y_gdn, y_mem, proj, 0, lw["w_rnn_up"], lw["w_gdn_up"], lw["w_mem_up"],
                            tm=proj_tm, tn=256)
    x1 = matmul_residual(merged, lw["w_out"], x, tm=proj_tm, tn=512)
    x2 = mlp_final_norm(x1, lw["norm_mlp_g"], lw["w_mlp_up"], lw["w_mlp_down"], final_g, tm=tm, tf=512)
    states = (rnn_last[:, SUBLANES - (CONV_W - 1):], h_last.reshape(n_seq, half),
              gdn_last[:, SUBLANES - (CONV_W - 1):], s_new)
    return x2, states


def kernel(x_prompt, x_sample, mem_prompt, cache_mem_k, cache_mem_v, state_rnn_conv, state_rnn_h,
           state_gdn_conv, state_gdn_S, norm_mix_g, w_in, rnn_conv_w, rnn_conv_b, rnn_wx, rnn_bx, rnn_wa,
           rnn_ba, rnn_L, gdn_conv_w, gdn_A_log, gdn_dt_bias, gdn_norm_g, mem_norm_g, w_mem_kv, w_rnn_up,
           w_gdn_up, w_mem_up, w_out, norm_mlp_g, w_mlp_up, w_mlp_down, norm_final_g):
    depth = w_in.shape[0]
    assert depth == 1, "the final norm is fused into the last layer's MLP kernel; one layer supported"
    Bp, T, D = x_prompt.shape
    Bs, Ts, _ = x_sample.shape
    half = D // 2
    mem_len = mem_prompt.shape[1]
    assert Ts == SUBLANES and T % GDN_CHUNK == 0
    l = 0
    w_gate, alog_row, dtb_row = _prep_layer_weights(rnn_wx[l], rnn_wa[l], gdn_A_log[l], gdn_dt_bias[l])
    lw = dict(norm_mix_g=norm_mix_g[l], w_in_t=jnp.swapaxes(w_in[l], 0, 1), w_gate=w_gate,
              rnn_conv_w=rnn_conv_w[l], rnn_conv_b=rnn_conv_b[l], rnn_bx=rnn_bx[l], rnn_ba=rnn_ba[l],
              rnn_L=rnn_L[l], gdn_conv_w=gdn_conv_w[l], alog_row=alog_row, dtb_row=dtb_row,
              gdn_norm_g=gdn_norm_g[l], w_rnn_up=w_rnn_up[l], w_gdn_up=w_gdn_up[l],
              w_mem_up=w_mem_up[l], w_out=w_out[l], norm_mlp_g=norm_mlp_g[l],
              w_mlp_up=w_mlp_up[l], w_mlp_down=w_mlp_down[l])

    kv = norm_matmul(mem_prompt.reshape(Bp * mem_len, D), mem_norm_g[l], w_mem_kv[l],
                     tm=min(Bp * mem_len, 1024), tn=512)
    kv3 = kv.reshape(Bp, mem_len, 2 * half)
    zeros = lambda *s: jnp.zeros(s, F32)
    yp, (rb_p, rh_p, gb_p, gs_p) = _group_layer(
        x_prompt.reshape(Bp * T, D), kv3, 0, kv3, 1,
        zeros(Bp, CONV_W - 1, half), zeros(Bp, half), zeros(Bp, CONV_W - 1, 3 * half),
        zeros(Bp, GDN_HEADS, half // GDN_HEADS, half // GDN_HEADS), lw,
        n_seq=Bp, seq_len=T, tm=min(Bp * T, 1024), proj_tm=min(Bp * T, 2048), row_tile=512, attn_tq=512, attn_seqs=1, gdn_tiles=2,
        final_g=norm_final_g)
    mk_p = kv3[:, :, :half].reshape(1, Bp, mem_len, MEM_HEADS, half // MEM_HEADS)
    mv_p = kv3[:, :, half:].reshape(1, Bp, mem_len, MEM_HEADS, half // MEM_HEADS)

    ys, (rb_s, rh_s, gb_s, gs_s) = _group_layer(
        x_sample.reshape(Bs * Ts, D), cache_mem_k.reshape((depth * Bs,) + cache_mem_k.shape[2:]), 0,
        cache_mem_v.reshape((depth * Bs,) + cache_mem_v.shape[2:]), 0,
        state_rnn_conv[l], state_rnn_h[l], state_gdn_conv[l], state_gdn_S[l], lw,
        n_seq=Bs, seq_len=Ts, tm=min(Bs * Ts, 1024), proj_tm=min(Bs * Ts, 1024), row_tile=min(Bs, 16) * SUBLANES, attn_tq=Ts,
        attn_seqs=min(Bs, 8), gdn_tiles=2 if Bs >= 16 else 1, final_g=norm_final_g)

    return (yp.reshape(Bp, T, D), ys.reshape(Bs, Ts, D), mk_p, mv_p, rb_p[None], rh_p[None], gb_p[None],
            gs_p[None], rb_s[None], rh_s[None], gb_s[None], gs_s[None])
```

```python
import functools

import jax
import jax.numpy as jnp
from jax import lax
from jax.experimental import pallas as pl
from jax.experimental.pallas import tpu as pltpu

F32 = jnp.float32
BF16 = jnp.bfloat16

EPS = 1e-6
RG_C = 8.0
CONV_W = 4
RNN_BLOCKS = 8
GDN_HEADS = 8
GDN_CHUNK = 64
MEM_HEADS = 4
N_BRANCH = 3

SUBLANES = 8
BF16_SUBLANES = 16
LANES = 128
VMEM_LIMIT_BYTES = 56 * 1024 * 1024


def _lhs_dtype(slab_rows):
    return BF16 if slab_rows % BF16_SUBLANES == 0 else F32


def _params(*sem):
    return pltpu.CompilerParams(dimension_semantics=sem, vmem_limit_bytes=VMEM_LIMIT_BYTES)


def _dot(a, b):
    return jnp.dot(a.astype(BF16), b.astype(BF16), preferred_element_type=F32)


def _dot_nt(a, b):
    return lax.dot_general(a.astype(BF16), b.astype(BF16), (((1,), (1,)), ((), ())),
                           preferred_element_type=F32)


def _dot_tn(a, b):
    return lax.dot_general(a.astype(BF16), b.astype(BF16), (((0,), (0,)), ((), ())),
                           preferred_element_type=F32)


def _rmsnorm(x, g):
    return (x * lax.rsqrt(jnp.mean(x * x, axis=-1, keepdims=True) + EPS)) * g


def _softplus(x):
    return jnp.maximum(x, 0.0) + jnp.log1p(jnp.exp(-jnp.abs(x)))


_sigmoid = jax.nn.sigmoid


def _silu(x):
    return x * _sigmoid(x)


def _norm_matmul_kernel(*refs, has_small):
    if has_small:
        x_ref, g_ref, w_ref, ws_ref, o_ref, os_ref, xn_ref = refs
    else:
        x_ref, g_ref, w_ref, o_ref, xn_ref = refs

    @pl.when(pl.program_id(1) == 0)
    def _():
        xn_ref[...] = _rmsnorm(x_ref[...], g_ref[...]).astype(BF16)
        if has_small:
            os_ref[...] = jnp.dot(xn_ref[...], ws_ref[...], preferred_element_type=F32)

    o_ref[...] = _dot(xn_ref[...], w_ref[...])


def norm_matmul(x, g, w, w_small=None, *, tm, tn):
    M, K = x.shape
    N = w.shape[1]
    has_small = w_small is not None
    in_specs = [pl.BlockSpec((tm, K), lambda i, j: (i, 0)),
                pl.BlockSpec((1, K), lambda i, j: (0, 0)),
                pl.BlockSpec((K, tn), lambda i, j: (0, j))]
    out_shape = [jax.ShapeDtypeStruct((M, N), F32)]
    out_specs = [pl.BlockSpec((tm, tn), lambda i, j: (i, j))]
    args = [x, g.reshape(1, K), w]
    if has_small:
        ns = w_small.shape[1]
        in_specs.append(pl.BlockSpec((K, ns), lambda i, j: (0, 0)))
        out_shape.append(jax.ShapeDtypeStruct((M, ns), F32))
        out_specs.append(pl.BlockSpec((tm, ns), lambda i, j: (i, 0)))
        args.append(w_small)
    outs = pl.pallas_call(
        functools.partial(_norm_matmul_kernel, has_small=has_small),
        grid=(M // tm, N // tn),
        in_specs=in_specs, out_specs=out_specs, out_shape=out_shape,
        scratch_shapes=[pltpu.VMEM((tm, K), BF16)],
        compiler_params=_params("parallel", "arbitrary"),
        name="norm_matmul",
    )(*args)
    return outs if has_small else outs[0]


def _in_proj_kernel(x_ref, g_ref, wt_ref, wst_ref, o_ref, os_ref, xn_ref):
    @pl.when(pl.program_id(1) == 0)
    def _():
        xn_ref[...] = _rmsnorm(x_ref[...], g_ref[...]).astype(BF16)
        os_ref[...] = _dot_nt(xn_ref[...], wst_ref[...])

    o_ref[...] = _dot_nt(xn_ref[...], wt_ref[...])


def in_projection(x, g, w_t, segments, small_row, *, tm, tn):
    M, K = x.shape
    n_tiles = [n // tn for _, n in segments]
    assert all(n % tn == 0 and r % SUBLANES == 0 for r, n in segments) and small_row % LANES == 0
    N = tn * sum(n_tiles)

    def w_row(j):
        row, first = None, 0
        for (r0, _), nt in zip(segments, n_tiles):
            cand = r0 // SUBLANES + (tn // SUBLANES) * (j - first)
            row = cand if row is None else jnp.where(j >= first, cand, row)
            first += nt
        return row * SUBLANES

    outs = pl.pallas_call(
        _in_proj_kernel,
        grid=(M // tm, N // tn),
        in_specs=[pl.BlockSpec((tm, K), lambda i, j: (i, 0), pipeline_mode=pl.Buffered(1)),
                  pl.BlockSpec((1, K), lambda i, j: (0, 0)),
                  pl.BlockSpec((pl.Element(tn), pl.Element(K)), lambda i, j: (w_row(j), 0)),
                  pl.BlockSpec((LANES, K), lambda i, j: (small_row // LANES, 0))],
        out_specs=[pl.BlockSpec((tm, tn), lambda i, j: (i, j)), pl.BlockSpec((tm, LANES), lambda i, j: (i, 0))],
        out_shape=[jax.ShapeDtypeStruct((M, N), F32), jax.ShapeDtypeStruct((M, LANES), F32)],
        scratch_shapes=[pltpu.VMEM((tm, K), BF16)],
        compiler_params=_params("parallel", "arbitrary"),
        name="in_projection",
    )(x, g.reshape(1, K), w_t, w_t)
    return outs


def _causal_conv(x, xprev, w_ref):
    t_idx = lax.broadcasted_iota(jnp.int32, x.shape, 1)
    y = x * w_ref[CONV_W - 1:CONV_W, :][None]
    for k in range(1, CONV_W):
        shifted = jnp.where(t_idx >= k, pltpu.roll(x, k, 1), pltpu.roll(xprev, k, 1))
        y = y + shifted * w_ref[CONV_W - 1 - k:CONV_W - k, :][None]
    return y


def _segment_scan(a, u):
    t_idx = lax.broadcasted_iota(jnp.int32, a.shape, 1)
    s = 1
    while s < SUBLANES:
        keep = t_idx >= s
        u = jnp.where(keep, a * pltpu.roll(u, s, 1) + u, u)
        a = jnp.where(keep, a * pltpu.roll(a, s, 1), a)
        s *= 2
    return a, u


def _stage_groups(x, xprev_ref, xs_ref, first):
    G = x.shape[0]

    @pl.when(first)
    def _():
        xs_ref[G] = xprev_ref[0]

    xs_ref[0] = xs_ref[G]
    xs_ref[1:G + 1] = x
    return xs_ref[0:G]


def _rglru_kernel(rx_ref, rg_ref, xprev_ref, h0_ref, cw_ref, cb_ref, wg_ref, bx_ref, ba_ref, l_ref,
                  y_ref, xlast_ref, hlast_ref, *scratch, carry):
    R, C = rx_ref.shape
    G = R // SUBLANES
    x = rx_ref[...].reshape(G, SUBLANES, C)
    if carry:
        xs_ref, a_ref, u_ref, h_ref, hc_ref = scratch
        first = pl.program_id(1) == 0
        xprev = _stage_groups(x, xprev_ref, xs_ref, first)

        @pl.when(first)
        def _():
            hc_ref[...] = h0_ref[0]
    else:
        xprev = xprev_ref[...]

    xc = (_causal_conv(x, xprev, cw_ref) + cb_ref[...][None]).reshape(R, C)
    xb = xc.astype(BF16)
    bs = C // RNN_BLOCKS
    zi, zr = [], []
    for n in range(RNN_BLOCKS):
        z = jnp.dot(xb[:, n * bs:(n + 1) * bs], wg_ref[n], preferred_element_type=F32)
        zi.append(z[:, :bs])
        zr.append(z[:, bs:])
    gi = _sigmoid(jnp.concatenate(zi, axis=1) + bx_ref[...])
    gr = _sigmoid(jnp.concatenate(zr, axis=1) + ba_ref[...])
    lv = l_ref[...]
    log_sig_l = -_softplus(-lv)
    log_a = RG_C * gr * log_sig_l
    a = jnp.exp(log_a)
    u = jnp.sqrt(-jnp.tanh(log_a) * (a * a + 1.0)) * (gi * xc)
    a_cum, h_loc = _segment_scan(a.reshape(G, SUBLANES, C), u.reshape(G, SUBLANES, C))

    if carry:
        a_ref[...] = a_cum
        u_ref[...] = h_loc

        def body(g, h_prev):
            hg = u_ref[g] + a_ref[g] * h_prev
            h_ref[g] = hg
            return hg[SUBLANES - 1:SUBLANES, :]

        h_last = lax.fori_loop(0, G, body, hc_ref[...])
        hc_ref[...] = h_last
        h = h_ref[...]
        xlast_ref[0] = x[G - 1]
        hlast_ref[0] = h_last
    else:
        h = h_loc + a_cum * h0_ref[...]
        xlast_ref[...] = x
        hlast_ref[...] = h[:, SUBLANES - 1:SUBLANES, :]

    y = h.reshape(R, C) * jax.nn.gelu(rg_ref[...])
    y_ref[...] = y.astype(BF16)


def rglru_branch(proj, col_rx, col_rg, xprev, h0, cw, cb, wg, bx, ba, lam, *, n_seq, seq_len, rows):
    C = cw.shape[1]
    carry = seq_len > SUBLANES
    if carry:
        nt = seq_len // rows
        grid = (n_seq, nt)
        row_map = lambda b, t: (b * nt + t)
        nb = 1
        G = rows // SUBLANES
        scratch = [pltpu.VMEM((G + 1, SUBLANES, C), F32), pltpu.VMEM((G, SUBLANES, C), F32),
                   pltpu.VMEM((G, SUBLANES, C), F32), pltpu.VMEM((G, SUBLANES, C), F32),
                   pltpu.VMEM((1, C), F32)]
    else:
        nb = rows // SUBLANES
        grid = (n_seq // nb, 1)
        row_map = lambda b, t: b
        scratch = []
    const2 = lambda b, t: (0, 0)
    in_specs = [pl.BlockSpec((rows, C), lambda b, t: (row_map(b, t), col_rx)),
                pl.BlockSpec((rows, C), lambda b, t: (row_map(b, t), col_rg)),
                pl.BlockSpec((nb, SUBLANES, C), lambda b, t: (b, 0, 0)),
                pl.BlockSpec((nb, 1, C), lambda b, t: (b, 0, 0)),
                pl.BlockSpec((CONV_W, C), const2),
                pl.BlockSpec((1, C), const2),
                pl.BlockSpec(wg.shape, lambda b, t: (0, 0, 0)),
                pl.BlockSpec((1, C), const2), pl.BlockSpec((1, C), const2), pl.BlockSpec((1, C), const2)]
    tokens = n_seq * seq_len
    out_shape = [jax.ShapeDtypeStruct((tokens, C), BF16),
                 jax.ShapeDtypeStruct((n_seq, SUBLANES, C), F32),
                 jax.ShapeDtypeStruct((n_seq, 1, C), F32)]
    out_specs = [pl.BlockSpec((rows, C), lambda b, t: (row_map(b, t), 0)),
                 pl.BlockSpec((nb, SUBLANES, C), lambda b, t: (b, 0, 0)),
                 pl.BlockSpec((nb, 1, C), lambda b, t: (b, 0, 0))]
    return pl.pallas_call(
        functools.partial(_rglru_kernel, carry=carry),
        grid=grid, in_specs=in_specs, out_specs=out_specs, out_shape=out_shape,
        scratch_shapes=scratch,
        compiler_params=_params("parallel", "arbitrary"),
        name="rglru",
    )(proj, proj, xprev, h0, cw, cb.reshape(1, C), wg, bx.reshape(1, C), ba.reshape(1, C),
      lam.reshape(1, C))


def _l2norm_heads(x, scale):
    dh = x.shape[1] // GDN_HEADS
    outs = []
    for h in range(GDN_HEADS):
        xh = x[:, h * dh:(h + 1) * dh]
        xh = xh * lax.rsqrt(jnp.sum(xh * xh, axis=-1, keepdims=True) + EPS)
        outs.append(xh * scale if scale != 1.0 else xh)
    return jnp.concatenate(outs, axis=1)


def _gdn_conv_norm(src_refs, xprev_ref, cw_ref, dst_refs, xlast_ref, xs_ref, carry):
    R, C = src_refs[0].shape
    G = R // SUBLANES
    dk = C // GDN_HEADS
    for s, (src, dst) in enumerate(zip(src_refs, dst_refs)):
        x = src[...].reshape(G, SUBLANES, C)
        cols = slice(s * C, (s + 1) * C)
        if carry:
            xprev = _stage_groups(x, xprev_ref.at[:, :, cols], xs_ref.at[s], pl.program_id(1) == 0)
            xlast_ref[0, :, cols] = x[G - 1]
        else:
            xprev = xprev_ref[:, :, cols]
            xlast_ref[:, :, cols] = x
        y = _silu(_causal_conv(x, xprev, cw_ref.at[:, cols])).reshape(R, C)
        if s == 0:
            y = _l2norm_heads(y, dk ** -0.5)
        elif s == 1:
            y = _l2norm_heads(y, 1.0)
        dst[...] = y


def _segment_cumsum(x, seg):
    pos = lax.broadcasted_iota(jnp.int32, x.shape, 0) & (seg - 1)
    s = 1
    while s < seg:
        x = x + jnp.where(pos >= s, pltpu.roll(x, s, 0), 0.0)
        s *= 2
    return x


def _segment_last(x, seg):
    n = x.shape[0]
    pos = lax.broadcasted_iota(jnp.int32, x.shape, 0) & (seg - 1)
    s = seg // 2
    while s >= 1:
        x = jnp.where((pos & (2 * s - 1)) < s, pltpu.roll(x, n - s, 0), x)
        s //= 2
    return x


class _TileMasks:
    def __init__(self, rows, seg):
        r = lax.broadcasted_iota(jnp.int32, (rows, rows), 0)
        c = lax.broadcasted_iota(jnp.int32, (rows, rows), 1)
        shift = seg.bit_length() - 1
        same = (r >> shift) == (c >> shift)
        self.incl = same & (r >= c)
        self.strict = same & (r > c)
        self.eye = r == c
        self.levels = []
        s = 1
        while s < seg:
            b = s.bit_length() - 1
            self.levels.append(((r >> (b + 1)) == (c >> (b + 1))) & (((r >> b) & 1) == 1) & (((c >> b) & 1) == 0))
            s *= 2


def _unit_lower_inverses(ms, masks):
    eye = jnp.where(masks.eye, 1.0, 0.0)
    xs = [eye - jnp.where(masks.levels[0], m, 0.0) for m in ms]
    for level in masks.levels[1:]:
        ts = [_dot(jnp.where(level, m, 0.0), x) for m, x in zip(ms, xs)]
        xs = [x - _dot(x, t) for x, t in zip(xs, ts)]
    return xs


def _gdn_tiles_local(chains, masks):
    r, d = chains[0][0].shape
    kbs, decays, kk_qks = [], [], []
    for q, k, v, beta, gc, g_last in chains:
        gc_row = jnp.sum(jnp.where(masks.eye, gc, 0.0), axis=0, keepdims=True)
        decays.append(jnp.where(masks.incl, jnp.exp(gc - gc_row), 0.0))
        kbs.append(k * beta)
        kk_qks.append(_dot_nt(jnp.concatenate([kbs[-1], q], axis=0), k))
    ms = [jnp.where(masks.strict, kq[:r] * dec, 0.0) for kq, dec in zip(kk_qks, decays)]
    tms = _unit_lower_inverses(ms, masks)
    out = []
    for (q, k, v, beta, gc, g_last), kb, dec, kq, tm in zip(chains, kbs, decays, kk_qks, tms):
        eg = jnp.exp(gc)
        vk = _dot(tm, jnp.concatenate([v * beta, kb * eg], axis=1))
        out.append(dict(value=vk[:, :d], kcd=vk[:, d:], attn=kq[r:] * dec, qg=q * eg,
                        kd=k * jnp.exp(g_last - gc), decay_last=jnp.exp(g_last)))
    return out


def _gdn_kernel(gq_ref, gk_ref, gv_ref, z_ref, sm_ref, xprev_ref, cw_ref, alog_ref, dtb_ref, ng_ref, s0_ref,
                o_ref, s_ref, xlast_ref, q_ref, k_ref, v_ref, *scratch, tile, seg, n_tiles, carry):
    dk = q_ref.shape[1] // GDN_HEADS
    n_seg = tile // seg
    masks = _TileMasks(tile, seg)
    neg_a = -jnp.exp(alog_ref[...])
    dtb = dtb_ref[...]
    ng = ng_ref[...]
    if carry:
        @pl.when(pl.program_id(1) == 0)
        def _():
            s_ref[...] = s0_ref[...]
    _gdn_conv_norm((gq_ref, gk_ref, gv_ref), xprev_ref, cw_ref, (q_ref, k_ref, v_ref), xlast_ref,
                   scratch[0] if carry else None, carry)

    chains = []
    for i in range(n_tiles):
        rows = slice(i * tile, (i + 1) * tile)
        sm = sm_ref[rows, :]
        beta_all = _sigmoid(sm)
        gc_all = _segment_cumsum(neg_a * _softplus(sm + dtb), seg)
        gl_all = _segment_last(gc_all, seg)
        for h in range(GDN_HEADS):
            cols = slice(h * dk, (h + 1) * dk)
            lane = slice(GDN_HEADS + h, GDN_HEADS + h + 1)
            chains.append((q_ref[rows, cols], k_ref[rows, cols], v_ref[rows, cols],
                           beta_all[:, h:h + 1], gc_all[:, lane], gl_all[:, lane]))
    local = _gdn_tiles_local(chains, masks)

    heads = range(GDN_HEADS)
    for i in range(n_tiles):
        rows = slice(i * tile, (i + 1) * tile)
        loc = local[i * GDN_HEADS:(i + 1) * GDN_HEADS]
        v_new = [[] for _ in heads]
        o_state = [[] for _ in heads]
        for j in range(n_seg):
            sl = slice(j * seg, (j + 1) * seg)
            seq = 0 if carry else i * n_seg + j
            s_prev = [s_ref[seq, h] if carry else s0_ref[seq, h] for h in heads]
            rs = [_dot(jnp.concatenate([loc[h]["kcd"][sl], loc[h]["qg"][sl]], axis=0), s_prev[h])
                  for h in heads]
            vns = [loc[h]["value"][sl] - rs[h][:seg] for h in heads]
            for h in heads:
                v_new[h].append(vns[h])
                o_state[h].append(rs[h][seg:])
                s_ref[seq, h] = (s_prev[h] * loc[h]["decay_last"][j * seg:j * seg + 1, :]
                                 + _dot_tn(loc[h]["kd"][sl], vns[h]))
        for h in heads:
            cols = slice(h * dk, (h + 1) * dk)
            vn = jnp.concatenate(v_new[h], axis=0) if n_seg > 1 else v_new[h][0]
            os_ = jnp.concatenate(o_state[h], axis=0) if n_seg > 1 else o_state[h][0]
            o = os_ + _dot(loc[h]["attn"], vn)
            o_ref[rows, cols] = (_rmsnorm(o, ng) * _silu(z_ref[rows, cols])).astype(o_ref.dtype)


def gdn_branch(proj, col_q, col_z, small, xprev, cw, alog_row, dtb_row, ng, s0, *, n_seq, seq_len,
               tiles_per_step):
    tokens = proj.shape[0]
    C = cw.shape[1] // 3
    dk = C // GDN_HEADS
    tile = GDN_CHUNK
    carry = seq_len >= GDN_CHUNK
    seg = GDN_CHUNK if carry else seq_len
    rows = tiles_per_step * tile
    if carry:
        nt = seq_len // rows
        seq_per_step = 1
        grid = (n_seq, nt)
    else:
        nt = 1
        seq_per_step = rows // seg
        grid = (n_seq // seq_per_step, 1)
    row_map = lambda b, t: (b * nt + t)
    tok_spec = pl.BlockSpec((rows, C), lambda b, t: (row_map(b, t), 0))
    const2 = lambda b, t: (0, 0)
    s_spec = pl.BlockSpec((seq_per_step, GDN_HEADS, dk, dk), lambda b, t: (b, 0, 0, 0))
    conv_spec = pl.BlockSpec((seq_per_step, SUBLANES, 3 * C), lambda b, t: (b, 0, 0))
    in_specs = [pl.BlockSpec((rows, C), lambda b, t, c=c: (row_map(b, t), c))
                for c in (col_q, col_q + 1, col_q + 2, col_z)]
    in_specs += [pl.BlockSpec((rows, LANES), lambda b, t: (row_map(b, t), 0)),
                 conv_spec, pl.BlockSpec((CONV_W, 3 * C), const2),
                 pl.BlockSpec((1, LANES), const2), pl.BlockSpec((1, LANES), const2),
                 pl.BlockSpec((1, dk), const2), s_spec]
    scratch = [pltpu.VMEM((rows, C), F32)] * 3
    if carry:
        scratch.append(pltpu.VMEM((3, rows // SUBLANES + 1, SUBLANES, C), F32))
    return pl.pallas_call(
        functools.partial(_gdn_kernel, tile=tile, seg=seg, n_tiles=tiles_per_step, carry=carry),
        grid=grid, in_specs=in_specs,
        out_specs=[tok_spec, s_spec, conv_spec],
        out_shape=[jax.ShapeDtypeStruct((tokens, C), _lhs_dtype(tile)), jax.ShapeDtypeStruct(s0.shape, F32),
                   jax.ShapeDtypeStruct((n_seq, SUBLANES, 3 * C), F32)],
        scratch_shapes=scratch,
        compiler_params=_params("parallel", "arbitrary"),
        name="gdn_branch",
    )(proj, proj, proj, proj, small, xprev, cw, alog_row, dtb_row, ng.reshape(1, dk), s0)


def _mem_attn_kernel(q_ref, k_ref, v_ref, o_ref, *, n_seq, tq):
    hd = q_ref.shape[1] // MEM_HEADS
    scale = hd ** -0.5
    rows = lambda s: slice(s * tq, (s + 1) * tq)
    cols = lambda h: slice(h * hd, (h + 1) * hd)
    if len(k_ref.shape) == 4:
        mem_rows = k_ref.shape[1] * MEM_HEADS
        r_head = lax.broadcasted_iota(jnp.int32, (MEM_HEADS * tq, mem_rows), 0) // tq
        c_head = lax.broadcasted_iota(jnp.int32, (MEM_HEADS * tq, mem_rows), 1) & (MEM_HEADS - 1)
        own = r_head == c_head
        scores = []
        for s in range(n_seq):
            q_all = jnp.concatenate([q_ref[rows(s), cols(h)] for h in range(MEM_HEADS)], axis=0)
            sc = _dot_nt(q_all, k_ref[s].reshape(mem_rows, hd)) * scale
            scores.append(jnp.where(own, sc, -jnp.inf))
        probs = []
        for sc in scores:
            e = jnp.exp(sc - jnp.max(sc, axis=-1, keepdims=True))
            probs.append(e / jnp.sum(e, axis=-1, keepdims=True))
        for s, p in enumerate(probs):
            o_all = _dot(p, v_ref[s].reshape(mem_rows, hd))
            for h in range(MEM_HEADS):
                o_ref[rows(s), cols(h)] = o_all[h * tq:(h + 1) * tq].astype(o_ref.dtype)
        return
    pairs = [(s, h) for s in range(n_seq) for h in range(MEM_HEADS)]
    mem = lambda ref, s, h: ref[s, :, cols(h)]
    scores = [_dot_nt(q_ref[rows(s), cols(h)], mem(k_ref, s, h)) * scale for s, h in pairs]
    probs = []
    for sc in scores:
        e = jnp.exp(sc - jnp.max(sc, axis=-1, keepdims=True))
        probs.append(e / jnp.sum(e, axis=-1, keepdims=True))
    for (s, h), p in zip(pairs, probs):
        o_ref[rows(s), cols(h)] = _dot(p, mem(v_ref, s, h)).astype(o_ref.dtype)


def mem_attention(proj, col_q, mem_k, col_k, mem_v, col_v, *, width, n_seq, seq_len, tq, seq_per_step):
    tokens = n_seq * seq_len
    mem_len = mem_k.shape[1]
    C = width
    nt = seq_len // tq
    rows = seq_per_step * tq
    grid = (n_seq // seq_per_step, nt)
    row_map = lambda b, t: (b * nt + t)
    if mem_k.ndim == 4:
        blk = (seq_per_step,) + mem_k.shape[1:]
        k_spec = v_spec = pl.BlockSpec(blk, lambda b, t: (b, 0, 0, 0))
    else:
        k_spec = pl.BlockSpec((seq_per_step, mem_len, C), lambda b, t: (b, 0, col_k))
        v_spec = pl.BlockSpec((seq_per_step, mem_len, C), lambda b, t: (b, 0, col_v))
    return pl.pallas_call(
        functools.partial(_mem_attn_kernel, n_seq=seq_per_step, tq=tq),
        grid=grid,
        in_specs=[pl.BlockSpec((rows, C), lambda b, t: (row_map(b, t), col_q)), k_spec, v_spec],
        out_specs=pl.BlockSpec((rows, C), lambda b, t: (row_map(b, t), 0)),
        out_shape=jax.ShapeDtypeStruct((tokens, C), _lhs_dtype(tq)),
        compiler_params=_params("parallel", "arbitrary"),
        name="mem_attention",
    )(proj, mem_k, mem_v)


def _merge_kernel(yr_ref, yg_ref, ym_ref, g0_ref, g1_ref, g2_ref, w0_ref, w1_ref, w2_ref, o_ref):
    acc = _sigmoid(g0_ref[...]) * _dot(yr_ref[...], w0_ref[...])
    acc = acc + _sigmoid(g1_ref[...]) * _dot(yg_ref[...], w1_ref[...])
    acc = acc + _sigmoid(g2_ref[...]) * _dot(ym_ref[...], w2_ref[...])
    o_ref[...] = acc.astype(BF16)


def merge_branches(y_rnn, y_gdn, y_mem, proj, col_gate, w_rnn_up, w_gdn_up, w_mem_up, *, tm, tn):
    M, C = y_rnn.shape
    N = w_rnn_up.shape[1]
    nj = N // tn
    y_spec = pl.BlockSpec((tm, C), lambda i, j: (i, 0), pipeline_mode=pl.Buffered(1))
    w_spec = pl.BlockSpec((C, tn), lambda i, j: (0, j))
    gate_specs = [pl.BlockSpec((tm, tn), lambda i, j, b=b: (i, col_gate + b * nj + j)) for b in range(N_BRANCH)]
    return pl.pallas_call(
        _merge_kernel,
        grid=(M // tm, nj),
        in_specs=[y_spec, y_spec, y_spec] + gate_specs + [w_spec, w_spec, w_spec],
        out_specs=pl.BlockSpec((tm, tn), lambda i, j: (i, j)),
        out_shape=jax.ShapeDtypeStruct((M, N), BF16),
        compiler_params=_params("parallel", "arbitrary"),
        name="merge_branches",
    )(y_rnn, y_gdn, y_mem, proj, proj, proj, w_rnn_up, w_gdn_up, w_mem_up)


def _matmul_residual_kernel(a_ref, w_ref, x_ref, o_ref):
    o_ref[...] = x_ref[...] + _dot(a_ref[...], w_ref[...])


def matmul_residual(a, w, x, *, tm, tn):
    M, K = a.shape
    N = w.shape[1]
    return pl.pallas_call(
        _matmul_residual_kernel,
        grid=(M // tm, N // tn),
        in_specs=[pl.BlockSpec((tm, K), lambda i, j: (i, 0), pipeline_mode=pl.Buffered(1)),
                  pl.BlockSpec((K, tn), lambda i, j: (0, j)),
                  pl.BlockSpec((tm, tn), lambda i, j: (i, j))],
        out_specs=pl.BlockSpec((tm, tn), lambda i, j: (i, j)),
        out_shape=jax.ShapeDtypeStruct((M, N), F32),
        compiler_params=_params("parallel", "arbitrary"),
        name="matmul_residual",
    )(a, w, x)


def _mlp_kernel(x_ref, g_ref, wu_ref, wd_ref, gf_ref, o_ref, xn_ref):
    j = pl.program_id(1)

    @pl.when(j == 0)
    def _():
        xn_ref[...] = _rmsnorm(x_ref[...], g_ref[...]).astype(BF16)
        o_ref[...] = jnp.zeros_like(o_ref)

    hid = _dot(xn_ref[...], wu_ref[...])
    act = jnp.square(jnp.maximum(hid, 0.0))
    o_ref[...] += _dot(act, wd_ref[...])

    @pl.when(j == pl.num_programs(1) - 1)
    def _():
        o_ref[...] = _rmsnorm(x_ref[...] + o_ref[...], gf_ref[...])


def mlp_final_norm(x, g, w_up, w_down, g_final, *, tm, tf):
    M, D = x.shape
    FF = w_up.shape[1]
    return pl.pallas_call(
        _mlp_kernel,
        grid=(M // tm, FF // tf),
        in_specs=[pl.BlockSpec((tm, D), lambda i, j: (i, 0), pipeline_mode=pl.Buffered(1)),
                  pl.BlockSpec((1, D), lambda i, j: (0, 0)),
                  pl.BlockSpec((D, tf), lambda i, j: (0, j)),
                  pl.BlockSpec((tf, D), lambda i, j: (j, 0)),
                  pl.BlockSpec((1, D), lambda i, j: (0, 0))],
        out_specs=pl.BlockSpec((tm, D), lambda i, j: (i, 0)),
        out_shape=jax.ShapeDtypeStruct((M, D), F32),
        scratch_shapes=[pltpu.VMEM((tm, D), BF16)],
        compiler_params=_params("parallel", "arbitrary"),
        name="mlp_final_norm",
    )(x, g.reshape(1, D), w_up, w_down, g_final.reshape(1, D))


def _in_proj_segments(d_model):
    half = d_model // 2
    small_row = 6 * half
    mq_row = small_row + 2 * GDN_HEADS
    mg_row = mq_row + half
    return ((mg_row, N_BRANCH * d_model), (0, 6 * half), (mq_row, half)), small_row


def _prep_layer_weights(rnn_wx, rnn_wa, gdn_A_log, gdn_dt_bias):
    w_gate = jnp.concatenate([rnn_wx, rnn_wa], axis=-1).astype(BF16)
    lane_pad = (GDN_HEADS, LANES - 2 * GDN_HEADS)
    alog_row = jnp.pad(gdn_A_log, lane_pad).reshape(1, LANES)
    dtb_row = jnp.pad(gdn_dt_bias, lane_pad).reshape(1, LANES)
    return w_gate, alog_row, dtb_row


def _pad_conv_state(buf):
    return jnp.pad(buf, ((0, 0), (SUBLANES - (CONV_W - 1), 0), (0, 0)))


def _group_layer(x, mem_k, col_k, mem_v, col_v, rnn_buf, rnn_h0, gdn_buf, gdn_s0, lw, *, n_seq, seq_len,
                 tm, proj_tm, row_tile, attn_tq, attn_seqs, gdn_tiles, final_g):
    D = x.shape[1]
    half = D // 2
    n_gate_blk = N_BRANCH * D // half
    c_rx, c_rg, c_gq, c_gz, c_mq = (n_gate_blk + i for i in (0, 1, 2, 5, 6))
    segments, small_row = _in_proj_segments(D)
    proj, small = in_projection(x, lw["norm_mix_g"], lw["w_in_t"], segments, small_row, tm=proj_tm,
                                tn=512 if proj_tm > 1024 else 1024)

    y_rnn, rnn_last, h_last = rglru_branch(
        proj, c_rx, c_rg, _pad_conv_state(rnn_buf), rnn_h0.reshape(n_seq, 1, half),
        lw["rnn_conv_w"], lw["rnn_conv_b"], lw["w_gate"], lw["rnn_bx"], lw["rnn_ba"], lw["rnn_L"],
        n_seq=n_seq, seq_len=seq_len, rows=row_tile)

    y_gdn, s_new, gdn_last = gdn_branch(
        proj, c_gq, c_gz, small, _pad_conv_state(gdn_buf), lw["gdn_conv_w"], lw["alog_row"], lw["dtb_row"],
        lw["gdn_norm_g"], gdn_s0, n_seq=n_seq, seq_len=seq_len, tiles_per_step=gdn_tiles)

    y_mem = mem_attention(proj, c_mq, mem_k, col_k, mem_v, col_v, width=half, n_seq=n_seq, seq_len=seq_len,
                          tq=attn_tq, seq_per_step=attn_seqs)

    merged = merge_branches(y_rnn, y_gdn, y_mem, proj, 0, lw["w_rnn_up"], lw["w_gdn_up"], lw["w_mem_up"],
                            tm=proj_tm, tn=256)
    x1 = matmul_residual(merged, lw["w_out"], x, tm=proj_tm, tn=512)
    x2 = mlp_final_norm(x1, lw["norm_mlp_g"], lw["w_mlp_up"], lw["w_mlp_down"], final_g, tm=tm, tf=512)
    states = (rnn_last[:, SUBLANES - (CONV_W - 1):], h_last.reshape(n_seq, half),
              gdn_last[:, SUBLANES - (CONV_W - 1):], s_new)
    return x2, states


def kernel(x_prompt, x_sample, mem_prompt, cache_mem_k, cache_mem_v, state_rnn_conv, state_rnn_h,
           state_gdn_conv, state_gdn_S, norm_mix_g, w_in, rnn_conv_w, rnn_conv_b, rnn_wx, rnn_bx, rnn_wa,
           rnn_ba, rnn_L, gdn_conv_w, gdn_A_log, gdn_dt_bias, gdn_norm_g, mem_norm_g, w_mem_kv, w_rnn_up,
           w_gdn_up, w_mem_up, w_out, norm_mlp_g, w_mlp_up, w_mlp_down, norm_final_g):
    depth = w_in.shape[0]
    assert depth == 1, "the final norm is fused into the last layer's MLP kernel; one layer supported"
    Bp, T, D = x_prompt.shape
    Bs, Ts, _ = x_sample.shape
    half = D // 2
    mem_len = mem_prompt.shape[1]
    assert Ts == SUBLANES and T % GDN_CHUNK == 0
    l = 0
    w_gate, alog_row, dtb_row = _prep_layer_weights(rnn_wx[l], rnn_wa[l], gdn_A_log[l], gdn_dt_bias[l])
    lw = dict(norm_mix_g=norm_mix_g[l], w_in_t=jnp.swapaxes(w_in[l], 0, 1), w_gate=w_gate,
              rnn_conv_w=rnn_conv_w[l], rnn_conv_b=rnn_conv_b[l], rnn_bx=rnn_bx[l], rnn_ba=rnn_ba[l],
              rnn_L=rnn_L[l], gdn_conv_w=gdn_conv_w[l], alog_row=alog_row, dtb_row=dtb_row,
              gdn_norm_g=gdn_norm_g[l], w_rnn_up=w_rnn_up[l], w_gdn_up=w_gdn_up[l],
              w_mem_up=w_mem_up[l], w_out=w_out[l], norm_mlp_g=norm_mlp_g[l],
              w_mlp_up=w_mlp_up[l], w_mlp_down=w_mlp_down[l])

    kv = norm_matmul(mem_prompt.reshape(Bp * mem_len, D), mem_norm_g[l], w_mem_kv[l],
                     tm=min(Bp * mem_len, 1024), tn=512)
    kv3 = kv.reshape(Bp, mem_len, 2 * half)
    zeros = lambda *s: jnp.zeros(s, F32)
    yp, (rb_p, rh_p, gb_p, gs_p) = _group_layer(
        x_prompt.reshape(Bp * T, D), kv3, 0, kv3, 1,
        zeros(Bp, CONV_W - 1, half), zeros(Bp, half), zeros(Bp, CONV_W - 1, 3 * half),
        zeros(Bp, GDN_HEADS, half // GDN_HEADS, half // GDN_HEADS), lw,
        n_seq=Bp, seq_len=T, tm=min(Bp * T, 1024), proj_tm=min(Bp * T, 2048), row_tile=512, attn_tq=min(T, 1024), attn_seqs=1,
        gdn_tiles=min(4, T // GDN_CHUNK),
        final_g=norm_final_g)
    mk_p = kv3[:, :, :half].reshape(1, Bp, mem_len, MEM_HEADS, half // MEM_HEADS)
    mv_p = kv3[:, :, half:].reshape(1, Bp, mem_len, MEM_HEADS, half // MEM_HEADS)

    ys, (rb_s, rh_s, gb_s, gs_s) = _group_layer(
        x_sample.reshape(Bs * Ts, D), cache_mem_k.reshape((depth * Bs,) + cache_mem_k.shape[2:]), 0,
        cache_mem_v.reshape((depth * Bs,) + cache_mem_v.shape[2:]), 0,
        state_rnn_conv[l], state_rnn_h[l], state_gdn_conv[l], state_gdn_S[l], lw,
        n_seq=Bs, seq_len=Ts, tm=min(Bs * Ts, 1024), proj_tm=min(Bs * Ts, 1024), row_tile=min(Bs, 16) * SUBLANES, attn_tq=Ts,
        attn_seqs=min(Bs, 8), gdn_tiles=max(1, min(2, Bs * Ts // GDN_CHUNK)), final_g=norm_final_g)

    return (yp.reshape(Bp, T, D), ys.reshape(Bs, Ts, D), mk_p, mv_p, rb_p[None], rh_p[None], gb_p[None],
            gs_p[None], rb_s[None], rh_s[None], gb_s[None], gs_s[None])
```

```python
import functools

import jax
import jax.numpy as jnp
from jax import lax
from jax.experimental import pallas as pl
from jax.experimental.pallas import tpu as pltpu

F32 = jnp.float32
BF16 = jnp.bfloat16

EPS = 1e-6
RG_C = 8.0
CONV_W = 4
RNN_BLOCKS = 8
GDN_HEADS = 8
GDN_CHUNK = 64
MEM_HEADS = 4
N_BRANCH = 3

SUBLANES = 8
BF16_SUBLANES = 16
LANES = 128
VMEM_LIMIT_BYTES = 56 * 1024 * 1024


def _lhs_dtype(slab_rows):
    return BF16 if slab_rows % BF16_SUBLANES == 0 else F32


def _params(*sem):
    return pltpu.CompilerParams(dimension_semantics=sem, vmem_limit_bytes=VMEM_LIMIT_BYTES)


def _dot(a, b):
    return jnp.dot(a.astype(BF16), b.astype(BF16), preferred_element_type=F32)


def _dot_nt(a, b):
    return lax.dot_general(a.astype(BF16), b.astype(BF16), (((1,), (1,)), ((), ())),
                           preferred_element_type=F32)


def _dot_tn(a, b):
    return lax.dot_general(a.astype(BF16), b.astype(BF16), (((0,), (0,)), ((), ())),
                           preferred_element_type=F32)


def _rmsnorm(x, g):
    return (x * lax.rsqrt(jnp.mean(x * x, axis=-1, keepdims=True) + EPS)) * g


def _softplus(x):
    return jnp.maximum(x, 0.0) + jnp.log1p(jnp.exp(-jnp.abs(x)))


_sigmoid = jax.nn.sigmoid


def _silu(x):
    return x * _sigmoid(x)


def _norm_matmul_kernel(*refs, has_small):
    if has_small:
        x_ref, g_ref, w_ref, ws_ref, o_ref, os_ref, xn_ref = refs
    else:
        x_ref, g_ref, w_ref, o_ref, xn_ref = refs

    @pl.when(pl.program_id(1) == 0)
    def _():
        xn_ref[...] = _rmsnorm(x_ref[...], g_ref[...]).astype(BF16)
        if has_small:
            os_ref[...] = jnp.dot(xn_ref[...], ws_ref[...], preferred_element_type=F32)

    o_ref[...] = _dot(xn_ref[...], w_ref[...])


def norm_matmul(x, g, w, w_small=None, *, tm, tn):
    M, K = x.shape
    N = w.shape[1]
    has_small = w_small is not None
    in_specs = [pl.BlockSpec((tm, K), lambda i, j: (i, 0)),
                pl.BlockSpec((1, K), lambda i, j: (0, 0)),
                pl.BlockSpec((K, tn), lambda i, j: (0, j))]
    out_shape = [jax.ShapeDtypeStruct((M, N), F32)]
    out_specs = [pl.BlockSpec((tm, tn), lambda i, j: (i, j))]
    args = [x, g.reshape(1, K), w]
    if has_small:
        ns = w_small.shape[1]
        in_specs.append(pl.BlockSpec((K, ns), lambda i, j: (0, 0)))
        out_shape.append(jax.ShapeDtypeStruct((M, ns), F32))
        out_specs.append(pl.BlockSpec((tm, ns), lambda i, j: (i, 0)))
        args.append(w_small)
    outs = pl.pallas_call(
        functools.partial(_norm_matmul_kernel, has_small=has_small),
        grid=(M // tm, N // tn),
        in_specs=in_specs, out_specs=out_specs, out_shape=out_shape,
        scratch_shapes=[pltpu.VMEM((tm, K), BF16)],
        compiler_params=_params("parallel", "arbitrary"),
        name="norm_matmul",
    )(*args)
    return outs if has_small else outs[0]


def _in_proj_kernel(x_ref, g_ref, wt_ref, wst_ref, o_ref, os_ref, xn_ref):
    @pl.when(pl.program_id(1) == 0)
    def _():
        xn_ref[...] = _rmsnorm(x_ref[...], g_ref[...]).astype(BF16)
        os_ref[...] = _dot_nt(xn_ref[...], wst_ref[...])

    o_ref[...] = _dot_nt(xn_ref[...], wt_ref[...])


def in_projection(x, g, w_t, segments, small_row, *, tm, tn):
    M, K = x.shape
    n_tiles = [n // tn for _, n in segments]
    assert all(n % tn == 0 and r % SUBLANES == 0 for r, n in segments) and small_row % LANES == 0
    N = tn * sum(n_tiles)

    def w_row(j):
        row, first = None, 0
        for (r0, _), nt in zip(segments, n_tiles):
            cand = r0 // SUBLANES + (tn // SUBLANES) * (j - first)
            row = cand if row is None else jnp.where(j >= first, cand, row)
            first += nt
        return row * SUBLANES

    outs = pl.pallas_call(
        _in_proj_kernel,
        grid=(M // tm, N // tn),
        in_specs=[pl.BlockSpec((tm, K), lambda i, j: (i, 0), pipeline_mode=pl.Buffered(1)),
                  pl.BlockSpec((1, K), lambda i, j: (0, 0)),
                  pl.BlockSpec((pl.Element(tn), pl.Element(K)), lambda i, j: (w_row(j), 0)),
                  pl.BlockSpec((LANES, K), lambda i, j: (small_row // LANES, 0))],
        out_specs=[pl.BlockSpec((tm, tn), lambda i, j: (i, j)), pl.BlockSpec((tm, LANES), lambda i, j: (i, 0))],
        out_shape=[jax.ShapeDtypeStruct((M, N), F32), jax.ShapeDtypeStruct((M, LANES), F32)],
        scratch_shapes=[pltpu.VMEM((tm, K), BF16)],
        compiler_params=_params("parallel", "arbitrary"),
        name="in_projection",
    )(x, g.reshape(1, K), w_t, w_t)
    return outs


def _causal_conv(x, xprev, w_ref):
    t_idx = lax.broadcasted_iota(jnp.int32, x.shape, 1)
    y = x * w_ref[CONV_W - 1:CONV_W, :][None]
    for k in range(1, CONV_W):
        shifted = jnp.where(t_idx >= k, pltpu.roll(x, k, 1), pltpu.roll(xprev, k, 1))
        y = y + shifted * w_ref[CONV_W - 1 - k:CONV_W - k, :][None]
    return y


def _segment_scan(a, u):
    t_idx = lax.broadcasted_iota(jnp.int32, a.shape, 1)
    s = 1
    while s < SUBLANES:
        keep = t_idx >= s
        u = jnp.where(keep, a * pltpu.roll(u, s, 1) + u, u)
        a = jnp.where(keep, a * pltpu.roll(a, s, 1), a)
        s *= 2
    return a, u


def _state_group(state_ref, cols=slice(None)):
    rows, nb = state_ref.shape[0], state_ref.shape[1]
    first = [state_ref[r, :, cols] for r in range(rows)]
    t_idx = lax.broadcasted_iota(jnp.int32, (nb, SUBLANES, first[0].shape[-1]), 1)
    y = jnp.zeros(t_idx.shape, F32)
    for r, row in enumerate(first):
        y = jnp.where(t_idx == SUBLANES - rows + r, row[:, None, :], y)
    return y


def _store_state(state_ref, x, cols=slice(None)):
    rows = state_ref.shape[0]
    for r in range(rows):
        state_ref[r, :, cols] = x[:, SUBLANES - rows + r, :]


def _conv_state_spec(carry, nb, n_seq, width):
    if carry:
        return pl.BlockSpec((nb, SUBLANES, width), lambda b, t: (b, 0, 0)), (n_seq, SUBLANES, width)
    return pl.BlockSpec((CONV_W - 1, nb, width), lambda b, t: (0, b, 0)), (CONV_W - 1, n_seq, width)


def _stage_groups(x, xprev_ref, xs_ref, first):
    G = x.shape[0]

    @pl.when(first)
    def _():
        xs_ref[G] = xprev_ref[0]

    xs_ref[0] = xs_ref[G]
    xs_ref[1:G + 1] = x
    return xs_ref[0:G]


def _rglru_kernel(rx_ref, rg_ref, xprev_ref, h0_ref, cw_ref, cb_ref, wg_ref, bx_ref, ba_ref, l_ref,
                  y_ref, xlast_ref, hlast_ref, *scratch, carry):
    R, C = rx_ref.shape
    G = R // SUBLANES
    x = rx_ref[...].reshape(G, SUBLANES, C)
    if carry:
        xs_ref, a_ref, u_ref, h_ref, hc_ref = scratch
        first = pl.program_id(1) == 0
        xprev = _stage_groups(x, xprev_ref, xs_ref, first)

        @pl.when(first)
        def _():
            hc_ref[...] = h0_ref[0]
    else:
        xprev = _state_group(xprev_ref)

    xc = (_causal_conv(x, xprev, cw_ref) + cb_ref[...][None]).reshape(R, C)
    xb = xc.astype(BF16)
    bs = C // RNN_BLOCKS
    zi, zr = [], []
    for n in range(RNN_BLOCKS):
        z = jnp.dot(xb[:, n * bs:(n + 1) * bs], wg_ref[n], preferred_element_type=F32)
        zi.append(z[:, :bs])
        zr.append(z[:, bs:])
    gi = _sigmoid(jnp.concatenate(zi, axis=1) + bx_ref[...])
    gr = _sigmoid(jnp.concatenate(zr, axis=1) + ba_ref[...])
    lv = l_ref[...]
    log_sig_l = -_softplus(-lv)
    log_a = RG_C * gr * log_sig_l
    a = jnp.exp(log_a)
    u = jnp.sqrt(-jnp.tanh(log_a) * (a * a + 1.0)) * (gi * xc)
    a_cum, h_loc = _segment_scan(a.reshape(G, SUBLANES, C), u.reshape(G, SUBLANES, C))

    if carry:
        a_ref[...] = a_cum
        u_ref[...] = h_loc

        def body(g, h_prev):
            hg = u_ref[g] + a_ref[g] * h_prev
            h_ref[g] = hg
            return hg[SUBLANES - 1:SUBLANES, :]

        h_last = lax.fori_loop(0, G, body, hc_ref[...])
        hc_ref[...] = h_last
        h = h_ref[...]
        xlast_ref[0] = x[G - 1]
        hlast_ref[0] = h_last
    else:
        h = h_loc + a_cum * h0_ref[...]
        _store_state(xlast_ref, x)
        hlast_ref[...] = h[:, SUBLANES - 1:SUBLANES, :]

    y = h.reshape(R, C) * jax.nn.gelu(rg_ref[...])
    y_ref[...] = y.astype(BF16)


def rglru_branch(proj, col_rx, col_rg, xprev, h0, cw, cb, wg, bx, ba, lam, *, n_seq, seq_len, rows):
    C = cw.shape[1]
    carry = seq_len > SUBLANES
    if carry:
        nt = seq_len // rows
        grid = (n_seq, nt)
        row_map = lambda b, t: (b * nt + t)
        nb = 1
        G = rows // SUBLANES
        scratch = [pltpu.VMEM((G + 1, SUBLANES, C), F32), pltpu.VMEM((G, SUBLANES, C), F32),
                   pltpu.VMEM((G, SUBLANES, C), F32), pltpu.VMEM((G, SUBLANES, C), F32),
                   pltpu.VMEM((1, C), F32)]
    else:
        nb = rows // SUBLANES
        grid = (n_seq // nb, 1)
        row_map = lambda b, t: b
        scratch = []
    conv_spec, conv_shape = _conv_state_spec(carry, nb, n_seq, C)
    const2 = lambda b, t: (0, 0)
    in_specs = [pl.BlockSpec((rows, C), lambda b, t: (row_map(b, t), col_rx)),
                pl.BlockSpec((rows, C), lambda b, t: (row_map(b, t), col_rg)),
                conv_spec,
                pl.BlockSpec((nb, 1, C), lambda b, t: (b, 0, 0)),
                pl.BlockSpec((CONV_W, C), const2),
                pl.BlockSpec((1, C), const2),
                pl.BlockSpec(wg.shape, lambda b, t: (0, 0, 0)),
                pl.BlockSpec((1, C), const2), pl.BlockSpec((1, C), const2), pl.BlockSpec((1, C), const2)]
    tokens = n_seq * seq_len
    out_shape = [jax.ShapeDtypeStruct((tokens, C), BF16),
                 jax.ShapeDtypeStruct(conv_shape, F32),
                 jax.ShapeDtypeStruct((n_seq, 1, C), F32)]
    out_specs = [pl.BlockSpec((rows, C), lambda b, t: (row_map(b, t), 0)),
                 conv_spec,
                 pl.BlockSpec((nb, 1, C), lambda b, t: (b, 0, 0))]
    return pl.pallas_call(
        functools.partial(_rglru_kernel, carry=carry),
        grid=grid, in_specs=in_specs, out_specs=out_specs, out_shape=out_shape,
        scratch_shapes=scratch,
        compiler_params=_params("parallel", "arbitrary"),
        name="rglru",
    )(proj, proj, xprev, h0, cw, cb.reshape(1, C), wg, bx.reshape(1, C), ba.reshape(1, C),
      lam.reshape(1, C))


def _l2norm_heads(x, scale):
    dh = x.shape[1] // GDN_HEADS
    outs = []
    for h in range(GDN_HEADS):
        xh = x[:, h * dh:(h + 1) * dh]
        xh = xh * lax.rsqrt(jnp.sum(xh * xh, axis=-1, keepdims=True) + EPS)
        outs.append(xh * scale if scale != 1.0 else xh)
    return jnp.concatenate(outs, axis=1)


def _gdn_conv_norm(src_refs, xprev_ref, cw_ref, dst_refs, xlast_ref, xs_ref, carry):
    R, C = src_refs[0].shape
    G = R // SUBLANES
    dk = C // GDN_HEADS
    for s, (src, dst) in enumerate(zip(src_refs, dst_refs)):
        x = src[...].reshape(G, SUBLANES, C)
        cols = slice(s * C, (s + 1) * C)
        if carry:
            xprev = _stage_groups(x, xprev_ref.at[:, :, cols], xs_ref.at[s], pl.program_id(1) == 0)
            xlast_ref[0, :, cols] = x[G - 1]
        else:
            xprev = _state_group(xprev_ref, cols)
            _store_state(xlast_ref, x, cols)
        y = _silu(_causal_conv(x, xprev, cw_ref.at[:, cols])).reshape(R, C)
        if s == 0:
            y = _l2norm_heads(y, dk ** -0.5)
        elif s == 1:
            y = _l2norm_heads(y, 1.0)
        dst[...] = y


def _segment_cumsum(x, seg):
    pos = lax.broadcasted_iota(jnp.int32, x.shape, 0) & (seg - 1)
    s = 1
    while s < seg:
        x = x + jnp.where(pos >= s, pltpu.roll(x, s, 0), 0.0)
        s *= 2
    return x


def _segment_last(x, seg):
    n = x.shape[0]
    pos = lax.broadcasted_iota(jnp.int32, x.shape, 0) & (seg - 1)
    s = seg // 2
    while s >= 1:
        x = jnp.where((pos & (2 * s - 1)) < s, pltpu.roll(x, n - s, 0), x)
        s //= 2
    return x


class _TileMasks:
    def __init__(self, rows, seg):
        r = lax.broadcasted_iota(jnp.int32, (rows, rows), 0)
        c = lax.broadcasted_iota(jnp.int32, (rows, rows), 1)
        shift = seg.bit_length() - 1
        same = (r >> shift) == (c >> shift)
        self.incl = same & (r >= c)
        self.strict = same & (r > c)
        self.eye = r == c
        self.levels = []
        s = 1
        while s < seg:
            b = s.bit_length() - 1
            self.levels.append(((r >> (b + 1)) == (c >> (b + 1))) & (((r >> b) & 1) == 1) & (((c >> b) & 1) == 0))
            s *= 2


def _unit_lower_inverses(ms, masks):
    eye = jnp.where(masks.eye, 1.0, 0.0)
    xs = [eye - jnp.where(masks.levels[0], m, 0.0) for m in ms]
    for level in masks.levels[1:]:
        ts = [_dot(jnp.where(level, m, 0.0), x) for m, x in zip(ms, xs)]
        xs = [x - _dot(x, t) for x, t in zip(xs, ts)]
    return xs


def _gdn_tiles_local(chains, masks):
    r, d = chains[0][0].shape
    kbs, decays, kk_qks = [], [], []
    for q, k, v, beta, gc, g_last in chains:
        gc_row = jnp.sum(jnp.where(masks.eye, gc, 0.0), axis=0, keepdims=True)
        decays.append(jnp.where(masks.incl, jnp.exp(gc - gc_row), 0.0))
        kbs.append(k * beta)
        kk_qks.append(_dot_nt(jnp.concatenate([kbs[-1], q], axis=0), k))
    ms = [jnp.where(masks.strict, kq[:r] * dec, 0.0) for kq, dec in zip(kk_qks, decays)]
    tms = _unit_lower_inverses(ms, masks)
    out = []
    for (q, k, v, beta, gc, g_last), kb, dec, kq, tm in zip(chains, kbs, decays, kk_qks, tms):
        eg = jnp.exp(gc)
        vk = _dot(tm, jnp.concatenate([v * beta, kb * eg], axis=1))
        out.append(dict(value=vk[:, :d], kcd=vk[:, d:], attn=kq[r:] * dec, qg=q * eg,
                        kd=k * jnp.exp(g_last - gc), decay_last=jnp.exp(g_last)))
    return out


def _gdn_kernel(gq_ref, gk_ref, gv_ref, z_ref, sm_ref, xprev_ref, cw_ref, alog_ref, dtb_ref, ng_ref, s0_ref,
                o_ref, s_ref, xlast_ref, q_ref, k_ref, v_ref, *scratch, tile, seg, n_tiles, carry):
    dk = q_ref.shape[1] // GDN_HEADS
    n_seg = tile // seg
    masks = _TileMasks(tile, seg)
    neg_a = -jnp.exp(alog_ref[...])
    dtb = dtb_ref[...]
    ng = ng_ref[...]
    if carry:
        @pl.when(pl.program_id(1) == 0)
        def _():
            s_ref[...] = s0_ref[...]
    _gdn_conv_norm((gq_ref, gk_ref, gv_ref), xprev_ref, cw_ref, (q_ref, k_ref, v_ref), xlast_ref,
                   scratch[0] if carry else None, carry)

    chains = []
    for i in range(n_tiles):
        rows = slice(i * tile, (i + 1) * tile)
        sm = sm_ref[rows, :]
        beta_all = _sigmoid(sm)
        gc_all = _segment_cumsum(neg_a * _softplus(sm + dtb), seg)
        gl_all = _segment_last(gc_all, seg)
        for h in range(GDN_HEADS):
            cols = slice(h * dk, (h + 1) * dk)
            lane = slice(GDN_HEADS + h, GDN_HEADS + h + 1)
            chains.append((q_ref[rows, cols], k_ref[rows, cols], v_ref[rows, cols],
                           beta_all[:, h:h + 1], gc_all[:, lane], gl_all[:, lane]))
    local = _gdn_tiles_local(chains, masks)

    heads = range(GDN_HEADS)
    for i in range(n_tiles):
        rows = slice(i * tile, (i + 1) * tile)
        loc = local[i * GDN_HEADS:(i + 1) * GDN_HEADS]
        v_new = [[] for _ in heads]
        o_state = [[] for _ in heads]
        for j in range(n_seg):
            sl = slice(j * seg, (j + 1) * seg)
            seq = 0 if carry else i * n_seg + j
            s_prev = [s_ref[seq, h] if carry else s0_ref[seq, h] for h in heads]
            rs = [_dot(jnp.concatenate([loc[h]["kcd"][sl], loc[h]["qg"][sl]], axis=0), s_prev[h])
                  for h in heads]
            vns = [loc[h]["value"][sl] - rs[h][:seg] for h in heads]
            for h in heads:
                v_new[h].append(vns[h])
                o_state[h].append(rs[h][seg:])
                s_ref[seq, h] = (s_prev[h] * loc[h]["decay_last"][j * seg:j * seg + 1, :]
                                 + _dot_tn(loc[h]["kd"][sl], vns[h]))
        for h in heads:
            cols = slice(h * dk, (h + 1) * dk)
            vn = jnp.concatenate(v_new[h], axis=0) if n_seg > 1 else v_new[h][0]
            os_ = jnp.concatenate(o_state[h], axis=0) if n_seg > 1 else o_state[h][0]
            o = os_ + _dot(loc[h]["attn"], vn)
            o_ref[rows, cols] = (_rmsnorm(o, ng) * _silu(z_ref[rows, cols])).astype(o_ref.dtype)


def gdn_branch(proj, col_q, col_z, small, xprev, cw, alog_row, dtb_row, ng, s0, *, n_seq, seq_len,
               tiles_per_step):
    tokens = proj.shape[0]
    C = cw.shape[1] // 3
    dk = C // GDN_HEADS
    tile = GDN_CHUNK
    carry = seq_len >= GDN_CHUNK
    seg = GDN_CHUNK if carry else seq_len
    rows = tiles_per_step * tile
    if carry:
        nt = seq_len // rows
        seq_per_step = 1
        grid = (n_seq, nt)
    else:
        nt = 1
        seq_per_step = rows // seg
        grid = (n_seq // seq_per_step, 1)
    row_map = lambda b, t: (b * nt + t)
    tok_spec = pl.BlockSpec((rows, C), lambda b, t: (row_map(b, t), 0))
    const2 = lambda b, t: (0, 0)
    s_spec = pl.BlockSpec((seq_per_step, GDN_HEADS, dk, dk), lambda b, t: (b, 0, 0, 0))
    conv_spec, conv_shape = _conv_state_spec(carry, seq_per_step, n_seq, 3 * C)
    in_specs = [pl.BlockSpec((rows, C), lambda b, t, c=c: (row_map(b, t), c))
                for c in (col_q, col_q + 1, col_q + 2, col_z)]
    in_specs += [pl.BlockSpec((rows, LANES), lambda b, t: (row_map(b, t), 0)),
                 conv_spec, pl.BlockSpec((CONV_W, 3 * C), const2),
                 pl.BlockSpec((1, LANES), const2), pl.BlockSpec((1, LANES), const2),
                 pl.BlockSpec((1, dk), const2), s_spec]
    scratch = [pltpu.VMEM((rows, C), F32)] * 3
    if carry:
        scratch.append(pltpu.VMEM((3, rows // SUBLANES + 1, SUBLANES, C), F32))
    return pl.pallas_call(
        functools.partial(_gdn_kernel, tile=tile, seg=seg, n_tiles=tiles_per_step, carry=carry),
        grid=grid, in_specs=in_specs,
        out_specs=[tok_spec, s_spec, conv_spec],
        out_shape=[jax.ShapeDtypeStruct((tokens, C), _lhs_dtype(tile)), jax.ShapeDtypeStruct(s0.shape, F32),
                   jax.ShapeDtypeStruct(conv_shape, F32)],
        scratch_shapes=scratch,
        compiler_params=_params("parallel", "arbitrary"),
        name="gdn_branch",
    )(proj, proj, proj, proj, small, xprev, cw, alog_row, dtb_row, ng.reshape(1, dk), s0)


def _mem_attn_kernel(q_ref, k_ref, v_ref, o_ref, *, n_seq, tq):
    hd = q_ref.shape[1] // MEM_HEADS
    scale = hd ** -0.5
    rows = lambda s: slice(s * tq, (s + 1) * tq)
    cols = lambda h: slice(h * hd, (h + 1) * hd)
    if len(k_ref.shape) == 4:
        mem_rows = k_ref.shape[1] * MEM_HEADS
        r_head = lax.broadcasted_iota(jnp.int32, (MEM_HEADS * tq, mem_rows), 0) // tq
        c_head = lax.broadcasted_iota(jnp.int32, (MEM_HEADS * tq, mem_rows), 1) & (MEM_HEADS - 1)
        own = r_head == c_head
        scores = []
        for s in range(n_seq):
            q_all = jnp.concatenate([q_ref[rows(s), cols(h)] for h in range(MEM_HEADS)], axis=0)
            sc = _dot_nt(q_all, k_ref[s].reshape(mem_rows, hd)) * scale
            scores.append(jnp.where(own, sc, -jnp.inf))
        probs = []
        for sc in scores:
            e = jnp.exp(sc - jnp.max(sc, axis=-1, keepdims=True))
            probs.append(e / jnp.sum(e, axis=-1, keepdims=True))
        for s, p in enumerate(probs):
            o_all = _dot(p, v_ref[s].reshape(mem_rows, hd))
            for h in range(MEM_HEADS):
                o_ref[rows(s), cols(h)] = o_all[h * tq:(h + 1) * tq].astype(o_ref.dtype)
        return
    pairs = [(s, h) for s in range(n_seq) for h in range(MEM_HEADS)]
    mem = lambda ref, s, h: ref[s, :, cols(h)]
    scores = [_dot_nt(q_ref[rows(s), cols(h)], mem(k_ref, s, h)) * scale for s, h in pairs]
    probs = []
    for sc in scores:
        e = jnp.exp(sc - jnp.max(sc, axis=-1, keepdims=True))
        probs.append(e / jnp.sum(e, axis=-1, keepdims=True))
    for (s, h), p in zip(pairs, probs):
        o_ref[rows(s), cols(h)] = _dot(p, mem(v_ref, s, h)).astype(o_ref.dtype)


def mem_attention(proj, col_q, mem_k, col_k, mem_v, col_v, *, width, n_seq, seq_len, tq, seq_per_step):
    tokens = n_seq * seq_len
    mem_len = mem_k.shape[1]
    C = width
    nt = seq_len // tq
    rows = seq_per_step * tq
    grid = (n_seq // seq_per_step, nt)
    row_map = lambda b, t: (b * nt + t)
    if mem_k.ndim == 4:
        blk = (seq_per_step,) + mem_k.shape[1:]
        k_spec = v_spec = pl.BlockSpec(blk, lambda b, t: (b, 0, 0, 0))
    else:
        k_spec = pl.BlockSpec((seq_per_step, mem_len, C), lambda b, t: (b, 0, col_k))
        v_spec = pl.BlockSpec((seq_per_step, mem_len, C), lambda b, t: (b, 0, col_v))
    return pl.pallas_call(
        functools.partial(_mem_attn_kernel, n_seq=seq_per_step, tq=tq),
        grid=grid,
        in_specs=[pl.BlockSpec((rows, C), lambda b, t: (row_map(b, t), col_q)), k_spec, v_spec],
        out_specs=pl.BlockSpec((rows, C), lambda b, t: (row_map(b, t), 0)),
        out_shape=jax.ShapeDtypeStruct((tokens, C), _lhs_dtype(tq)),
        compiler_params=_params("parallel", "arbitrary"),
        name="mem_attention",
    )(proj, mem_k, mem_v)


def _merge_kernel(yr_ref, yg_ref, ym_ref, g0_ref, g1_ref, g2_ref, w0_ref, w1_ref, w2_ref, o_ref):
    acc = _sigmoid(g0_ref[...]) * _dot(yr_ref[...], w0_ref[...])
    acc = acc + _sigmoid(g1_ref[...]) * _dot(yg_ref[...], w1_ref[...])
    acc = acc + _sigmoid(g2_ref[...]) * _dot(ym_ref[...], w2_ref[...])
    o_ref[...] = acc.astype(BF16)


def merge_branches(y_rnn, y_gdn, y_mem, proj, col_gate, w_rnn_up, w_gdn_up, w_mem_up, *, tm, tn):
    M, C = y_rnn.shape
    N = w_rnn_up.shape[1]
    nj = N // tn
    y_spec = pl.BlockSpec((tm, C), lambda i, j: (i, 0), pipeline_mode=pl.Buffered(1))
    w_spec = pl.BlockSpec((C, tn), lambda i, j: (0, j))
    gate_specs = [pl.BlockSpec((tm, tn), lambda i, j, b=b: (i, col_gate + b * nj + j)) for b in range(N_BRANCH)]
    return pl.pallas_call(
        _merge_kernel,
        grid=(M // tm, nj),
        in_specs=[y_spec, y_spec, y_spec] + gate_specs + [w_spec, w_spec, w_spec],
        out_specs=pl.BlockSpec((tm, tn), lambda i, j: (i, j)),
        out_shape=jax.ShapeDtypeStruct((M, N), BF16),
        compiler_params=_params("parallel", "arbitrary"),
        name="merge_branches",
    )(y_rnn, y_gdn, y_mem, proj, proj, proj, w_rnn_up, w_gdn_up, w_mem_up)


def _matmul_residual_kernel(a_ref, w_ref, x_ref, o_ref):
    o_ref[...] = x_ref[...] + _dot(a_ref[...], w_ref[...])


def matmul_residual(a, w, x, *, tm, tn):
    M, K = a.shape
    N = w.shape[1]
    return pl.pallas_call(
        _matmul_residual_kernel,
        grid=(M // tm, N // tn),
        in_specs=[pl.BlockSpec((tm, K), lambda i, j: (i, 0), pipeline_mode=pl.Buffered(1)),
                  pl.BlockSpec((K, tn), lambda i, j: (0, j)),
                  pl.BlockSpec((tm, tn), lambda i, j: (i, j))],
        out_specs=pl.BlockSpec((tm, tn), lambda i, j: (i, j)),
        out_shape=jax.ShapeDtypeStruct((M, N), F32),
        compiler_params=_params("parallel", "arbitrary"),
        name="matmul_residual",
    )(a, w, x)


def _mlp_kernel(x_ref, g_ref, wu_ref, wd_ref, gf_ref, o_ref, xn_ref):
    j = pl.program_id(1)

    @pl.when(j == 0)
    def _():
        xn_ref[...] = _rmsnorm(x_ref[...], g_ref[...]).astype(BF16)
        o_ref[...] = jnp.zeros_like(o_ref)

    hid = _dot(xn_ref[...], wu_ref[...])
    act = jnp.square(jnp.maximum(hid, 0.0))
    o_ref[...] += _dot(act, wd_ref[...])

    @pl.when(j == pl.num_programs(1) - 1)
    def _():
        o_ref[...] = _rmsnorm(x_ref[...] + o_ref[...], gf_ref[...])


def mlp_final_norm(x, g, w_up, w_down, g_final, *, tm, tf):
    M, D = x.shape
    FF = w_up.shape[1]
    return pl.pallas_call(
        _mlp_kernel,
        grid=(M // tm, FF // tf),
        in_specs=[pl.BlockSpec((tm, D), lambda i, j: (i, 0), pipeline_mode=pl.Buffered(1)),
                  pl.BlockSpec((1, D), lambda i, j: (0, 0)),
                  pl.BlockSpec((D, tf), lambda i, j: (0, j)),
                  pl.BlockSpec((tf, D), lambda i, j: (j, 0)),
                  pl.BlockSpec((1, D), lambda i, j: (0, 0))],
        out_specs=pl.BlockSpec((tm, D), lambda i, j: (i, 0)),
        out_shape=jax.ShapeDtypeStruct((M, D), F32),
        scratch_shapes=[pltpu.VMEM((tm, D), BF16)],
        compiler_params=_params("parallel", "arbitrary"),
        name="mlp_final_norm",
    )(x, g.reshape(1, D), w_up, w_down, g_final.reshape(1, D))


def _in_proj_segments(d_model):
    half = d_model // 2
    small_row = 6 * half
    mq_row = small_row + 2 * GDN_HEADS
    mg_row = mq_row + half
    return ((mg_row, N_BRANCH * d_model), (0, 6 * half), (mq_row, half)), small_row


def _prep_layer_weights(rnn_wx, rnn_wa, gdn_A_log, gdn_dt_bias):
    w_gate = jnp.concatenate([rnn_wx, rnn_wa], axis=-1).astype(BF16)
    lane_pad = (GDN_HEADS, LANES - 2 * GDN_HEADS)
    alog_row = jnp.pad(gdn_A_log, lane_pad).reshape(1, LANES)
    dtb_row = jnp.pad(gdn_dt_bias, lane_pad).reshape(1, LANES)
    return w_gate, alog_row, dtb_row


def _conv_state_in(buf, carry):
    if carry:
        return jnp.pad(buf, ((0, 0), (SUBLANES - (CONV_W - 1), 0), (0, 0)))
    return jnp.swapaxes(buf, 0, 1)


def _conv_state_out(state, carry):
    return state[:, SUBLANES - (CONV_W - 1):] if carry else jnp.swapaxes(state, 0, 1)


def _group_layer(x, mem_k, col_k, mem_v, col_v, rnn_buf, rnn_h0, gdn_buf, gdn_s0, lw, *, n_seq, seq_len,
                 tm, proj_tm, row_tile, attn_tq, attn_seqs, gdn_tiles, final_g):
    D = x.shape[1]
    half = D // 2
    n_gate_blk = N_BRANCH * D // half
    c_rx, c_rg, c_gq, c_gz, c_mq = (n_gate_blk + i for i in (0, 1, 2, 5, 6))
    carry = seq_len > SUBLANES
    assert carry == (seq_len >= GDN_CHUNK)
    segments, small_row = _in_proj_segments(D)
    proj, small = in_projection(x, lw["norm_mix_g"], lw["w_in_t"], segments, small_row, tm=proj_tm,
                                tn=512 if proj_tm > 1024 else 1024)

    y_rnn, rnn_last, h_last = rglru_branch(
        proj, c_rx, c_rg, _conv_state_in(rnn_buf, carry), rnn_h0.reshape(n_seq, 1, half),
        lw["rnn_conv_w"], lw["rnn_conv_b"], lw["w_gate"], lw["rnn_bx"], lw["rnn_ba"], lw["rnn_L"],
        n_seq=n_seq, seq_len=seq_len, rows=row_tile)

    y_gdn, s_new, gdn_last = gdn_branch(
        proj, c_gq, c_gz, small, _conv_state_in(gdn_buf, carry), lw["gdn_conv_w"], lw["alog_row"],
        lw["dtb_row"],
        lw["gdn_norm_g"], gdn_s0, n_seq=n_seq, seq_len=seq_len, tiles_per_step=gdn_tiles)

    y_mem = mem_attention(proj, c_mq, mem_k, col_k, mem_v, col_v, width=half, n_seq=n_seq, seq_len=seq_len,
                          tq=attn_tq, seq_per_step=attn_seqs)

    merged = merge_branches(y_rnn, y_gdn, y_mem, proj, 0, lw["w_rnn_up"], lw["w_gdn_up"], lw["w_mem_up"],
                            tm=proj_tm, tn=256)
    x1 = matmul_residual(merged, lw["w_out"], x, tm=proj_tm, tn=512)
    x2 = mlp_final_norm(x1, lw["norm_mlp_g"], lw["w_mlp_up"], lw["w_mlp_down"], final_g, tm=tm, tf=512)
    states = (_conv_state_out(rnn_last, carry), h_last.reshape(n_seq, half),
              _conv_state_out(gdn_last, carry), s_new)
    return x2, states


def kernel(x_prompt, x_sample, mem_prompt, cache_mem_k, cache_mem_v, state_rnn_conv, state_rnn_h,
           state_gdn_conv, state_gdn_S, norm_mix_g, w_in, rnn_conv_w, rnn_conv_b, rnn_wx, rnn_bx, rnn_wa,
           rnn_ba, rnn_L, gdn_conv_w, gdn_A_log, gdn_dt_bias, gdn_norm_g, mem_norm_g, w_mem_kv, w_rnn_up,
           w_gdn_up, w_mem_up, w_out, norm_mlp_g, w_mlp_up, w_mlp_down, norm_final_g):
    depth = w_in.shape[0]
    assert depth == 1, "the final norm is fused into the last layer's MLP kernel; one layer supported"
    Bp, T, D = x_prompt.shape
    Bs, Ts, _ = x_sample.shape
    half = D // 2
    mem_len = mem_prompt.shape[1]
    assert Ts == SUBLANES and T % GDN_CHUNK == 0
    l = 0
    w_gate, alog_row, dtb_row = _prep_layer_weights(rnn_wx[l], rnn_wa[l], gdn_A_log[l], gdn_dt_bias[l])
    lw = dict(norm_mix_g=norm_mix_g[l], w_in_t=jnp.swapaxes(w_in[l], 0, 1), w_gate=w_gate,
              rnn_conv_w=rnn_conv_w[l], rnn_conv_b=rnn_conv_b[l], rnn_bx=rnn_bx[l], rnn_ba=rnn_ba[l],
              rnn_L=rnn_L[l], gdn_conv_w=gdn_conv_w[l], alog_row=alog_row, dtb_row=dtb_row,
              gdn_norm_g=gdn_norm_g[l], w_rnn_up=w_rnn_up[l], w_gdn_up=w_gdn_up[l],
              w_mem_up=w_mem_up[l], w_out=w_out[l], norm_mlp_g=norm_mlp_g[l],
              w_mlp_up=w_mlp_up[l], w_mlp_down=w_mlp_down[l])

    kv = norm_matmul(mem_prompt.reshape(Bp * mem_len, D), mem_norm_g[l], w_mem_kv[l],
                     tm=min(Bp * mem_len, 1024), tn=512)
    kv3 = kv.reshape(Bp, mem_len, 2 * half)
    zeros = lambda *s: jnp.zeros(s, F32)
    yp, (rb_p, rh_p, gb_p, gs_p) = _group_layer(
        x_prompt.reshape(Bp * T, D), kv3, 0, kv3, 1,
        zeros(Bp, CONV_W - 1, half), zeros(Bp, half), zeros(Bp, CONV_W - 1, 3 * half),
        zeros(Bp, GDN_HEADS, half // GDN_HEADS, half // GDN_HEADS), lw,
        n_seq=Bp, seq_len=T, tm=min(Bp * T, 1024), proj_tm=min(Bp * T, 2048), row_tile=512, attn_tq=min(T, 1024), attn_seqs=1,
        gdn_tiles=min(4, T // GDN_CHUNK),
        final_g=norm_final_g)
    mk_p = kv3[:, :, :half].reshape(1, Bp, mem_len, MEM_HEADS, half // MEM_HEADS)
    mv_p = kv3[:, :, half:].reshape(1, Bp, mem_len, MEM_HEADS, half // MEM_HEADS)

    ys, (rb_s, rh_s, gb_s, gs_s) = _group_layer(
        x_sample.reshape(Bs * Ts, D), cache_mem_k.reshape((depth * Bs,) + cache_mem_k.shape[2:]), 0,
        cache_mem_v.reshape((depth * Bs,) + cache_mem_v.shape[2:]), 0,
        state_rnn_conv[l], state_rnn_h[l], state_gdn_conv[l], state_gdn_S[l], lw,
        n_seq=Bs, seq_len=Ts, tm=min(Bs * Ts, 1024), proj_tm=min(Bs * Ts, 1024), row_tile=min(Bs, 16) * SUBLANES, attn_tq=Ts,
        attn_seqs=min(Bs, 8), gdn_tiles=max(1, min(2, Bs * Ts // GDN_CHUNK)), final_g=norm_final_g)

    return (yp.reshape(Bp, T, D), ys.reshape(Bs, Ts, D), mk_p, mv_p, rb_p[None], rh_p[None], gb_p[None],
            gs_p[None], rb_s[None], rh_s[None], gb_s[None], gs_s[None])
```

```python
import functools

import jax
import jax.numpy as jnp
from jax import lax
from jax.experimental import pallas as pl
from jax.experimental.pallas import tpu as pltpu

F32 = jnp.float32
BF16 = jnp.bfloat16

EPS = 1e-6
RG_C = 8.0
CONV_W = 4
RNN_BLOCKS = 8
GDN_HEADS = 8
GDN_CHUNK = 64
MEM_HEADS = 4
N_BRANCH = 3

SUBLANES = 8
BF16_SUBLANES = 16
LANES = 128
VMEM_LIMIT_BYTES = 56 * 1024 * 1024


def _lhs_dtype(slab_rows):
    return BF16 if slab_rows % BF16_SUBLANES == 0 else F32


def _params(*sem):
    return pltpu.CompilerParams(dimension_semantics=sem, vmem_limit_bytes=VMEM_LIMIT_BYTES)


def _dot(a, b):
    return jnp.dot(a.astype(BF16), b.astype(BF16), preferred_element_type=F32)


def _dot_nt(a, b):
    return lax.dot_general(a.astype(BF16), b.astype(BF16), (((1,), (1,)), ((), ())),
                           preferred_element_type=F32)


def _dot_tn(a, b):
    return lax.dot_general(a.astype(BF16), b.astype(BF16), (((0,), (0,)), ((), ())),
                           preferred_element_type=F32)


def _rmsnorm(x, g):
    return (x * lax.rsqrt(jnp.mean(x * x, axis=-1, keepdims=True) + EPS)) * g


def _softplus(x):
    return jnp.maximum(x, 0.0) + jnp.log1p(jnp.exp(-jnp.abs(x)))


_sigmoid = jax.nn.sigmoid


def _silu(x):
    return x * _sigmoid(x)


def _norm_matmul_kernel(*refs, has_small):
    if has_small:
        x_ref, g_ref, w_ref, ws_ref, o_ref, os_ref, xn_ref = refs
    else:
        x_ref, g_ref, w_ref, o_ref, xn_ref = refs

    @pl.when(pl.program_id(1) == 0)
    def _():
        xn_ref[...] = _rmsnorm(x_ref[...], g_ref[...]).astype(BF16)
        if has_small:
            os_ref[...] = jnp.dot(xn_ref[...], ws_ref[...], preferred_element_type=F32)

    o_ref[...] = _dot(xn_ref[...], w_ref[...])


def norm_matmul(x, g, w, w_small=None, *, tm, tn):
    M, K = x.shape
    N = w.shape[1]
    has_small = w_small is not None
    in_specs = [pl.BlockSpec((tm, K), lambda i, j: (i, 0)),
                pl.BlockSpec((1, K), lambda i, j: (0, 0)),
                pl.BlockSpec((K, tn), lambda i, j: (0, j))]
    out_shape = [jax.ShapeDtypeStruct((M, N), F32)]
    out_specs = [pl.BlockSpec((tm, tn), lambda i, j: (i, j))]
    args = [x, g.reshape(1, K), w]
    if has_small:
        ns = w_small.shape[1]
        in_specs.append(pl.BlockSpec((K, ns), lambda i, j: (0, 0)))
        out_shape.append(jax.ShapeDtypeStruct((M, ns), F32))
        out_specs.append(pl.BlockSpec((tm, ns), lambda i, j: (i, 0)))
        args.append(w_small)
    outs = pl.pallas_call(
        functools.partial(_norm_matmul_kernel, has_small=has_small),
        grid=(M // tm, N // tn),
        in_specs=in_specs, out_specs=out_specs, out_shape=out_shape,
        scratch_shapes=[pltpu.VMEM((tm, K), BF16)],
        compiler_params=_params("parallel", "arbitrary"),
        name="norm_matmul",
    )(*args)
    return outs if has_small else outs[0]


def _row_tile_prefetch(x_hbm, x_buf, sem, consume):
    i, j = pl.program_id(0), pl.program_id(1)
    tm = x_buf.shape[0]

    def copy(tile):
        return pltpu.make_async_copy(x_hbm.at[pl.ds(pl.multiple_of(tile * tm, tm), tm), :], x_buf, sem)

    @pl.when((i == 0) & (j == 0))
    def _():
        copy(0).start()

    @pl.when(j == 0)
    def _():
        copy(i).wait()
        consume(x_buf)

    @pl.when((j == 1) & (i + 1 < pl.num_programs(0)))
    def _():
        copy(i + 1).start()


def _in_proj_kernel(x_hbm, g_ref, wt_ref, wst_ref, o_ref, os_ref, xn_ref, x_buf, sem):
    def consume(x_ref):
        xn_ref[...] = _rmsnorm(x_ref[...], g_ref[...]).astype(BF16)
        os_ref[...] = _dot_nt(xn_ref[...], wst_ref[...])

    _row_tile_prefetch(x_hbm, x_buf, sem, consume)
    o_ref[...] = _dot_nt(xn_ref[...], wt_ref[...])


def in_projection(x, g, w_t, segments, small_row, *, tm, tn):
    M, K = x.shape
    n_tiles = [n // tn for _, n in segments]
    assert all(n % tn == 0 and r % SUBLANES == 0 for r, n in segments) and small_row % LANES == 0
    N = tn * sum(n_tiles)

    def w_row(j):
        row, first = None, 0
        for (r0, _), nt in zip(segments, n_tiles):
            cand = r0 // SUBLANES + (tn // SUBLANES) * (j - first)
            row = cand if row is None else jnp.where(j >= first, cand, row)
            first += nt
        return row * SUBLANES

    assert N // tn >= 2
    outs = pl.pallas_call(
        _in_proj_kernel,
        grid=(M // tm, N // tn),
        in_specs=[pl.BlockSpec(memory_space=pl.ANY),
                  pl.BlockSpec((1, K), lambda i, j: (0, 0)),
                  pl.BlockSpec((pl.Element(tn), pl.Element(K)), lambda i, j: (w_row(j), 0)),
                  pl.BlockSpec((LANES, K), lambda i, j: (small_row // LANES, 0))],
        out_specs=[pl.BlockSpec((tm, tn), lambda i, j: (i, j)), pl.BlockSpec((tm, LANES), lambda i, j: (i, 0))],
        out_shape=[jax.ShapeDtypeStruct((M, N), F32), jax.ShapeDtypeStruct((M, LANES), F32)],
        scratch_shapes=[pltpu.VMEM((tm, K), BF16), pltpu.VMEM((tm, K), F32), pltpu.SemaphoreType.DMA(())],
        compiler_params=_params("arbitrary", "arbitrary"),
        name="in_projection",
    )(x, g.reshape(1, K), w_t, w_t)
    return outs


def _causal_conv(x, xprev, w_ref):
    t_idx = lax.broadcasted_iota(jnp.int32, x.shape, 1)
    y = x * w_ref[CONV_W - 1:CONV_W, :][None]
    for k in range(1, CONV_W):
        shifted = jnp.where(t_idx >= k, pltpu.roll(x, k, 1), pltpu.roll(xprev, k, 1))
        y = y + shifted * w_ref[CONV_W - 1 - k:CONV_W - k, :][None]
    return y


def _segment_scan(a, u):
    t_idx = lax.broadcasted_iota(jnp.int32, a.shape, 1)
    s = 1
    while s < SUBLANES:
        keep = t_idx >= s
        u = jnp.where(keep, a * pltpu.roll(u, s, 1) + u, u)
        a = jnp.where(keep, a * pltpu.roll(a, s, 1), a)
        s *= 2
    return a, u


def _state_group(state_ref, cols=slice(None)):
    rows, nb = state_ref.shape[0], state_ref.shape[1]
    first = [state_ref[r, :, cols] for r in range(rows)]
    t_idx = lax.broadcasted_iota(jnp.int32, (nb, SUBLANES, first[0].shape[-1]), 1)
    y = jnp.zeros(t_idx.shape, F32)
    for r, row in enumerate(first):
        y = jnp.where(t_idx == SUBLANES - rows + r, row[:, None, :], y)
    return y


def _store_state(state_ref, x, cols=slice(None)):
    rows = state_ref.shape[0]
    for r in range(rows):
        state_ref[r, :, cols] = x[:, SUBLANES - rows + r, :]


def _conv_state_spec(carry, nb, n_seq, width):
    if carry:
        return pl.BlockSpec((nb, SUBLANES, width), lambda b, t: (b, 0, 0)), (n_seq, SUBLANES, width)
    return pl.BlockSpec((CONV_W - 1, nb, width), lambda b, t: (0, b, 0)), (CONV_W - 1, n_seq, width)


def _stage_groups(x, xprev_ref, xs_ref, first):
    G = x.shape[0]

    @pl.when(first)
    def _():
        xs_ref[G] = xprev_ref[0]

    xs_ref[0] = xs_ref[G]
    xs_ref[1:G + 1] = x
    return xs_ref[0:G]


def _rglru_kernel(rx_ref, rg_ref, xprev_ref, h0_ref, cw_ref, cb_ref, wg_ref, bx_ref, ba_ref, l_ref,
                  y_ref, xlast_ref, hlast_ref, *scratch, carry):
    R, C = rx_ref.shape
    G = R // SUBLANES
    x = rx_ref[...].reshape(G, SUBLANES, C)
    if carry:
        xs_ref, a_ref, u_ref, h_ref, hc_ref = scratch
        first = pl.program_id(1) == 0
        xprev = _stage_groups(x, xprev_ref, xs_ref, first)

        @pl.when(first)
        def _():
            hc_ref[...] = h0_ref[0]
    else:
        xprev = _state_group(xprev_ref)

    xc = (_causal_conv(x, xprev, cw_ref) + cb_ref[...][None]).reshape(R, C)
    xb = xc.astype(BF16)
    bs = C // RNN_BLOCKS
    zi, zr = [], []
    for n in range(RNN_BLOCKS):
        z = jnp.dot(xb[:, n * bs:(n + 1) * bs], wg_ref[n], preferred_element_type=F32)
        zi.append(z[:, :bs])
        zr.append(z[:, bs:])
    gi = _sigmoid(jnp.concatenate(zi, axis=1) + bx_ref[...])
    gr = _sigmoid(jnp.concatenate(zr, axis=1) + ba_ref[...])
    lv = l_ref[...]
    log_sig_l = -_softplus(-lv)
    log_a = RG_C * gr * log_sig_l
    a = jnp.exp(log_a)
    u = jnp.sqrt(-jnp.tanh(log_a) * (a * a + 1.0)) * (gi * xc)
    a_cum, h_loc = _segment_scan(a.reshape(G, SUBLANES, C), u.reshape(G, SUBLANES, C))

    if carry:
        a_ref[...] = a_cum
        u_ref[...] = h_loc

        def body(g, h_prev):
            hg = u_ref[g] + a_ref[g] * h_prev
            h_ref[g] = hg
            return hg[SUBLANES - 1:SUBLANES, :]

        h_last = lax.fori_loop(0, G, body, hc_ref[...])
        hc_ref[...] = h_last
        h = h_ref[...]
        xlast_ref[0] = x[G - 1]
        hlast_ref[0] = h_last
    else:
        h = h_loc + a_cum * h0_ref[...]
        _store_state(xlast_ref, x)
        hlast_ref[...] = h[:, SUBLANES - 1:SUBLANES, :]

    y = h.reshape(R, C) * jax.nn.gelu(rg_ref[...])
    y_ref[...] = y.astype(BF16)


def rglru_branch(proj, col_rx, col_rg, xprev, h0, cw, cb, wg, bx, ba, lam, *, n_seq, seq_len, rows):
    C = cw.shape[1]
    carry = seq_len > SUBLANES
    if carry:
        nt = seq_len // rows
        grid = (n_seq, nt)
        row_map = lambda b, t: (b * nt + t)
        nb = 1
        G = rows // SUBLANES
        scratch = [pltpu.VMEM((G + 1, SUBLANES, C), F32), pltpu.VMEM((G, SUBLANES, C), F32),
                   pltpu.VMEM((G, SUBLANES, C), F32), pltpu.VMEM((G, SUBLANES, C), F32),
                   pltpu.VMEM((1, C), F32)]
    else:
        nb = rows // SUBLANES
        grid = (n_seq // nb, 1)
        row_map = lambda b, t: b
        scratch = []
    conv_spec, conv_shape = _conv_state_spec(carry, nb, n_seq, C)
    const2 = lambda b, t: (0, 0)
    in_specs = [pl.BlockSpec((rows, C), lambda b, t: (row_map(b, t), col_rx)),
                pl.BlockSpec((rows, C), lambda b, t: (row_map(b, t), col_rg)),
                conv_spec,
                pl.BlockSpec((nb, 1, C), lambda b, t: (b, 0, 0)),
                pl.BlockSpec((CONV_W, C), const2),
                pl.BlockSpec((1, C), const2),
                pl.BlockSpec(wg.shape, lambda b, t: (0, 0, 0)),
                pl.BlockSpec((1, C), const2), pl.BlockSpec((1, C), const2), pl.BlockSpec((1, C), const2)]
    tokens = n_seq * seq_len
    out_shape = [jax.ShapeDtypeStruct((tokens, C), BF16),
                 jax.ShapeDtypeStruct(conv_shape, F32),
                 jax.ShapeDtypeStruct((n_seq, 1, C), F32)]
    out_specs = [pl.BlockSpec((rows, C), lambda b, t: (row_map(b, t), 0)),
                 conv_spec,
                 pl.BlockSpec((nb, 1, C), lambda b, t: (b, 0, 0))]
    return pl.pallas_call(
        functools.partial(_rglru_kernel, carry=carry),
        grid=grid, in_specs=in_specs, out_specs=out_specs, out_shape=out_shape,
        scratch_shapes=scratch,
        compiler_params=_params("parallel", "arbitrary"),
        name="rglru",
    )(proj, proj, xprev, h0, cw, cb.reshape(1, C), wg, bx.reshape(1, C), ba.reshape(1, C),
      lam.reshape(1, C))


def _l2norm_heads(x, scale):
    dh = x.shape[1] // GDN_HEADS
    outs = []
    for h in range(GDN_HEADS):
        xh = x[:, h * dh:(h + 1) * dh]
        xh = xh * lax.rsqrt(jnp.sum(xh * xh, axis=-1, keepdims=True) + EPS)
        outs.append(xh * scale if scale != 1.0 else xh)
    return jnp.concatenate(outs, axis=1)


def _gdn_conv_norm(src_refs, xprev_ref, cw_ref, dst_refs, xlast_ref, xs_ref, carry):
    R, C = src_refs[0].shape
    G = R // SUBLANES
    dk = C // GDN_HEADS
    for s, (src, dst) in enumerate(zip(src_refs, dst_refs)):
        x = src[...].reshape(G, SUBLANES, C)
        cols = slice(s * C, (s + 1) * C)
        if carry:
            xprev = _stage_groups(x, xprev_ref.at[:, :, cols], xs_ref.at[s], pl.program_id(1) == 0)
            xlast_ref[0, :, cols] = x[G - 1]
        else:
            xprev = _state_group(xprev_ref, cols)
            _store_state(xlast_ref, x, cols)
        y = _silu(_causal_conv(x, xprev, cw_ref.at[:, cols])).reshape(R, C)
        if s == 0:
            y = _l2norm_heads(y, dk ** -0.5)
        elif s == 1:
            y = _l2norm_heads(y, 1.0)
        dst[...] = y


def _segment_cumsum(x, seg):
    pos = lax.broadcasted_iota(jnp.int32, x.shape, 0) & (seg - 1)
    s = 1
    while s < seg:
        x = x + jnp.where(pos >= s, pltpu.roll(x, s, 0), 0.0)
        s *= 2
    return x


def _segment_last(x, seg):
    n = x.shape[0]
    pos = lax.broadcasted_iota(jnp.int32, x.shape, 0) & (seg - 1)
    s = seg // 2
    while s >= 1:
        x = jnp.where((pos & (2 * s - 1)) < s, pltpu.roll(x, n - s, 0), x)
        s //= 2
    return x


class _TileMasks:
    def __init__(self, rows, seg):
        r = lax.broadcasted_iota(jnp.int32, (rows, rows), 0)
        c = lax.broadcasted_iota(jnp.int32, (rows, rows), 1)
        shift = seg.bit_length() - 1
        same = (r >> shift) == (c >> shift)
        self.incl = same & (r >= c)
        self.strict = same & (r > c)
        self.eye = r == c
        self.levels = []
        s = 1
        while s < seg:
            b = s.bit_length() - 1
            self.levels.append(((r >> (b + 1)) == (c >> (b + 1))) & (((r >> b) & 1) == 1) & (((c >> b) & 1) == 0))
            s *= 2


def _unit_lower_inverses(ms, masks):
    eye = jnp.where(masks.eye, 1.0, 0.0)
    xs = [eye - jnp.where(masks.levels[0], m, 0.0) for m in ms]
    for level in masks.levels[1:]:
        ts = [_dot(jnp.where(level, m, 0.0), x) for m, x in zip(ms, xs)]
        xs = [x - _dot(x, t) for x, t in zip(xs, ts)]
    return xs


def _gdn_tiles_local(chains, masks):
    r, d = chains[0][0].shape
    kbs, decays, kk_qks = [], [], []
    for q, k, v, beta, gc, g_last in chains:
        gc_row = jnp.sum(jnp.where(masks.eye, gc, 0.0), axis=0, keepdims=True)
        decays.append(jnp.where(masks.incl, jnp.exp(gc - gc_row), 0.0))
        kbs.append(k * beta)
        kk_qks.append(_dot_nt(jnp.concatenate([kbs[-1], q], axis=0), k))
    ms = [jnp.where(masks.strict, kq[:r] * dec, 0.0) for kq, dec in zip(kk_qks, decays)]
    tms = _unit_lower_inverses(ms, masks)
    out = []
    for (q, k, v, beta, gc, g_last), kb, dec, kq, tm in zip(chains, kbs, decays, kk_qks, tms):
        eg = jnp.exp(gc)
        vk = _dot(tm, jnp.concatenate([v * beta, kb * eg], axis=1))
        out.append(dict(value=vk[:, :d], kcd=vk[:, d:], attn=kq[r:] * dec, qg=q * eg,
                        kd=k * jnp.exp(g_last - gc), decay_last=jnp.exp(g_last)))
    return out


def _gdn_kernel(gq_ref, gk_ref, gv_ref, z_ref, sm_ref, xprev_ref, cw_ref, alog_ref, dtb_ref, ng_ref, s0_ref,
                o_ref, s_ref, xlast_ref, q_ref, k_ref, v_ref, *scratch, tile, seg, n_tiles, carry):
    dk = q_ref.shape[1] // GDN_HEADS
    n_seg = tile // seg
    masks = _TileMasks(tile, seg)
    neg_a = -jnp.exp(alog_ref[...])
    dtb = dtb_ref[...]
    ng = ng_ref[...]
    if carry:
        @pl.when(pl.program_id(1) == 0)
        def _():
            s_ref[...] = s0_ref[...]
    _gdn_conv_norm((gq_ref, gk_ref, gv_ref), xprev_ref, cw_ref, (q_ref, k_ref, v_ref), xlast_ref,
                   scratch[0] if carry else None, carry)

    chains = []
    for i in range(n_tiles):
        rows = slice(i * tile, (i + 1) * tile)
        sm = sm_ref[rows, :]
        beta_all = _sigmoid(sm)
        gc_all = _segment_cumsum(neg_a * _softplus(sm + dtb), seg)
        gl_all = _segment_last(gc_all, seg)
        for h in range(GDN_HEADS):
            cols = slice(h * dk, (h + 1) * dk)
            lane = slice(GDN_HEADS + h, GDN_HEADS + h + 1)
            chains.append((q_ref[rows, cols], k_ref[rows, cols], v_ref[rows, cols],
                           beta_all[:, h:h + 1], gc_all[:, lane], gl_all[:, lane]))
    local = _gdn_tiles_local(chains, masks)

    heads = range(GDN_HEADS)
    for i in range(n_tiles):
        rows = slice(i * tile, (i + 1) * tile)
        loc = local[i * GDN_HEADS:(i + 1) * GDN_HEADS]
        v_new = [[] for _ in heads]
        o_state = [[] for _ in heads]
        for j in range(n_seg):
            sl = slice(j * seg, (j + 1) * seg)
            seq = 0 if carry else i * n_seg + j
            s_prev = [s_ref[seq, h] if carry else s0_ref[seq, h] for h in heads]
            rs = [_dot(jnp.concatenate([loc[h]["kcd"][sl], loc[h]["qg"][sl]], axis=0), s_prev[h])
                  for h in heads]
            vns = [loc[h]["value"][sl] - rs[h][:seg] for h in heads]
            for h in heads:
                v_new[h].append(vns[h])
                o_state[h].append(rs[h][seg:])
                s_ref[seq, h] = (s_prev[h] * loc[h]["decay_last"][j * seg:j * seg + 1, :]
                                 + _dot_tn(loc[h]["kd"][sl], vns[h]))
        for h in heads:
            cols = slice(h * dk, (h + 1) * dk)
            vn = jnp.concatenate(v_new[h], axis=0) if n_seg > 1 else v_new[h][0]
            os_ = jnp.concatenate(o_state[h], axis=0) if n_seg > 1 else o_state[h][0]
            o = os_ + _dot(loc[h]["attn"], vn)
            o_ref[rows, cols] = (_rmsnorm(o, ng) * _silu(z_ref[rows, cols])).astype(o_ref.dtype)


def gdn_branch(proj, col_q, col_z, small, xprev, cw, alog_row, dtb_row, ng, s0, *, n_seq, seq_len,
               tiles_per_step):
    tokens = proj.shape[0]
    C = cw.shape[1] // 3
    dk = C // GDN_HEADS
    tile = GDN_CHUNK
    carry = seq_len >= GDN_CHUNK
    seg = GDN_CHUNK if carry else seq_len
    rows = tiles_per_step * tile
    if carry:
        nt = seq_len // rows
        seq_per_step = 1
        grid = (n_seq, nt)
    else:
        nt = 1
        seq_per_step = rows // seg
        grid = (n_seq // seq_per_step, 1)
    row_map = lambda b, t: (b * nt + t)
    tok_spec = pl.BlockSpec((rows, C), lambda b, t: (row_map(b, t), 0))
    const2 = lambda b, t: (0, 0)
    s_spec = pl.BlockSpec((seq_per_step, GDN_HEADS, dk, dk), lambda b, t: (b, 0, 0, 0))
    conv_spec, conv_shape = _conv_state_spec(carry, seq_per_step, n_seq, 3 * C)
    in_specs = [pl.BlockSpec((rows, C), lambda b, t, c=c: (row_map(b, t), c))
                for c in (col_q, col_q + 1, col_q + 2, col_z)]
    in_specs += [pl.BlockSpec((rows, LANES), lambda b, t: (row_map(b, t), 0)),
                 conv_spec, pl.BlockSpec((CONV_W, 3 * C), const2),
                 pl.BlockSpec((1, LANES), const2), pl.BlockSpec((1, LANES), const2),
                 pl.BlockSpec((1, dk), const2), s_spec]
    scratch = [pltpu.VMEM((rows, C), F32)] * 3
    if carry:
        scratch.append(pltpu.VMEM((3, rows // SUBLANES + 1, SUBLANES, C), F32))
    return pl.pallas_call(
        functools.partial(_gdn_kernel, tile=tile, seg=seg, n_tiles=tiles_per_step, carry=carry),
        grid=grid, in_specs=in_specs,
        out_specs=[tok_spec, s_spec, conv_spec],
        out_shape=[jax.ShapeDtypeStruct((tokens, C), _lhs_dtype(tile)), jax.ShapeDtypeStruct(s0.shape, F32),
                   jax.ShapeDtypeStruct(conv_shape, F32)],
        scratch_shapes=scratch,
        compiler_params=_params("parallel", "arbitrary"),
        name="gdn_branch",
    )(proj, proj, proj, proj, small, xprev, cw, alog_row, dtb_row, ng.reshape(1, dk), s0)


def _mem_attn_kernel(q_ref, k_ref, v_ref, o_ref, *, n_seq, tq):
    hd = q_ref.shape[1] // MEM_HEADS
    scale = hd ** -0.5
    rows = lambda s: slice(s * tq, (s + 1) * tq)
    cols = lambda h: slice(h * hd, (h + 1) * hd)
    if len(k_ref.shape) == 4:
        mem_rows = k_ref.shape[1] * MEM_HEADS
        r_head = lax.broadcasted_iota(jnp.int32, (MEM_HEADS * tq, mem_rows), 0) // tq
        c_head = lax.broadcasted_iota(jnp.int32, (MEM_HEADS * tq, mem_rows), 1) & (MEM_HEADS - 1)
        own = r_head == c_head
        scores = []
        for s in range(n_seq):
            q_all = jnp.concatenate([q_ref[rows(s), cols(h)] for h in range(MEM_HEADS)], axis=0)
            sc = _dot_nt(q_all, k_ref[s].reshape(mem_rows, hd)) * scale
            scores.append(jnp.where(own, sc, -jnp.inf))
        probs = []
        for sc in scores:
            e = jnp.exp(sc - jnp.max(sc, axis=-1, keepdims=True))
            probs.append(e / jnp.sum(e, axis=-1, keepdims=True))
        for s, p in enumerate(probs):
            o_all = _dot(p, v_ref[s].reshape(mem_rows, hd))
            for h in range(MEM_HEADS):
                o_ref[rows(s), cols(h)] = o_all[h * tq:(h + 1) * tq].astype(o_ref.dtype)
        return
    pairs = [(s, h) for s in range(n_seq) for h in range(MEM_HEADS)]
    mem = lambda ref, s, h: ref[s, :, cols(h)]
    scores = [_dot_nt(q_ref[rows(s), cols(h)], mem(k_ref, s, h)) * scale for s, h in pairs]
    probs = []
    for sc in scores:
        e = jnp.exp(sc - jnp.max(sc, axis=-1, keepdims=True))
        probs.append(e / jnp.sum(e, axis=-1, keepdims=True))
    for (s, h), p in zip(pairs, probs):
        o_ref[rows(s), cols(h)] = _dot(p, mem(v_ref, s, h)).astype(o_ref.dtype)


def mem_attention(proj, col_q, mem_k, col_k, mem_v, col_v, *, width, n_seq, seq_len, tq, seq_per_step):
    tokens = n_seq * seq_len
    mem_len = mem_k.shape[1]
    C = width
    nt = seq_len // tq
    rows = seq_per_step * tq
    grid = (n_seq // seq_per_step, nt)
    row_map = lambda b, t: (b * nt + t)
    if mem_k.ndim == 4:
        blk = (seq_per_step,) + mem_k.shape[1:]
        k_spec = v_spec = pl.BlockSpec(blk, lambda b, t: (b, 0, 0, 0))
    else:
        k_spec = pl.BlockSpec((seq_per_step, mem_len, C), lambda b, t: (b, 0, col_k))
        v_spec = pl.BlockSpec((seq_per_step, mem_len, C), lambda b, t: (b, 0, col_v))
    return pl.pallas_call(
        functools.partial(_mem_attn_kernel, n_seq=seq_per_step, tq=tq),
        grid=grid,
        in_specs=[pl.BlockSpec((rows, C), lambda b, t: (row_map(b, t), col_q)), k_spec, v_spec],
        out_specs=pl.BlockSpec((rows, C), lambda b, t: (row_map(b, t), 0)),
        out_shape=jax.ShapeDtypeStruct((tokens, C), _lhs_dtype(tq)),
        compiler_params=_params("parallel", "arbitrary"),
        name="mem_attention",
    )(proj, mem_k, mem_v)


def _merge_kernel(yr_ref, yg_ref, ym_ref, g0_ref, g1_ref, g2_ref, w0_ref, w1_ref, w2_ref, o_ref):
    acc = _sigmoid(g0_ref[...]) * _dot(yr_ref[...], w0_ref[...])
    acc = acc + _sigmoid(g1_ref[...]) * _dot(yg_ref[...], w1_ref[...])
    acc = acc + _sigmoid(g2_ref[...]) * _dot(ym_ref[...], w2_ref[...])
    o_ref[...] = acc.astype(BF16)


def merge_branches(y_rnn, y_gdn, y_mem, proj, col_gate, w_rnn_up, w_gdn_up, w_mem_up, *, tm, tn):
    M, C = y_rnn.shape
    N = w_rnn_up.shape[1]
    nj = N // tn
    y_spec = pl.BlockSpec((tm, C), lambda i, j: (i, 0))
    w_spec = pl.BlockSpec((C, tn), lambda i, j: (0, j))
    gate_specs = [pl.BlockSpec((tm, tn), lambda i, j, b=b: (i, col_gate + b * nj + j)) for b in range(N_BRANCH)]
    return pl.pallas_call(
        _merge_kernel,
        grid=(M // tm, nj),
        in_specs=[y_spec, y_spec, y_spec] + gate_specs + [w_spec, w_spec, w_spec],
        out_specs=pl.BlockSpec((tm, tn), lambda i, j: (i, j)),
        out_shape=jax.ShapeDtypeStruct((M, N), BF16),
        compiler_params=_params("parallel", "arbitrary"),
        name="merge_branches",
    )(y_rnn, y_gdn, y_mem, proj, proj, proj, w_rnn_up, w_gdn_up, w_mem_up)


def _matmul_residual_kernel(a_ref, w_ref, x_ref, o_ref):
    o_ref[...] = x_ref[...] + _dot(a_ref[...], w_ref[...])


def matmul_residual(a, w, x, *, tm, tn):
    M, K = a.shape
    N = w.shape[1]
    return pl.pallas_call(
        _matmul_residual_kernel,
        grid=(M // tm, N // tn),
        in_specs=[pl.BlockSpec((tm, K), lambda i, j: (i, 0)),
                  pl.BlockSpec((K, tn), lambda i, j: (0, j)),
                  pl.BlockSpec((tm, tn), lambda i, j: (i, j))],
        out_specs=pl.BlockSpec((tm, tn), lambda i, j: (i, j)),
        out_shape=jax.ShapeDtypeStruct((M, N), F32),
        compiler_params=_params("parallel", "arbitrary"),
        name="matmul_residual",
    )(a, w, x)


def _mlp_kernel(x_hbm, g_ref, wu_ref, wd_ref, gf_ref, o_ref, xn_ref, x_buf, sem):
    def consume(x_ref):
        xn_ref[...] = _rmsnorm(x_ref[...], g_ref[...]).astype(BF16)
        o_ref[...] = x_ref[...]

    _row_tile_prefetch(x_hbm, x_buf, sem, consume)
    hid = _dot(xn_ref[...], wu_ref[...])
    act = jnp.square(jnp.maximum(hid, 0.0))
    o_ref[...] += _dot(act, wd_ref[...])

    @pl.when(pl.program_id(1) == pl.num_programs(1) - 1)
    def _():
        o_ref[...] = _rmsnorm(o_ref[...], gf_ref[...])


def mlp_final_norm(x, g, w_up, w_down, g_final, *, tm, tf):
    M, D = x.shape
    FF = w_up.shape[1]
    assert FF // tf >= 2
    return pl.pallas_call(
        _mlp_kernel,
        grid=(M // tm, FF // tf),
        in_specs=[pl.BlockSpec(memory_space=pl.ANY),
                  pl.BlockSpec((1, D), lambda i, j: (0, 0)),
                  pl.BlockSpec((D, tf), lambda i, j: (0, j)),
                  pl.BlockSpec((tf, D), lambda i, j: (j, 0)),
                  pl.BlockSpec((1, D), lambda i, j: (0, 0))],
        out_specs=pl.BlockSpec((tm, D), lambda i, j: (i, 0)),
        out_shape=jax.ShapeDtypeStruct((M, D), F32),
        scratch_shapes=[pltpu.VMEM((tm, D), BF16), pltpu.VMEM((tm, D), F32), pltpu.SemaphoreType.DMA(())],
        compiler_params=_params("arbitrary", "arbitrary"),
        name="mlp_final_norm",
    )(x, g.reshape(1, D), w_up, w_down, g_final.reshape(1, D))


def _in_proj_segments(d_model):
    half = d_model // 2
    small_row = 6 * half
    mq_row = small_row + 2 * GDN_HEADS
    mg_row = mq_row + half
    return ((mg_row, N_BRANCH * d_model), (0, 6 * half), (mq_row, half)), small_row


def _prep_layer_weights(rnn_wx, rnn_wa, gdn_A_log, gdn_dt_bias):
    w_gate = jnp.concatenate([rnn_wx, rnn_wa], axis=-1).astype(BF16)
    lane_pad = (GDN_HEADS, LANES - 2 * GDN_HEADS)
    alog_row = jnp.pad(gdn_A_log, lane_pad).reshape(1, LANES)
    dtb_row = jnp.pad(gdn_dt_bias, lane_pad).reshape(1, LANES)
    return w_gate, alog_row, dtb_row


def _conv_state_in(buf, carry):
    if carry:
        return jnp.pad(buf, ((0, 0), (SUBLANES - (CONV_W - 1), 0), (0, 0)))
    return jnp.swapaxes(buf, 0, 1)


def _conv_state_out(state, carry):
    return state[:, SUBLANES - (CONV_W - 1):] if carry else jnp.swapaxes(state, 0, 1)


def _group_layer(x, mem_k, col_k, mem_v, col_v, rnn_buf, rnn_h0, gdn_buf, gdn_s0, lw, *, n_seq, seq_len,
                 tm, proj_tm, row_tile, attn_tq, attn_seqs, gdn_tiles, final_g):
    D = x.shape[1]
    half = D // 2
    n_gate_blk = N_BRANCH * D // half
    c_rx, c_rg, c_gq, c_gz, c_mq = (n_gate_blk + i for i in (0, 1, 2, 5, 6))
    carry = seq_len > SUBLANES
    assert carry == (seq_len >= GDN_CHUNK)
    segments, small_row = _in_proj_segments(D)
    proj, small = in_projection(x, lw["norm_mix_g"], lw["w_in_t"], segments, small_row, tm=proj_tm,
                                tn=512 if proj_tm > 1024 else 1024)

    y_rnn, rnn_last, h_last = rglru_branch(
        proj, c_rx, c_rg, _conv_state_in(rnn_buf, carry), rnn_h0.reshape(n_seq, 1, half),
        lw["rnn_conv_w"], lw["rnn_conv_b"], lw["w_gate"], lw["rnn_bx"], lw["rnn_ba"], lw["rnn_L"],
        n_seq=n_seq, seq_len=seq_len, rows=row_tile)

    y_gdn, s_new, gdn_last = gdn_branch(
        proj, c_gq, c_gz, small, _conv_state_in(gdn_buf, carry), lw["gdn_conv_w"], lw["alog_row"],
        lw["dtb_row"],
        lw["gdn_norm_g"], gdn_s0, n_seq=n_seq, seq_len=seq_len, tiles_per_step=gdn_tiles)

    y_mem = mem_attention(proj, c_mq, mem_k, col_k, mem_v, col_v, width=half, n_seq=n_seq, seq_len=seq_len,
                          tq=attn_tq, seq_per_step=attn_seqs)

    merged = merge_branches(y_rnn, y_gdn, y_mem, proj, 0, lw["w_rnn_up"], lw["w_gdn_up"], lw["w_mem_up"],
                            tm=proj_tm, tn=256)
    x1 = matmul_residual(merged, lw["w_out"], x, tm=proj_tm, tn=512)
    x2 = mlp_final_norm(x1, lw["norm_mlp_g"], lw["w_mlp_up"], lw["w_mlp_down"], final_g, tm=tm, tf=512)
    states = (_conv_state_out(rnn_last, carry), h_last.reshape(n_seq, half),
              _conv_state_out(gdn_last, carry), s_new)
    return x2, states


def kernel(x_prompt, x_sample, mem_prompt, cache_mem_k, cache_mem_v, state_rnn_conv, state_rnn_h,
           state_gdn_conv, state_gdn_S, norm_mix_g, w_in, rnn_conv_w, rnn_conv_b, rnn_wx, rnn_bx, rnn_wa,
           rnn_ba, rnn_L, gdn_conv_w, gdn_A_log, gdn_dt_bias, gdn_norm_g, mem_norm_g, w_mem_kv, w_rnn_up,
           w_gdn_up, w_mem_up, w_out, norm_mlp_g, w_mlp_up, w_mlp_down, norm_final_g):
    depth = w_in.shape[0]
    assert depth == 1, "the final norm is fused into the last layer's MLP kernel; one layer supported"
    Bp, T, D = x_prompt.shape
    Bs, Ts, _ = x_sample.shape
    half = D // 2
    mem_len = mem_prompt.shape[1]
    assert Ts == SUBLANES and T % GDN_CHUNK == 0
    l = 0
    w_gate, alog_row, dtb_row = _prep_layer_weights(rnn_wx[l], rnn_wa[l], gdn_A_log[l], gdn_dt_bias[l])
    lw = dict(norm_mix_g=norm_mix_g[l], w_in_t=jnp.swapaxes(w_in[l], 0, 1), w_gate=w_gate,
              rnn_conv_w=rnn_conv_w[l], rnn_conv_b=rnn_conv_b[l], rnn_bx=rnn_bx[l], rnn_ba=rnn_ba[l],
              rnn_L=rnn_L[l], gdn_conv_w=gdn_conv_w[l], alog_row=alog_row, dtb_row=dtb_row,
              gdn_norm_g=gdn_norm_g[l], w_rnn_up=w_rnn_up[l], w_gdn_up=w_gdn_up[l],
              w_mem_up=w_mem_up[l], w_out=w_out[l], norm_mlp_g=norm_mlp_g[l],
              w_mlp_up=w_mlp_up[l], w_mlp_down=w_mlp_down[l])

    kv = norm_matmul(mem_prompt.reshape(Bp * mem_len, D), mem_norm_g[l], w_mem_kv[l],
                     tm=min(Bp * mem_len, 1024), tn=512)
    kv3 = kv.reshape(Bp, mem_len, 2 * half)
    zeros = lambda *s: jnp.zeros(s, F32)
    yp, (rb_p, rh_p, gb_p, gs_p) = _group_layer(
        x_prompt.reshape(Bp * T, D), kv3, 0, kv3, 1,
        zeros(Bp, CONV_W - 1, half), zeros(Bp, half), zeros(Bp, CONV_W - 1, 3 * half),
        zeros(Bp, GDN_HEADS, half // GDN_HEADS, half // GDN_HEADS), lw,
        n_seq=Bp, seq_len=T, tm=min(Bp * T, 1024), proj_tm=min(Bp * T, 2048), row_tile=512, attn_tq=min(T, 1024), attn_seqs=1,
        gdn_tiles=min(4, T // GDN_CHUNK),
        final_g=norm_final_g)
    mk_p = kv3[:, :, :half].reshape(1, Bp, mem_len, MEM_HEADS, half // MEM_HEADS)
    mv_p = kv3[:, :, half:].reshape(1, Bp, mem_len, MEM_HEADS, half // MEM_HEADS)

    ys, (rb_s, rh_s, gb_s, gs_s) = _group_layer(
        x_sample.reshape(Bs * Ts, D), cache_mem_k.reshape((depth * Bs,) + cache_mem_k.shape[2:]), 0,
        cache_mem_v.reshape((depth * Bs,) + cache_mem_v.shape[2:]), 0,
        state_rnn_conv[l], state_rnn_h[l], state_gdn_conv[l], state_gdn_S[l], lw,
        n_seq=Bs, seq_len=Ts, tm=min(Bs * Ts, 1024), proj_tm=min(Bs * Ts, 1024), row_tile=min(Bs, 16) * SUBLANES, attn_tq=Ts,
        attn_seqs=min(Bs, 8), gdn_tiles=max(1, min(2, Bs * Ts // GDN_CHUNK)), final_g=norm_final_g)

    return (yp.reshape(Bp, T, D), ys.reshape(Bs, Ts, D), mk_p, mv_p, rb_p[None], rh_p[None], gb_p[None],
            gs_p[None], rb_s[None], rh_s[None], gb_s[None], gs_s[None])
```

```python
import functools

import jax
import jax.numpy as jnp
from jax import lax
from jax.experimental import pallas as pl
from jax.experimental.pallas import tpu as pltpu

F32 = jnp.float32
BF16 = jnp.bfloat16

EPS = 1e-6
RG_C = 8.0
CONV_W = 4
RNN_BLOCKS = 8
GDN_HEADS = 8
GDN_CHUNK = 64
MEM_HEADS = 4
N_BRANCH = 3

SUBLANES = 8
BF16_SUBLANES = 16
LANES = 128
VMEM_LIMIT_BYTES = 56 * 1024 * 1024


def _lhs_dtype(slab_rows):
    return BF16 if slab_rows % BF16_SUBLANES == 0 else F32


def _params(*sem):
    return pltpu.CompilerParams(dimension_semantics=sem, vmem_limit_bytes=VMEM_LIMIT_BYTES)


def _dot(a, b):
    return jnp.dot(a.astype(BF16), b.astype(BF16), preferred_element_type=F32)


def _dot_nt(a, b):
    return lax.dot_general(a.astype(BF16), b.astype(BF16), (((1,), (1,)), ((), ())),
                           preferred_element_type=F32)


def _dot_tn(a, b):
    return lax.dot_general(a.astype(BF16), b.astype(BF16), (((0,), (0,)), ((), ())),
                           preferred_element_type=F32)


def _rmsnorm(x, g):
    return (x * lax.rsqrt(jnp.mean(x * x, axis=-1, keepdims=True) + EPS)) * g


def _softplus(x):
    return jnp.maximum(x, 0.0) + jnp.log1p(jnp.exp(-jnp.abs(x)))


_sigmoid = jax.nn.sigmoid


def _silu(x):
    return x * _sigmoid(x)


def _memory_kv_kernel(x_ref, g_ref, w_ref, kv_ref, k4_ref, v4_ref, xn_ref):
    j = pl.program_id(1)

    @pl.when(j == 0)
    def _():
        xn_ref[...] = _rmsnorm(x_ref[...], g_ref[...]).astype(BF16)

    acc = _dot(xn_ref[...], w_ref[...])
    kv_ref[...] = acc
    split = acc.reshape(k4_ref.shape)

    @pl.when(j == 0)
    def _():
        k4_ref[...] = split

    @pl.when(j == 1)
    def _():
        v4_ref[...] = split


def memory_kv(x, g, w, *, tm):
    M, K = x.shape
    C = w.shape[1] // 2
    hd = C // MEM_HEADS
    head_spec = pl.BlockSpec((tm, MEM_HEADS, hd), lambda i, j: (i, 0, 0))
    return pl.pallas_call(
        _memory_kv_kernel,
        grid=(M // tm, 2),
        in_specs=[pl.BlockSpec((tm, K), lambda i, j: (i, 0)),
                  pl.BlockSpec((1, K), lambda i, j: (0, 0)),
                  pl.BlockSpec((K, C), lambda i, j: (0, j))],
        out_specs=[pl.BlockSpec((tm, C), lambda i, j: (i, j)), head_spec, head_spec],
        out_shape=[jax.ShapeDtypeStruct((M, 2 * C), F32)] + [jax.ShapeDtypeStruct((M, MEM_HEADS, hd), F32)] * 2,
        scratch_shapes=[pltpu.VMEM((tm, K), BF16)],
        compiler_params=_params("parallel", "arbitrary"),
        name="memory_kv",
    )(x, g.reshape(1, K), w)


def _row_tile_prefetch(x_hbm, x_buf, sem, consume):
    i, j = pl.program_id(0), pl.program_id(1)
    tm = x_buf.shape[0]

    def copy(tile):
        return pltpu.make_async_copy(x_hbm.at[pl.ds(pl.multiple_of(tile * tm, tm), tm), :], x_buf, sem)

    @pl.when((i == 0) & (j == 0))
    def _():
        copy(0).start()

    @pl.when(j == 0)
    def _():
        copy(i).wait()
        consume(x_buf)

    @pl.when((j == 1) & (i + 1 < pl.num_programs(0)))
    def _():
        copy(i + 1).start()


def _in_proj_kernel(x_hbm, g_ref, wt_ref, wst_ref, o_ref, os_ref, xn_ref, x_buf, sem):
    def consume(x_ref):
        xn_ref[...] = _rmsnorm(x_ref[...], g_ref[...]).astype(BF16)
        os_ref[...] = _dot_nt(xn_ref[...], wst_ref[...])

    _row_tile_prefetch(x_hbm, x_buf, sem, consume)
    o_ref[...] = _dot_nt(xn_ref[...], wt_ref[...])


def in_projection(x, g, w_t, segments, small_row, *, tm, tn):
    M, K = x.shape
    n_tiles = [n // tn for _, n in segments]
    assert all(n % tn == 0 and r % SUBLANES == 0 for r, n in segments) and small_row % LANES == 0
    N = tn * sum(n_tiles)

    def w_row(j):
        row, first = None, 0
        for (r0, _), nt in zip(segments, n_tiles):
            cand = r0 // SUBLANES + (tn // SUBLANES) * (j - first)
            row = cand if row is None else jnp.where(j >= first, cand, row)
            first += nt
        return row * SUBLANES

    assert N // tn >= 2
    outs = pl.pallas_call(
        _in_proj_kernel,
        grid=(M // tm, N // tn),
        in_specs=[pl.BlockSpec(memory_space=pl.ANY),
                  pl.BlockSpec((1, K), lambda i, j: (0, 0)),
                  pl.BlockSpec((pl.Element(tn), pl.Element(K)), lambda i, j: (w_row(j), 0)),
                  pl.BlockSpec((LANES, K), lambda i, j: (small_row // LANES, 0))],
        out_specs=[pl.BlockSpec((tm, tn), lambda i, j: (i, j)), pl.BlockSpec((tm, LANES), lambda i, j: (i, 0))],
        out_shape=[jax.ShapeDtypeStruct((M, N), F32), jax.ShapeDtypeStruct((M, LANES), F32)],
        scratch_shapes=[pltpu.VMEM((tm, K), BF16), pltpu.VMEM((tm, K), F32), pltpu.SemaphoreType.DMA(())],
        compiler_params=_params("arbitrary", "arbitrary"),
        name="in_projection",
    )(x, g.reshape(1, K), w_t, w_t)
    return outs


def _causal_conv(x, xprev, w_ref):
    t_idx = lax.broadcasted_iota(jnp.int32, x.shape, 1)
    y = x * w_ref[CONV_W - 1:CONV_W, :][None]
    for k in range(1, CONV_W):
        shifted = jnp.where(t_idx >= k, pltpu.roll(x, k, 1), pltpu.roll(xprev, k, 1))
        y = y + shifted * w_ref[CONV_W - 1 - k:CONV_W - k, :][None]
    return y


def _segment_scan(a, u):
    t_idx = lax.broadcasted_iota(jnp.int32, a.shape, 1)
    s = 1
    while s < SUBLANES:
        keep = t_idx >= s
        u = jnp.where(keep, a * pltpu.roll(u, s, 1) + u, u)
        a = jnp.where(keep, a * pltpu.roll(a, s, 1), a)
        s *= 2
    return a, u


def _state_group(state_ref, cols=slice(None)):
    rows, nb = state_ref.shape[0], state_ref.shape[1]
    first = [state_ref[r, :, cols] for r in range(rows)]
    t_idx = lax.broadcasted_iota(jnp.int32, (nb, SUBLANES, first[0].shape[-1]), 1)
    y = jnp.zeros(t_idx.shape, F32)
    for r, row in enumerate(first):
        y = jnp.where(t_idx == SUBLANES - rows + r, row[:, None, :], y)
    return y


def _store_state(state_ref, x, cols=slice(None)):
    rows = state_ref.shape[0]
    for r in range(rows):
        state_ref[r, :, cols] = x[:, SUBLANES - rows + r, :]


def _conv_state_spec(carry, nb, n_seq, width):
    if carry:
        return pl.BlockSpec((nb, SUBLANES, width), lambda b, t: (b, 0, 0)), (n_seq, SUBLANES, width)
    return pl.BlockSpec((CONV_W - 1, nb, width), lambda b, t: (0, b, 0)), (CONV_W - 1, n_seq, width)


def _stage_groups(x, xprev_ref, xs_ref, first):
    G = x.shape[0]

    @pl.when(first)
    def _():
        xs_ref[G] = xprev_ref[0]

    xs_ref[0] = xs_ref[G]
    xs_ref[1:G + 1] = x
    return xs_ref[0:G]


def _rglru_kernel(rx_ref, rg_ref, xprev_ref, h0_ref, cw_ref, cb_ref, wg_ref, bx_ref, ba_ref, l_ref,
                  y_ref, xlast_ref, hlast_ref, *scratch, carry):
    R, C = rx_ref.shape
    G = R // SUBLANES
    x = rx_ref[...].reshape(G, SUBLANES, C)
    if carry:
        xs_ref, a_ref, u_ref, h_ref, hc_ref = scratch
        first = pl.program_id(1) == 0
        xprev = _stage_groups(x, xprev_ref, xs_ref, first)

        @pl.when(first)
        def _():
            hc_ref[...] = h0_ref[0]
    else:
        xprev = _state_group(xprev_ref)

    xc = (_causal_conv(x, xprev, cw_ref) + cb_ref[...][None]).reshape(R, C)
    xb = xc.astype(BF16)
    bs = C // RNN_BLOCKS
    zi, zr = [], []
    for n in range(RNN_BLOCKS):
        z = jnp.dot(xb[:, n * bs:(n + 1) * bs], wg_ref[n], preferred_element_type=F32)
        zi.append(z[:, :bs])
        zr.append(z[:, bs:])
    gi = _sigmoid(jnp.concatenate(zi, axis=1) + bx_ref[...])
    gr = _sigmoid(jnp.concatenate(zr, axis=1) + ba_ref[...])
    lv = l_ref[...]
    log_sig_l = -_softplus(-lv)
    log_a = RG_C * gr * log_sig_l
    a = jnp.exp(log_a)
    u = jnp.sqrt(-jnp.tanh(log_a) * (a * a + 1.0)) * (gi * xc)
    a_cum, h_loc = _segment_scan(a.reshape(G, SUBLANES, C), u.reshape(G, SUBLANES, C))

    if carry:
        a_ref[...] = a_cum
        u_ref[...] = h_loc

        def body(g, h_prev):
            hg = u_ref[g] + a_ref[g] * h_prev
            h_ref[g] = hg
            return hg[SUBLANES - 1:SUBLANES, :]

        h_last = lax.fori_loop(0, G, body, hc_ref[...])
        hc_ref[...] = h_last
        h = h_ref[...]
        xlast_ref[0] = x[G - 1]
        hlast_ref[0] = h_last
    else:
        h = h_loc + a_cum * h0_ref[...]
        _store_state(xlast_ref, x)
        hlast_ref[...] = h[:, SUBLANES - 1:SUBLANES, :]

    y = h.reshape(R, C) * jax.nn.gelu(rg_ref[...])
    y_ref[...] = y.astype(BF16)


def rglru_branch(proj, col_rx, col_rg, xprev, h0, cw, cb, wg, bx, ba, lam, *, n_seq, seq_len, rows):
    C = cw.shape[1]
    carry = seq_len > SUBLANES
    if carry:
        nt = seq_len // rows
        grid = (n_seq, nt)
        row_map = lambda b, t: (b * nt + t)
        nb = 1
        G = rows // SUBLANES
        scratch = [pltpu.VMEM((G + 1, SUBLANES, C), F32), pltpu.VMEM((G, SUBLANES, C), F32),
                   pltpu.VMEM((G, SUBLANES, C), F32), pltpu.VMEM((G, SUBLANES, C), F32),
                   pltpu.VMEM((1, C), F32)]
    else:
        nb = rows // SUBLANES
        grid = (n_seq // nb, 1)
        row_map = lambda b, t: b
        scratch = []
    conv_spec, conv_shape = _conv_state_spec(carry, nb, n_seq, C)
    const2 = lambda b, t: (0, 0)
    in_specs = [pl.BlockSpec((rows, C), lambda b, t: (row_map(b, t), col_rx)),
                pl.BlockSpec((rows, C), lambda b, t: (row_map(b, t), col_rg)),
                conv_spec,
                pl.BlockSpec((nb, 1, C), lambda b, t: (b, 0, 0)),
                pl.BlockSpec((CONV_W, C), const2),
                pl.BlockSpec((1, C), const2),
                pl.BlockSpec(wg.shape, lambda b, t: (0, 0, 0)),
                pl.BlockSpec((1, C), const2), pl.BlockSpec((1, C), const2), pl.BlockSpec((1, C), const2)]
    tokens = n_seq * seq_len
    out_shape = [jax.ShapeDtypeStruct((tokens, C), BF16),
                 jax.ShapeDtypeStruct(conv_shape, F32),
                 jax.ShapeDtypeStruct((n_seq, 1, C), F32)]
    out_specs = [pl.BlockSpec((rows, C), lambda b, t: (row_map(b, t), 0)),
                 conv_spec,
                 pl.BlockSpec((nb, 1, C), lambda b, t: (b, 0, 0))]
    return pl.pallas_call(
        functools.partial(_rglru_kernel, carry=carry),
        grid=grid, in_specs=in_specs, out_specs=out_specs, out_shape=out_shape,
        scratch_shapes=scratch,
        compiler_params=_params("parallel", "arbitrary"),
        name="rglru",
    )(proj, proj, xprev, h0, cw, cb.reshape(1, C), wg, bx.reshape(1, C), ba.reshape(1, C),
      lam.reshape(1, C))


def _l2norm_heads(x, scale):
    dh = x.shape[1] // GDN_HEADS
    outs = []
    for h in range(GDN_HEADS):
        xh = x[:, h * dh:(h + 1) * dh]
        xh = xh * lax.rsqrt(jnp.sum(xh * xh, axis=-1, keepdims=True) + EPS)
        outs.append(xh * scale if scale != 1.0 else xh)
    return jnp.concatenate(outs, axis=1)


def _gdn_conv_norm(src_refs, xprev_ref, cw_ref, dst_refs, xlast_ref, xs_ref, carry):
    R, C = src_refs[0].shape
    G = R // SUBLANES
    dk = C // GDN_HEADS
    for s, (src, dst) in enumerate(zip(src_refs, dst_refs)):
        x = src[...].reshape(G, SUBLANES, C)
        cols = slice(s * C, (s + 1) * C)
        if carry:
            xprev = _stage_groups(x, xprev_ref.at[:, :, cols], xs_ref.at[s], pl.program_id(1) == 0)
            xlast_ref[0, :, cols] = x[G - 1]
        else:
            xprev = _state_group(xprev_ref, cols)
            _store_state(xlast_ref, x, cols)
        y = _silu(_causal_conv(x, xprev, cw_ref.at[:, cols])).reshape(R, C)
        if s == 0:
            y = _l2norm_heads(y, dk ** -0.5)
        elif s == 1:
            y = _l2norm_heads(y, 1.0)
        dst[...] = y


def _segment_cumsum(x, seg):
    pos = lax.broadcasted_iota(jnp.int32, x.shape, 0) & (seg - 1)
    s = 1
    while s < seg:
        x = x + jnp.where(pos >= s, pltpu.roll(x, s, 0), 0.0)
        s *= 2
    return x


def _segment_last(x, seg):
    n = x.shape[0]
    pos = lax.broadcasted_iota(jnp.int32, x.shape, 0) & (seg - 1)
    s = seg // 2
    while s >= 1:
        x = jnp.where((pos & (2 * s - 1)) < s, pltpu.roll(x, n - s, 0), x)
        s //= 2
    return x


class _TileMasks:
    def __init__(self, rows, seg):
        r = lax.broadcasted_iota(jnp.int32, (rows, rows), 0)
        c = lax.broadcasted_iota(jnp.int32, (rows, rows), 1)
        shift = seg.bit_length() - 1
        same = (r >> shift) == (c >> shift)
        self.incl = same & (r >= c)
        self.strict = same & (r > c)
        self.eye = r == c
        self.levels = []
        s = 1
        while s < seg:
            b = s.bit_length() - 1
            self.levels.append(((r >> (b + 1)) == (c >> (b + 1))) & (((r >> b) & 1) == 1) & (((c >> b) & 1) == 0))
            s *= 2


def _unit_lower_inverses(ms, masks):
    eye = jnp.where(masks.eye, 1.0, 0.0)
    xs = [eye - jnp.where(masks.levels[0], m, 0.0) for m in ms]
    for level in masks.levels[1:]:
        ts = [_dot(jnp.where(level, m, 0.0), x) for m, x in zip(ms, xs)]
        xs = [x - _dot(x, t) for x, t in zip(xs, ts)]
    return xs


def _gdn_tiles_local(chains, masks):
    r, d = chains[0][0].shape
    kbs, decays, kk_qks = [], [], []
    for q, k, v, beta, gc, g_last in chains:
        gc_row = jnp.sum(jnp.where(masks.eye, gc, 0.0), axis=0, keepdims=True)
        decays.append(jnp.where(masks.incl, jnp.exp(gc - gc_row), 0.0))
        kbs.append(k * beta)
        kk_qks.append(_dot_nt(jnp.concatenate([kbs[-1], q], axis=0), k))
    ms = [jnp.where(masks.strict, kq[:r] * dec, 0.0) for kq, dec in zip(kk_qks, decays)]
    tms = _unit_lower_inverses(ms, masks)
    out = []
    for (q, k, v, beta, gc, g_last), kb, dec, kq, tm in zip(chains, kbs, decays, kk_qks, tms):
        eg = jnp.exp(gc)
        vk = _dot(tm, jnp.concatenate([v * beta, kb * eg], axis=1))
        out.append(dict(value=vk[:, :d], kcd=vk[:, d:], attn=kq[r:] * dec, qg=q * eg,
                        kd=k * jnp.exp(g_last - gc), decay_last=jnp.exp(g_last)))
    return out


def _gdn_kernel(gq_ref, gk_ref, gv_ref, z_ref, sm_ref, xprev_ref, cw_ref, alog_ref, dtb_ref, ng_ref, s0_ref,
                o_ref, s_ref, xlast_ref, q_ref, k_ref, v_ref, *scratch, tile, seg, n_tiles, carry):
    dk = q_ref.shape[1] // GDN_HEADS
    n_seg = tile // seg
    masks = _TileMasks(tile, seg)
    neg_a = -jnp.exp(alog_ref[...])
    dtb = dtb_ref[...]
    ng = ng_ref[...]
    if carry:
        @pl.when(pl.program_id(1) == 0)
        def _():
            s_ref[...] = s0_ref[...]
    _gdn_conv_norm((gq_ref, gk_ref, gv_ref), xprev_ref, cw_ref, (q_ref, k_ref, v_ref), xlast_ref,
                   scratch[0] if carry else None, carry)

    chains = []
    for i in range(n_tiles):
        rows = slice(i * tile, (i + 1) * tile)
        sm = sm_ref[rows, :]
        beta_all = _sigmoid(sm)
        gc_all = _segment_cumsum(neg_a * _softplus(sm + dtb), seg)
        gl_all = _segment_last(gc_all, seg)
        for h in range(GDN_HEADS):
            cols = slice(h * dk, (h + 1) * dk)
            lane = slice(GDN_HEADS + h, GDN_HEADS + h + 1)
            chains.append((q_ref[rows, cols], k_ref[rows, cols], v_ref[rows, cols],
                           beta_all[:, h:h + 1], gc_all[:, lane], gl_all[:, lane]))
    local = _gdn_tiles_local(chains, masks)

    heads = range(GDN_HEADS)
    for i in range(n_tiles):
        rows = slice(i * tile, (i + 1) * tile)
        loc = local[i * GDN_HEADS:(i + 1) * GDN_HEADS]
        v_new = [[] for _ in heads]
        o_state = [[] for _ in heads]
        for j in range(n_seg):
            sl = slice(j * seg, (j + 1) * seg)
            seq = 0 if carry else i * n_seg + j
            s_prev = [s_ref[seq, h] if carry else s0_ref[seq, h] for h in heads]
            rs = [_dot(jnp.concatenate([loc[h]["kcd"][sl], loc[h]["qg"][sl]], axis=0), s_prev[h])
                  for h in heads]
            vns = [loc[h]["value"][sl] - rs[h][:seg] for h in heads]
            for h in heads:
                v_new[h].append(vns[h])
                o_state[h].append(rs[h][seg:])
                s_ref[seq, h] = (s_prev[h] * loc[h]["decay_last"][j * seg:j * seg + 1, :]
                                 + _dot_tn(loc[h]["kd"][sl], vns[h]))
        for h in heads:
            cols = slice(h * dk, (h + 1) * dk)
            vn = jnp.concatenate(v_new[h], axis=0) if n_seg > 1 else v_new[h][0]
            os_ = jnp.concatenate(o_state[h], axis=0) if n_seg > 1 else o_state[h][0]
            o = os_ + _dot(loc[h]["attn"], vn)
            o_ref[rows, cols] = (_rmsnorm(o, ng) * _silu(z_ref[rows, cols])).astype(o_ref.dtype)


def gdn_branch(proj, col_q, col_z, small, xprev, cw, alog_row, dtb_row, ng, s0, *, n_seq, seq_len,
               tiles_per_step):
    tokens = proj.shape[0]
    C = cw.shape[1] // 3
    dk = C // GDN_HEADS
    tile = GDN_CHUNK
    carry = seq_len >= GDN_CHUNK
    seg = GDN_CHUNK if carry else seq_len
    rows = tiles_per_step * tile
    if carry:
        nt = seq_len // rows
        seq_per_step = 1
        grid = (n_seq, nt)
    else:
        nt = 1
        seq_per_step = rows // seg
        grid = (n_seq // seq_per_step, 1)
    row_map = lambda b, t: (b * nt + t)
    tok_spec = pl.BlockSpec((rows, C), lambda b, t: (row_map(b, t), 0))
    const2 = lambda b, t: (0, 0)
    s_spec = pl.BlockSpec((seq_per_step, GDN_HEADS, dk, dk), lambda b, t: (b, 0, 0, 0))
    conv_spec, conv_shape = _conv_state_spec(carry, seq_per_step, n_seq, 3 * C)
    in_specs = [pl.BlockSpec((rows, C), lambda b, t, c=c: (row_map(b, t), c))
                for c in (col_q, col_q + 1, col_q + 2, col_z)]
    in_specs += [pl.BlockSpec((rows, LANES), lambda b, t: (row_map(b, t), 0)),
                 conv_spec, pl.BlockSpec((CONV_W, 3 * C), const2),
                 pl.BlockSpec((1, LANES), const2), pl.BlockSpec((1, LANES), const2),
                 pl.BlockSpec((1, dk), const2), s_spec]
    scratch = [pltpu.VMEM((rows, C), F32)] * 3
    if carry:
        scratch.append(pltpu.VMEM((3, rows // SUBLANES + 1, SUBLANES, C), F32))
    return pl.pallas_call(
        functools.partial(_gdn_kernel, tile=tile, seg=seg, n_tiles=tiles_per_step, carry=carry),
        grid=grid, in_specs=in_specs,
        out_specs=[tok_spec, s_spec, conv_spec],
        out_shape=[jax.ShapeDtypeStruct((tokens, C), _lhs_dtype(tile)), jax.ShapeDtypeStruct(s0.shape, F32),
                   jax.ShapeDtypeStruct(conv_shape, F32)],
        scratch_shapes=scratch,
        compiler_params=_params("parallel", "arbitrary"),
        name="gdn_branch",
    )(proj, proj, proj, proj, small, xprev, cw, alog_row, dtb_row, ng.reshape(1, dk), s0)


def _mem_attn_kernel(q_ref, k_ref, v_ref, o_ref, *, n_seq, tq):
    hd = q_ref.shape[1] // MEM_HEADS
    scale = hd ** -0.5
    rows = lambda s: slice(s * tq, (s + 1) * tq)
    cols = lambda h: slice(h * hd, (h + 1) * hd)
    if len(k_ref.shape) == 4:
        mem_rows = k_ref.shape[1] * MEM_HEADS
        r_head = lax.broadcasted_iota(jnp.int32, (MEM_HEADS * tq, mem_rows), 0) // tq
        c_head = lax.broadcasted_iota(jnp.int32, (MEM_HEADS * tq, mem_rows), 1) & (MEM_HEADS - 1)
        own = r_head == c_head
        scores = []
        for s in range(n_seq):
            q_all = jnp.concatenate([q_ref[rows(s), cols(h)] for h in range(MEM_HEADS)], axis=0)
            sc = _dot_nt(q_all, k_ref[s].reshape(mem_rows, hd)) * scale
            scores.append(jnp.where(own, sc, -jnp.inf))
        probs = []
        for sc in scores:
            e = jnp.exp(sc - jnp.max(sc, axis=-1, keepdims=True))
            probs.append(e / jnp.sum(e, axis=-1, keepdims=True))
        for s, p in enumerate(probs):
            o_all = _dot(p, v_ref[s].reshape(mem_rows, hd))
            for h in range(MEM_HEADS):
                o_ref[rows(s), cols(h)] = o_all[h * tq:(h + 1) * tq].astype(o_ref.dtype)
        return
    pairs = [(s, h) for s in range(n_seq) for h in range(MEM_HEADS)]
    mem = lambda ref, s, h: ref[s, :, cols(h)]
    scores = [_dot_nt(q_ref[rows(s), cols(h)], mem(k_ref, s, h)) * scale for s, h in pairs]
    probs = []
    for sc in scores:
        e = jnp.exp(sc - jnp.max(sc, axis=-1, keepdims=True))
        probs.append(e / jnp.sum(e, axis=-1, keepdims=True))
    for (s, h), p in zip(pairs, probs):
        o_ref[rows(s), cols(h)] = _dot(p, mem(v_ref, s, h)).astype(o_ref.dtype)


def mem_attention(proj, col_q, mem_k, col_k, mem_v, col_v, *, width, n_seq, seq_len, tq, seq_per_step):
    tokens = n_seq * seq_len
    mem_len = mem_k.shape[1]
    C = width
    nt = seq_len // tq
    rows = seq_per_step * tq
    grid = (n_seq // seq_per_step, nt)
    row_map = lambda b, t: (b * nt + t)
    if mem_k.ndim == 4:
        blk = (seq_per_step,) + mem_k.shape[1:]
        k_spec = v_spec = pl.BlockSpec(blk, lambda b, t: (b, 0, 0, 0))
    else:
        k_spec = pl.BlockSpec((seq_per_step, mem_len, C), lambda b, t: (b, 0, col_k))
        v_spec = pl.BlockSpec((seq_per_step, mem_len, C), lambda b, t: (b, 0, col_v))
    return pl.pallas_call(
        functools.partial(_mem_attn_kernel, n_seq=seq_per_step, tq=tq),
        grid=grid,
        in_specs=[pl.BlockSpec((rows, C), lambda b, t: (row_map(b, t), col_q)), k_spec, v_spec],
        out_specs=pl.BlockSpec((rows, C), lambda b, t: (row_map(b, t), 0)),
        out_shape=jax.ShapeDtypeStruct((tokens, C), _lhs_dtype(tq)),
        compiler_params=_params("parallel", "arbitrary"),
        name="mem_attention",
    )(proj, mem_k, mem_v)


def _merge_kernel(yr_ref, yg_ref, ym_ref, g0_ref, g1_ref, g2_ref, w0_ref, w1_ref, w2_ref, o_ref):
    acc = _sigmoid(g0_ref[...]) * _dot(yr_ref[...], w0_ref[...])
    acc = acc + _sigmoid(g1_ref[...]) * _dot(yg_ref[...], w1_ref[...])
    acc = acc + _sigmoid(g2_ref[...]) * _dot(ym_ref[...], w2_ref[...])
    o_ref[...] = acc.astype(BF16)


def merge_branches(y_rnn, y_gdn, y_mem, proj, col_gate, w_rnn_up, w_gdn_up, w_mem_up, *, tm, tn):
    M, C = y_rnn.shape
    N = w_rnn_up.shape[1]
    nj = N // tn
    y_spec = pl.BlockSpec((tm, C), lambda i, j: (i, 0))
    w_spec = pl.BlockSpec((C, tn), lambda i, j: (0, j))
    gate_specs = [pl.BlockSpec((tm, tn), lambda i, j, b=b: (i, col_gate + b * nj + j)) for b in range(N_BRANCH)]
    return pl.pallas_call(
        _merge_kernel,
        grid=(M // tm, nj),
        in_specs=[y_spec, y_spec, y_spec] + gate_specs + [w_spec, w_spec, w_spec],
        out_specs=pl.BlockSpec((tm, tn), lambda i, j: (i, j)),
        out_shape=jax.ShapeDtypeStruct((M, N), BF16),
        compiler_params=_params("parallel", "arbitrary"),
        name="merge_branches",
    )(y_rnn, y_gdn, y_mem, proj, proj, proj, w_rnn_up, w_gdn_up, w_mem_up)


def _matmul_residual_kernel(a_ref, w_ref, x_ref, o_ref):
    o_ref[...] = x_ref[...] + _dot(a_ref[...], w_ref[...])


def matmul_residual(a, w, x, *, tm, tn):
    M, K = a.shape
    N = w.shape[1]
    return pl.pallas_call(
        _matmul_residual_kernel,
        grid=(M // tm, N // tn),
        in_specs=[pl.BlockSpec((tm, K), lambda i, j: (i, 0)),
                  pl.BlockSpec((K, tn), lambda i, j: (0, j)),
                  pl.BlockSpec((tm, tn), lambda i, j: (i, j))],
        out_specs=pl.BlockSpec((tm, tn), lambda i, j: (i, j)),
        out_shape=jax.ShapeDtypeStruct((M, N), F32),
        compiler_params=_params("parallel", "arbitrary"),
        name="matmul_residual",
    )(a, w, x)


def _mlp_kernel(x_hbm, g_ref, wu_ref, wd_ref, gf_ref, o_ref, xn_ref, x_buf, sem):
    def consume(x_ref):
        xn_ref[...] = _rmsnorm(x_ref[...], g_ref[...]).astype(BF16)
        o_ref[...] = x_ref[...]

    _row_tile_prefetch(x_hbm, x_buf, sem, consume)
    hid = _dot(xn_ref[...], wu_ref[...])
    act = jnp.square(jnp.maximum(hid, 0.0))
    o_ref[...] += _dot(act, wd_ref[...])

    @pl.when(pl.program_id(1) == pl.num_programs(1) - 1)
    def _():
        o_ref[...] = _rmsnorm(o_ref[...], gf_ref[...])


def mlp_final_norm(x, g, w_up, w_down, g_final, *, tm, tf):
    M, D = x.shape
    FF = w_up.shape[1]
    assert FF // tf >= 2
    return pl.pallas_call(
        _mlp_kernel,
        grid=(M // tm, FF // tf),
        in_specs=[pl.BlockSpec(memory_space=pl.ANY),
                  pl.BlockSpec((1, D), lambda i, j: (0, 0)),
                  pl.BlockSpec((D, tf), lambda i, j: (0, j)),
                  pl.BlockSpec((tf, D), lambda i, j: (j, 0)),
                  pl.BlockSpec((1, D), lambda i, j: (0, 0))],
        out_specs=pl.BlockSpec((tm, D), lambda i, j: (i, 0)),
        out_shape=jax.ShapeDtypeStruct((M, D), F32),
        scratch_shapes=[pltpu.VMEM((tm, D), BF16), pltpu.VMEM((tm, D), F32), pltpu.SemaphoreType.DMA(())],
        compiler_params=_params("arbitrary", "arbitrary"),
        name="mlp_final_norm",
    )(x, g.reshape(1, D), w_up, w_down, g_final.reshape(1, D))


def _in_proj_segments(d_model):
    half = d_model // 2
    small_row = 6 * half
    mq_row = small_row + 2 * GDN_HEADS
    mg_row = mq_row + half
    return ((mg_row, N_BRANCH * d_model), (0, 6 * half), (mq_row, half)), small_row


def _prep_layer_weights(rnn_wx, rnn_wa, gdn_A_log, gdn_dt_bias):
    w_gate = jnp.concatenate([rnn_wx, rnn_wa], axis=-1).astype(BF16)
    lane_pad = (GDN_HEADS, LANES - 2 * GDN_HEADS)
    alog_row = jnp.pad(gdn_A_log, lane_pad).reshape(1, LANES)
    dtb_row = jnp.pad(gdn_dt_bias, lane_pad).reshape(1, LANES)
    return w_gate, alog_row, dtb_row


def _tile_plan(n_seq, seq_len):
    tokens = n_seq * seq_len
    if seq_len > SUBLANES:
        return dict(tm=min(tokens, 1024), proj_tm=min(tokens, 2048), row_tile=min(seq_len, 512),
                    attn_tq=min(seq_len, 1024), attn_seqs=1, gdn_tiles=min(4, seq_len // GDN_CHUNK))
    return dict(tm=min(tokens, 1024), proj_tm=min(tokens, 1024), row_tile=min(n_seq, 16) * SUBLANES,
                attn_tq=seq_len, attn_seqs=min(n_seq, 8), gdn_tiles=max(1, min(2, tokens // GDN_CHUNK)))


def _conv_state_in(buf, carry):
    if carry:
        return jnp.pad(buf, ((0, 0), (SUBLANES - (CONV_W - 1), 0), (0, 0)))
    return jnp.swapaxes(buf, 0, 1)


def _conv_state_out(state, carry):
    return state[:, SUBLANES - (CONV_W - 1):] if carry else jnp.swapaxes(state, 0, 1)


def _group_layer(x, mem_k, col_k, mem_v, col_v, rnn_buf, rnn_h0, gdn_buf, gdn_s0, lw, *, n_seq, seq_len,
                 tm, proj_tm, row_tile, attn_tq, attn_seqs, gdn_tiles, final_g):
    D = x.shape[1]
    half = D // 2
    n_gate_blk = N_BRANCH * D // half
    c_rx, c_rg, c_gq, c_gz, c_mq = (n_gate_blk + i for i in (0, 1, 2, 5, 6))
    carry = seq_len > SUBLANES
    assert carry == (seq_len >= GDN_CHUNK)
    segments, small_row = _in_proj_segments(D)
    proj, small = in_projection(x, lw["norm_mix_g"], lw["w_in_t"], segments, small_row, tm=proj_tm,
                                tn=512 if proj_tm > 1024 else 1024)

    y_rnn, rnn_last, h_last = rglru_branch(
        proj, c_rx, c_rg, _conv_state_in(rnn_buf, carry), rnn_h0.reshape(n_seq, 1, half),
        lw["rnn_conv_w"], lw["rnn_conv_b"], lw["w_gate"], lw["rnn_bx"], lw["rnn_ba"], lw["rnn_L"],
        n_seq=n_seq, seq_len=seq_len, rows=row_tile)

    y_gdn, s_new, gdn_last = gdn_branch(
        proj, c_gq, c_gz, small, _conv_state_in(gdn_buf, carry), lw["gdn_conv_w"], lw["alog_row"],
        lw["dtb_row"],
        lw["gdn_norm_g"], gdn_s0, n_seq=n_seq, seq_len=seq_len, tiles_per_step=gdn_tiles)

    y_mem = mem_attention(proj, c_mq, mem_k, col_k, mem_v, col_v, width=half, n_seq=n_seq, seq_len=seq_len,
                          tq=attn_tq, seq_per_step=attn_seqs)

    merged = merge_branches(y_rnn, y_gdn, y_mem, proj, 0, lw["w_rnn_up"], lw["w_gdn_up"], lw["w_mem_up"],
                            tm=proj_tm, tn=256)
    x1 = matmul_residual(merged, lw["w_out"], x, tm=proj_tm, tn=512)
    x2 = mlp_final_norm(x1, lw["norm_mlp_g"], lw["w_mlp_up"], lw["w_mlp_down"], final_g, tm=tm, tf=512)
    states = (_conv_state_out(rnn_last, carry), h_last.reshape(n_seq, half),
              _conv_state_out(gdn_last, carry), s_new)
    return x2, states


def kernel(x_prompt, x_sample, mem_prompt, cache_mem_k, cache_mem_v, state_rnn_conv, state_rnn_h,
           state_gdn_conv, state_gdn_S, norm_mix_g, w_in, rnn_conv_w, rnn_conv_b, rnn_wx, rnn_bx, rnn_wa,
           rnn_ba, rnn_L, gdn_conv_w, gdn_A_log, gdn_dt_bias, gdn_norm_g, mem_norm_g, w_mem_kv, w_rnn_up,
           w_gdn_up, w_mem_up, w_out, norm_mlp_g, w_mlp_up, w_mlp_down, norm_final_g):
    depth = w_in.shape[0]
    assert depth == 1, "the final norm is fused into the last layer's MLP kernel; one layer supported"
    Bp, T, D = x_prompt.shape
    Bs, Ts, _ = x_sample.shape
    half = D // 2
    mem_len = mem_prompt.shape[1]
    assert Ts == SUBLANES and T % GDN_CHUNK == 0
    l = 0
    w_gate, alog_row, dtb_row = _prep_layer_weights(rnn_wx[l], rnn_wa[l], gdn_A_log[l], gdn_dt_bias[l])
    lw = dict(norm_mix_g=norm_mix_g[l], w_in_t=jnp.swapaxes(w_in[l], 0, 1), w_gate=w_gate,
              rnn_conv_w=rnn_conv_w[l], rnn_conv_b=rnn_conv_b[l], rnn_bx=rnn_bx[l], rnn_ba=rnn_ba[l],
              rnn_L=rnn_L[l], gdn_conv_w=gdn_conv_w[l], alog_row=alog_row, dtb_row=dtb_row,
              gdn_norm_g=gdn_norm_g[l], w_rnn_up=w_rnn_up[l], w_gdn_up=w_gdn_up[l],
              w_mem_up=w_mem_up[l], w_out=w_out[l], norm_mlp_g=norm_mlp_g[l],
              w_mlp_up=w_mlp_up[l], w_mlp_down=w_mlp_down[l])

    kv, k4, v4 = memory_kv(mem_prompt.reshape(Bp * mem_len, D), mem_norm_g[l], w_mem_kv[l],
                           tm=min(Bp * mem_len, 512))
    kv3 = kv.reshape(Bp, mem_len, 2 * half)
    zeros = lambda *s: jnp.zeros(s, F32)
    yp, (rb_p, rh_p, gb_p, gs_p) = _group_layer(
        x_prompt.reshape(Bp * T, D), kv3, 0, kv3, 1,
        zeros(Bp, CONV_W - 1, half), zeros(Bp, half), zeros(Bp, CONV_W - 1, 3 * half),
        zeros(Bp, GDN_HEADS, half // GDN_HEADS, half // GDN_HEADS), lw,
        n_seq=Bp, seq_len=T, final_g=norm_final_g, **_tile_plan(Bp, T))
    mk_p = k4.reshape(1, Bp, mem_len, MEM_HEADS, half // MEM_HEADS)
    mv_p = v4.reshape(1, Bp, mem_len, MEM_HEADS, half // MEM_HEADS)

    ys, (rb_s, rh_s, gb_s, gs_s) = _group_layer(
        x_sample.reshape(Bs * Ts, D), cache_mem_k.reshape((depth * Bs,) + cache_mem_k.shape[2:]), 0,
        cache_mem_v.reshape((depth * Bs,) + cache_mem_v.shape[2:]), 0,
        state_rnn_conv[l], state_rnn_h[l], state_gdn_conv[l], state_gdn_S[l], lw,
        n_seq=Bs, seq_len=Ts, final_g=norm_final_g, **_tile_plan(Bs, Ts))

    return (yp.reshape(Bp, T, D), ys.reshape(Bs, Ts, D), mk_p, mv_p, rb_p[None], rh_p[None], gb_p[None],
            gs_p[None], rb_s[None], rh_s[None], gb_s[None], gs_s[None])
```

```python
import functools

import jax
import jax.numpy as jnp
from jax import lax
from jax.experimental import pallas as pl
from jax.experimental.pallas import tpu as pltpu

F32 = jnp.float32
BF16 = jnp.bfloat16

EPS = 1e-6
RG_C = 8.0
CONV_W = 4
RNN_BLOCKS = 8
GDN_HEADS = 8
GDN_CHUNK = 64
MEM_HEADS = 4
N_BRANCH = 3
MERGE_ROW_CHUNKS = 8

SUBLANES = 8
BF16_SUBLANES = 16
LANES = 128
VMEM_LIMIT_BYTES = 56 * 1024 * 1024


def _lhs_dtype(slab_rows):
    return BF16 if slab_rows % BF16_SUBLANES == 0 else F32


def _params(*sem):
    return pltpu.CompilerParams(dimension_semantics=sem, vmem_limit_bytes=VMEM_LIMIT_BYTES)


def _dot(a, b):
    return jnp.dot(a.astype(BF16), b.astype(BF16), preferred_element_type=F32)


def _dot_nt(a, b):
    return lax.dot_general(a.astype(BF16), b.astype(BF16), (((1,), (1,)), ((), ())),
                           preferred_element_type=F32)


def _dot_tn(a, b):
    return lax.dot_general(a.astype(BF16), b.astype(BF16), (((0,), (0,)), ((), ())),
                           preferred_element_type=F32)


def _rmsnorm(x, g):
    return (x * lax.rsqrt(jnp.mean(x * x, axis=-1, keepdims=True) + EPS)) * g


def _softplus(x):
    return jnp.maximum(x, 0.0) + jnp.log1p(jnp.exp(-jnp.abs(x)))


_sigmoid = jax.nn.sigmoid


def _silu(x):
    return x * _sigmoid(x)


def _memory_kv_kernel(x_ref, g_ref, w_ref, kv_ref, k4_ref, v4_ref, xn_ref):
    j = pl.program_id(1)

    @pl.when(j == 0)
    def _():
        xn_ref[...] = _rmsnorm(x_ref[...], g_ref[...]).astype(BF16)

    acc = _dot(xn_ref[...], w_ref[...])
    kv_ref[...] = acc
    split = acc.reshape(k4_ref.shape)

    @pl.when(j == 0)
    def _():
        k4_ref[...] = split

    @pl.when(j == 1)
    def _():
        v4_ref[...] = split


def memory_kv(x, g, w, *, tm):
    M, K = x.shape
    C = w.shape[1] // 2
    hd = C // MEM_HEADS
    head_spec = pl.BlockSpec((tm, MEM_HEADS, hd), lambda i, j: (i, 0, 0))
    return pl.pallas_call(
        _memory_kv_kernel,
        grid=(M // tm, 2),
        in_specs=[pl.BlockSpec((tm, K), lambda i, j: (i, 0)),
                  pl.BlockSpec((1, K), lambda i, j: (0, 0)),
                  pl.BlockSpec((K, C), lambda i, j: (0, j))],
        out_specs=[pl.BlockSpec((tm, C), lambda i, j: (i, j)), head_spec, head_spec],
        out_shape=[jax.ShapeDtypeStruct((M, 2 * C), F32)] + [jax.ShapeDtypeStruct((M, MEM_HEADS, hd), F32)] * 2,
        scratch_shapes=[pltpu.VMEM((tm, K), BF16)],
        compiler_params=_params("parallel", "arbitrary"),
        name="memory_kv",
    )(x, g.reshape(1, K), w)


def _row_tile_prefetch(x_hbm, x_buf, sem, consume):
    i, j = pl.program_id(0), pl.program_id(1)
    tm = x_buf.shape[0]

    def copy(tile):
        return pltpu.make_async_copy(x_hbm.at[pl.ds(pl.multiple_of(tile * tm, tm), tm), :], x_buf, sem)

    @pl.when((i == 0) & (j == 0))
    def _():
        copy(0).start()

    @pl.when(j == 0)
    def _():
        copy(i).wait()
        consume(x_buf)

    @pl.when((j == 1) & (i + 1 < pl.num_programs(0)))
    def _():
        copy(i + 1).start()


def _in_proj_kernel(x_hbm, g_ref, wt_ref, wst_ref, o_ref, os_ref, xn_ref, x_buf, sem):
    def consume(x_ref):
        xn_ref[...] = _rmsnorm(x_ref[...], g_ref[...]).astype(BF16)
        os_ref[...] = _dot_nt(xn_ref[...], wst_ref[...])

    _row_tile_prefetch(x_hbm, x_buf, sem, consume)
    o_ref[...] = _dot_nt(xn_ref[...], wt_ref[...])


def in_projection(x, g, w_t, segments, small_row, *, tm, tn):
    M, K = x.shape
    n_tiles = [n // tn for _, n in segments]
    assert all(n % tn == 0 and r % SUBLANES == 0 for r, n in segments) and small_row % LANES == 0
    N = tn * sum(n_tiles)

    def w_row(j):
        row, first = None, 0
        for (r0, _), nt in zip(segments, n_tiles):
            cand = r0 // SUBLANES + (tn // SUBLANES) * (j - first)
            row = cand if row is None else jnp.where(j >= first, cand, row)
            first += nt
        return row * SUBLANES

    assert N // tn >= 2
    outs = pl.pallas_call(
        _in_proj_kernel,
        grid=(M // tm, N // tn),
        in_specs=[pl.BlockSpec(memory_space=pl.ANY),
                  pl.BlockSpec((1, K), lambda i, j: (0, 0)),
                  pl.BlockSpec((pl.Element(tn), pl.Element(K)), lambda i, j: (w_row(j), 0)),
                  pl.BlockSpec((LANES, K), lambda i, j: (small_row // LANES, 0))],
        out_specs=[pl.BlockSpec((tm, tn), lambda i, j: (i, j)), pl.BlockSpec((tm, LANES), lambda i, j: (i, 0))],
        out_shape=[jax.ShapeDtypeStruct((M, N), F32), jax.ShapeDtypeStruct((M, LANES), F32)],
        scratch_shapes=[pltpu.VMEM((tm, K), BF16), pltpu.VMEM((tm, K), F32), pltpu.SemaphoreType.DMA(())],
        compiler_params=_params("arbitrary", "arbitrary"),
        name="in_projection",
    )(x, g.reshape(1, K), w_t, w_t)
    return outs


def _causal_conv(x, xprev, w_ref):
    t_idx = lax.broadcasted_iota(jnp.int32, x.shape, 1)
    y = x * w_ref[CONV_W - 1:CONV_W, :][None]
    for k in range(1, CONV_W):
        shifted = jnp.where(t_idx >= k, pltpu.roll(x, k, 1), pltpu.roll(xprev, k, 1))
        y = y + shifted * w_ref[CONV_W - 1 - k:CONV_W - k, :][None]
    return y


def _segment_scan(a, u):
    t_idx = lax.broadcasted_iota(jnp.int32, a.shape, 1)
    s = 1
    while s < SUBLANES:
        keep = t_idx >= s
        u = jnp.where(keep, a * pltpu.roll(u, s, 1) + u, u)
        a = jnp.where(keep, a * pltpu.roll(a, s, 1), a)
        s *= 2
    return a, u


def _state_group(state_ref, cols=slice(None)):
    rows, nb = state_ref.shape[0], state_ref.shape[1]
    first = [state_ref[r, :, cols] for r in range(rows)]
    t_idx = lax.broadcasted_iota(jnp.int32, (nb, SUBLANES, first[0].shape[-1]), 1)
    y = jnp.zeros(t_idx.shape, F32)
    for r, row in enumerate(first):
        y = jnp.where(t_idx == SUBLANES - rows + r, row[:, None, :], y)
    return y


def _store_state(state_ref, x, cols=slice(None)):
    rows = state_ref.shape[0]
    for r in range(rows):
        state_ref[r, :, cols] = x[:, SUBLANES - rows + r, :]


def _conv_state_spec(carry, nb, n_seq, width):
    if carry:
        return pl.BlockSpec((nb, SUBLANES, width), lambda b, t: (b, 0, 0)), (n_seq, SUBLANES, width)
    return pl.BlockSpec((CONV_W - 1, nb, width), lambda b, t: (0, b, 0)), (CONV_W - 1, n_seq, width)


def _stage_groups(x, xprev_ref, xs_ref, first):
    G = x.shape[0]

    @pl.when(first)
    def _():
        xs_ref[G] = xprev_ref[0]

    xs_ref[0] = xs_ref[G]
    xs_ref[1:G + 1] = x
    return xs_ref[0:G]


def _rglru_kernel(rx_ref, rg_ref, xprev_ref, h0_ref, cw_ref, cb_ref, wg_ref, bx_ref, ba_ref, l_ref,
                  y_ref, xlast_ref, hlast_ref, *scratch, carry):
    R, C = rx_ref.shape
    G = R // SUBLANES
    x = rx_ref[...].reshape(G, SUBLANES, C)
    if carry:
        xs_ref, a_ref, u_ref, h_ref, hc_ref = scratch
        first = pl.program_id(1) == 0
        xprev = _stage_groups(x, xprev_ref, xs_ref, first)

        @pl.when(first)
        def _():
            hc_ref[...] = h0_ref[0]
    else:
        xprev = _state_group(xprev_ref)

    xc = (_causal_conv(x, xprev, cw_ref) + cb_ref[...][None]).reshape(R, C)
    xb = xc.astype(BF16)
    bs = C // RNN_BLOCKS
    zi, zr = [], []
    for n in range(RNN_BLOCKS):
        z = jnp.dot(xb[:, n * bs:(n + 1) * bs], wg_ref[n], preferred_element_type=F32)
        zi.append(z[:, :bs])
        zr.append(z[:, bs:])
    gi = _sigmoid(jnp.concatenate(zi, axis=1) + bx_ref[...])
    gr = _sigmoid(jnp.concatenate(zr, axis=1) + ba_ref[...])
    lv = l_ref[...]
    log_sig_l = -_softplus(-lv)
    log_a = RG_C * gr * log_sig_l
    a = jnp.exp(log_a)
    u = jnp.sqrt(-jnp.tanh(log_a) * (a * a + 1.0)) * (gi * xc)
    a_cum, h_loc = _segment_scan(a.reshape(G, SUBLANES, C), u.reshape(G, SUBLANES, C))

    if carry:
        a_ref[...] = a_cum
        u_ref[...] = h_loc

        def body(g, h_prev):
            hg = u_ref[g] + a_ref[g] * h_prev
            h_ref[g] = hg
            return hg[SUBLANES - 1:SUBLANES, :]

        h_last = lax.fori_loop(0, G, body, hc_ref[...])
        hc_ref[...] = h_last
        h = h_ref[...]
        xlast_ref[0] = x[G - 1]
        hlast_ref[0] = h_last
    else:
        h = h_loc + a_cum * h0_ref[...]
        _store_state(xlast_ref, x)
        hlast_ref[...] = h[:, SUBLANES - 1:SUBLANES, :]

    y = h.reshape(R, C) * jax.nn.gelu(rg_ref[...])
    y_ref[...] = y.astype(BF16)


def rglru_branch(proj, col_rx, col_rg, xprev, h0, cw, cb, wg, bx, ba, lam, *, n_seq, seq_len, rows):
    C = cw.shape[1]
    carry = seq_len > SUBLANES
    if carry:
        nt = seq_len // rows
        grid = (n_seq, nt)
        row_map = lambda b, t: (b * nt + t)
        nb = 1
        G = rows // SUBLANES
        scratch = [pltpu.VMEM((G + 1, SUBLANES, C), F32), pltpu.VMEM((G, SUBLANES, C), F32),
                   pltpu.VMEM((G, SUBLANES, C), F32), pltpu.VMEM((G, SUBLANES, C), F32),
                   pltpu.VMEM((1, C), F32)]
    else:
        nb = rows // SUBLANES
        grid = (n_seq // nb, 1)
        row_map = lambda b, t: b
        scratch = []
    conv_spec, conv_shape = _conv_state_spec(carry, nb, n_seq, C)
    const2 = lambda b, t: (0, 0)
    in_specs = [pl.BlockSpec((rows, C), lambda b, t: (row_map(b, t), col_rx)),
                pl.BlockSpec((rows, C), lambda b, t: (row_map(b, t), col_rg)),
                conv_spec,
                pl.BlockSpec((nb, 1, C), lambda b, t: (b, 0, 0)),
                pl.BlockSpec((CONV_W, C), const2),
                pl.BlockSpec((1, C), const2),
                pl.BlockSpec(wg.shape, lambda b, t: (0, 0, 0)),
                pl.BlockSpec((1, C), const2), pl.BlockSpec((1, C), const2), pl.BlockSpec((1, C), const2)]
    tokens = n_seq * seq_len
    out_shape = [jax.ShapeDtypeStruct((tokens, C), BF16),
                 jax.ShapeDtypeStruct(conv_shape, F32),
                 jax.ShapeDtypeStruct((n_seq, 1, C), F32)]
    out_specs = [pl.BlockSpec((rows, C), lambda b, t: (row_map(b, t), 0)),
                 conv_spec,
                 pl.BlockSpec((nb, 1, C), lambda b, t: (b, 0, 0))]
    return pl.pallas_call(
        functools.partial(_rglru_kernel, carry=carry),
        grid=grid, in_specs=in_specs, out_specs=out_specs, out_shape=out_shape,
        scratch_shapes=scratch,
        compiler_params=_params("parallel", "arbitrary"),
        name="rglru",
    )(proj, proj, xprev, h0, cw, cb.reshape(1, C), wg, bx.reshape(1, C), ba.reshape(1, C),
      lam.reshape(1, C))


def _l2norm_heads(x, scale):
    dh = x.shape[1] // GDN_HEADS
    outs = []
    for h in range(GDN_HEADS):
        xh = x[:, h * dh:(h + 1) * dh]
        xh = xh * lax.rsqrt(jnp.sum(xh * xh, axis=-1, keepdims=True) + EPS)
        outs.append(xh * scale if scale != 1.0 else xh)
    return jnp.concatenate(outs, axis=1)


def _gdn_conv_norm(src_refs, xprev_ref, cw_ref, dst_refs, xlast_ref, xs_ref, carry):
    R, C = src_refs[0].shape
    G = R // SUBLANES
    dk = C // GDN_HEADS
    for s, (src, dst) in enumerate(zip(src_refs, dst_refs)):
        x = src[...].reshape(G, SUBLANES, C)
        cols = slice(s * C, (s + 1) * C)
        if carry:
            xprev = _stage_groups(x, xprev_ref.at[:, :, cols], xs_ref.at[s], pl.program_id(1) == 0)
            xlast_ref[0, :, cols] = x[G - 1]
        else:
            xprev = _state_group(xprev_ref, cols)
            _store_state(xlast_ref, x, cols)
        y = _silu(_causal_conv(x, xprev, cw_ref.at[:, cols])).reshape(R, C)
        if s == 0:
            y = _l2norm_heads(y, dk ** -0.5)
        elif s == 1:
            y = _l2norm_heads(y, 1.0)
        dst[...] = y


def _segment_cumsum(x, seg):
    pos = lax.broadcasted_iota(jnp.int32, x.shape, 0) & (seg - 1)
    s = 1
    while s < seg:
        x = x + jnp.where(pos >= s, pltpu.roll(x, s, 0), 0.0)
        s *= 2
    return x


def _segment_last(x, seg):
    n = x.shape[0]
    pos = lax.broadcasted_iota(jnp.int32, x.shape, 0) & (seg - 1)
    s = seg // 2
    while s >= 1:
        x = jnp.where((pos & (2 * s - 1)) < s, pltpu.roll(x, n - s, 0), x)
        s //= 2
    return x


class _TileMasks:
    def __init__(self, rows, seg):
        r = lax.broadcasted_iota(jnp.int32, (rows, rows), 0)
        c = lax.broadcasted_iota(jnp.int32, (rows, rows), 1)
        shift = seg.bit_length() - 1
        same = (r >> shift) == (c >> shift)
        self.incl = same & (r >= c)
        self.strict = same & (r > c)
        self.eye = r == c
        self.levels = []
        s = 1
        while s < seg:
            b = s.bit_length() - 1
            self.levels.append(((r >> (b + 1)) == (c >> (b + 1))) & (((r >> b) & 1) == 1) & (((c >> b) & 1) == 0))
            s *= 2


def _unit_lower_inverses(ms, masks):
    eye = jnp.where(masks.eye, 1.0, 0.0)
    xs = [eye - jnp.where(masks.levels[0], m, 0.0) for m in ms]
    for level in masks.levels[1:]:
        ts = [_dot(jnp.where(level, m, 0.0), x) for m, x in zip(ms, xs)]
        xs = [x - _dot(x, t) for x, t in zip(xs, ts)]
    return xs


def _gdn_tiles_local(chains, masks):
    r, d = chains[0][0].shape
    kbs, decays, kk_qks = [], [], []
    for q, k, v, beta, gc, g_last in chains:
        gc_row = jnp.sum(jnp.where(masks.eye, gc, 0.0), axis=0, keepdims=True)
        decays.append(jnp.where(masks.incl, jnp.exp(gc - gc_row), 0.0))
        kbs.append(k * beta)
        kk_qks.append(_dot_nt(jnp.concatenate([kbs[-1], q], axis=0), k))
    ms = [jnp.where(masks.strict, kq[:r] * dec, 0.0) for kq, dec in zip(kk_qks, decays)]
    tms = _unit_lower_inverses(ms, masks)
    out = []
    for (q, k, v, beta, gc, g_last), kb, dec, kq, tm in zip(chains, kbs, decays, kk_qks, tms):
        eg = jnp.exp(gc)
        vk = _dot(tm, jnp.concatenate([v * beta, kb * eg], axis=1))
        out.append(dict(value=vk[:, :d], kcd=vk[:, d:], attn=kq[r:] * dec, qg=q * eg,
                        kd=k * jnp.exp(g_last - gc), decay_last=jnp.exp(g_last)))
    return out


def _gdn_kernel(gq_ref, gk_ref, gv_ref, z_ref, sm_ref, xprev_ref, cw_ref, alog_ref, dtb_ref, ng_ref, s0_ref,
                o_ref, s_ref, xlast_ref, q_ref, k_ref, v_ref, *scratch, tile, seg, n_tiles, carry):
    dk = q_ref.shape[1] // GDN_HEADS
    n_seg = tile // seg
    masks = _TileMasks(tile, seg)
    neg_a = -jnp.exp(alog_ref[...])
    dtb = dtb_ref[...]
    ng = ng_ref[...]
    if carry:
        @pl.when(pl.program_id(1) == 0)
        def _():
            s_ref[...] = s0_ref[...]
    _gdn_conv_norm((gq_ref, gk_ref, gv_ref), xprev_ref, cw_ref, (q_ref, k_ref, v_ref), xlast_ref,
                   scratch[0] if carry else None, carry)

    chains = []
    for i in range(n_tiles):
        rows = slice(i * tile, (i + 1) * tile)
        sm = sm_ref[rows, :]
        beta_all = _sigmoid(sm)
        gc_all = _segment_cumsum(neg_a * _softplus(sm + dtb), seg)
        gl_all = _segment_last(gc_all, seg)
        for h in range(GDN_HEADS):
            cols = slice(h * dk, (h + 1) * dk)
            lane = slice(GDN_HEADS + h, GDN_HEADS + h + 1)
            chains.append((q_ref[rows, cols], k_ref[rows, cols], v_ref[rows, cols],
                           beta_all[:, h:h + 1], gc_all[:, lane], gl_all[:, lane]))
    local = _gdn_tiles_local(chains, masks)

    heads = range(GDN_HEADS)
    for i in range(n_tiles):
        rows = slice(i * tile, (i + 1) * tile)
        loc = local[i * GDN_HEADS:(i + 1) * GDN_HEADS]
        v_new = [[] for _ in heads]
        o_state = [[] for _ in heads]
        for j in range(n_seg):
            sl = slice(j * seg, (j + 1) * seg)
            seq = 0 if carry else i * n_seg + j
            s_prev = [s_ref[seq, h] if carry else s0_ref[seq, h] for h in heads]
            rs = [_dot(jnp.concatenate([loc[h]["kcd"][sl], loc[h]["qg"][sl]], axis=0), s_prev[h])
                  for h in heads]
            vns = [loc[h]["value"][sl] - rs[h][:seg] for h in heads]
            for h in heads:
                v_new[h].append(vns[h])
                o_state[h].append(rs[h][seg:])
                s_ref[seq, h] = (s_prev[h] * loc[h]["decay_last"][j * seg:j * seg + 1, :]
                                 + _dot_tn(loc[h]["kd"][sl], vns[h]))
        for h in heads:
            cols = slice(h * dk, (h + 1) * dk)
            vn = jnp.concatenate(v_new[h], axis=0) if n_seg > 1 else v_new[h][0]
            os_ = jnp.concatenate(o_state[h], axis=0) if n_seg > 1 else o_state[h][0]
            o = os_ + _dot(loc[h]["attn"], vn)
            o_ref[rows, cols] = (_rmsnorm(o, ng) * _silu(z_ref[rows, cols])).astype(o_ref.dtype)


def gdn_branch(proj, col_q, col_z, small, xprev, cw, alog_row, dtb_row, ng, s0, *, n_seq, seq_len,
               tiles_per_step):
    tokens = proj.shape[0]
    C = cw.shape[1] // 3
    dk = C // GDN_HEADS
    tile = GDN_CHUNK
    carry = seq_len >= GDN_CHUNK
    seg = GDN_CHUNK if carry else seq_len
    rows = tiles_per_step * tile
    if carry:
        nt = seq_len // rows
        seq_per_step = 1
        grid = (n_seq, nt)
    else:
        nt = 1
        seq_per_step = rows // seg
        grid = (n_seq // seq_per_step, 1)
    row_map = lambda b, t: (b * nt + t)
    tok_spec = pl.BlockSpec((rows, C), lambda b, t: (row_map(b, t), 0))
    const2 = lambda b, t: (0, 0)
    s_spec = pl.BlockSpec((seq_per_step, GDN_HEADS, dk, dk), lambda b, t: (b, 0, 0, 0))
    conv_spec, conv_shape = _conv_state_spec(carry, seq_per_step, n_seq, 3 * C)
    in_specs = [pl.BlockSpec((rows, C), lambda b, t, c=c: (row_map(b, t), c))
                for c in (col_q, col_q + 1, col_q + 2, col_z)]
    in_specs += [pl.BlockSpec((rows, LANES), lambda b, t: (row_map(b, t), 0)),
                 conv_spec, pl.BlockSpec((CONV_W, 3 * C), const2),
                 pl.BlockSpec((1, LANES), const2), pl.BlockSpec((1, LANES), const2),
                 pl.BlockSpec((1, dk), const2), s_spec]
    scratch = [pltpu.VMEM((rows, C), F32)] * 3
    if carry:
        scratch.append(pltpu.VMEM((3, rows // SUBLANES + 1, SUBLANES, C), F32))
    return pl.pallas_call(
        functools.partial(_gdn_kernel, tile=tile, seg=seg, n_tiles=tiles_per_step, carry=carry),
        grid=grid, in_specs=in_specs,
        out_specs=[tok_spec, s_spec, conv_spec],
        out_shape=[jax.ShapeDtypeStruct((tokens, C), _lhs_dtype(tile)), jax.ShapeDtypeStruct(s0.shape, F32),
                   jax.ShapeDtypeStruct(conv_shape, F32)],
        scratch_shapes=scratch,
        compiler_params=_params("parallel", "arbitrary"),
        name="gdn_branch",
    )(proj, proj, proj, proj, small, xprev, cw, alog_row, dtb_row, ng.reshape(1, dk), s0)


def _mem_attn_kernel(q_ref, k_ref, v_ref, o_ref, *, n_seq, tq):
    hd = q_ref.shape[1] // MEM_HEADS
    scale = hd ** -0.5
    rows = lambda s: slice(s * tq, (s + 1) * tq)
    cols = lambda h: slice(h * hd, (h + 1) * hd)
    if len(k_ref.shape) == 4:
        mem_rows = k_ref.shape[1] * MEM_HEADS
        r_head = lax.broadcasted_iota(jnp.int32, (MEM_HEADS * tq, mem_rows), 0) // tq
        c_head = lax.broadcasted_iota(jnp.int32, (MEM_HEADS * tq, mem_rows), 1) & (MEM_HEADS - 1)
        own = r_head == c_head
        scores = []
        for s in range(n_seq):
            q_all = jnp.concatenate([q_ref[rows(s), cols(h)] for h in range(MEM_HEADS)], axis=0)
            sc = _dot_nt(q_all, k_ref[s].reshape(mem_rows, hd)) * scale
            scores.append(jnp.where(own, sc, -jnp.inf))
        probs = []
        for sc in scores:
            e = jnp.exp(sc - jnp.max(sc, axis=-1, keepdims=True))
            probs.append(e / jnp.sum(e, axis=-1, keepdims=True))
        for s, p in enumerate(probs):
            o_all = _dot(p, v_ref[s].reshape(mem_rows, hd))
            for h in range(MEM_HEADS):
                o_ref[rows(s), cols(h)] = o_all[h * tq:(h + 1) * tq].astype(o_ref.dtype)
        return
    pairs = [(s, h) for s in range(n_seq) for h in range(MEM_HEADS)]
    mem = lambda ref, s, h: ref[s, :, cols(h)]
    scores = [_dot_nt(q_ref[rows(s), cols(h)], mem(k_ref, s, h)) * scale for s, h in pairs]
    probs = []
    for sc in scores:
        e = jnp.exp(sc - jnp.max(sc, axis=-1, keepdims=True))
        probs.append(e / jnp.sum(e, axis=-1, keepdims=True))
    for (s, h), p in zip(pairs, probs):
        o_ref[rows(s), cols(h)] = _dot(p, mem(v_ref, s, h)).astype(o_ref.dtype)


def mem_attention(proj, col_q, mem_k, col_k, mem_v, col_v, *, width, n_seq, seq_len, tq, seq_per_step):
    tokens = n_seq * seq_len
    mem_len = mem_k.shape[1]
    C = width
    nt = seq_len // tq
    rows = seq_per_step * tq
    grid = (n_seq // seq_per_step, nt)
    row_map = lambda b, t: (b * nt + t)
    if mem_k.ndim == 4:
        blk = (seq_per_step,) + mem_k.shape[1:]
        k_spec = v_spec = pl.BlockSpec(blk, lambda b, t: (b, 0, 0, 0))
    else:
        k_spec = pl.BlockSpec((seq_per_step, mem_len, C), lambda b, t: (b, 0, col_k))
        v_spec = pl.BlockSpec((seq_per_step, mem_len, C), lambda b, t: (b, 0, col_v))
    return pl.pallas_call(
        functools.partial(_mem_attn_kernel, n_seq=seq_per_step, tq=tq),
        grid=grid,
        in_specs=[pl.BlockSpec((rows, C), lambda b, t: (row_map(b, t), col_q)), k_spec, v_spec],
        out_specs=pl.BlockSpec((rows, C), lambda b, t: (row_map(b, t), 0)),
        out_shape=jax.ShapeDtypeStruct((tokens, C), _lhs_dtype(tq)),
        compiler_params=_params("parallel", "arbitrary"),
        name="mem_attention",
    )(proj, mem_k, mem_v)


def _merge_kernel(yr_ref, yg_ref, ym_ref, g0_ref, g1_ref, g2_ref, w0_ref, w1_ref, w2_ref, o_ref):
    weights = [w.astype(BF16) for w in (w0_ref[...], w1_ref[...], w2_ref[...])]
    chunk = o_ref.shape[0] // MERGE_ROW_CHUNKS
    for r in range(MERGE_ROW_CHUNKS):
        rows = slice(r * chunk, (r + 1) * chunk)
        acc = None
        for y_ref, g_ref, w in zip((yr_ref, yg_ref, ym_ref), (g0_ref, g1_ref, g2_ref), weights):
            term = _sigmoid(g_ref[rows, :]) * _dot(y_ref[rows, :], w)
            acc = term if acc is None else acc + term
        o_ref[rows, :] = acc.astype(BF16)


def merge_branches(y_rnn, y_gdn, y_mem, proj, col_gate, w_rnn_up, w_gdn_up, w_mem_up, *, tm, tn):
    M, C = y_rnn.shape
    N = w_rnn_up.shape[1]
    nj = N // tn
    y_spec = pl.BlockSpec((tm, C), lambda i, j: (i, 0))
    w_spec = pl.BlockSpec((C, tn), lambda i, j: (0, j))
    gate_specs = [pl.BlockSpec((tm, tn), lambda i, j, b=b: (i, col_gate + b * nj + j)) for b in range(N_BRANCH)]
    return pl.pallas_call(
        _merge_kernel,
        grid=(M // tm, nj),
        in_specs=[y_spec, y_spec, y_spec] + gate_specs + [w_spec, w_spec, w_spec],
        out_specs=pl.BlockSpec((tm, tn), lambda i, j: (i, j)),
        out_shape=jax.ShapeDtypeStruct((M, N), BF16),
        compiler_params=_params("parallel", "arbitrary"),
        name="merge_branches",
    )(y_rnn, y_gdn, y_mem, proj, proj, proj, w_rnn_up, w_gdn_up, w_mem_up)


def _matmul_residual_kernel(a_ref, w_ref, x_ref, o_ref):
    o_ref[...] = x_ref[...] + _dot(a_ref[...], w_ref[...])


def matmul_residual(a, w, x, *, tm, tn):
    M, K = a.shape
    N = w.shape[1]
    return pl.pallas_call(
        _matmul_residual_kernel,
        grid=(M // tm, N // tn),
        in_specs=[pl.BlockSpec((tm, K), lambda i, j: (i, 0)),
                  pl.BlockSpec((K, tn), lambda i, j: (0, j)),
                  pl.BlockSpec((tm, tn), lambda i, j: (i, j))],
        out_specs=pl.BlockSpec((tm, tn), lambda i, j: (i, j)),
        out_shape=jax.ShapeDtypeStruct((M, N), F32),
        compiler_params=_params("parallel", "arbitrary"),
        name="matmul_residual",
    )(a, w, x)


def _mlp_kernel(x_hbm, g_ref, wu_ref, wd_ref, gf_ref, o_ref, xn_ref, x_buf, sem):
    def consume(x_ref):
        xn_ref[...] = _rmsnorm(x_ref[...], g_ref[...]).astype(BF16)
        o_ref[...] = x_ref[...]

    _row_tile_prefetch(x_hbm, x_buf, sem, consume)
    hid = _dot(xn_ref[...], wu_ref[...])
    act = jnp.square(jnp.maximum(hid, 0.0))
    o_ref[...] += _dot(act, wd_ref[...])

    @pl.when(pl.program_id(1) == pl.num_programs(1) - 1)
    def _():
        o_ref[...] = _rmsnorm(o_ref[...], gf_ref[...])


def mlp_final_norm(x, g, w_up, w_down, g_final, *, tm, tf):
    M, D = x.shape
    FF = w_up.shape[1]
    assert FF // tf >= 2
    return pl.pallas_call(
        _mlp_kernel,
        grid=(M // tm, FF // tf),
        in_specs=[pl.BlockSpec(memory_space=pl.ANY),
                  pl.BlockSpec((1, D), lambda i, j: (0, 0)),
                  pl.BlockSpec((D, tf), lambda i, j: (0, j)),
                  pl.BlockSpec((tf, D), lambda i, j: (j, 0)),
                  pl.BlockSpec((1, D), lambda i, j: (0, 0))],
        out_specs=pl.BlockSpec((tm, D), lambda i, j: (i, 0)),
        out_shape=jax.ShapeDtypeStruct((M, D), F32),
        scratch_shapes=[pltpu.VMEM((tm, D), BF16), pltpu.VMEM((tm, D), F32), pltpu.SemaphoreType.DMA(())],
        compiler_params=_params("arbitrary", "arbitrary"),
        name="mlp_final_norm",
    )(x, g.reshape(1, D), w_up, w_down, g_final.reshape(1, D))


def _in_proj_segments(d_model):
    half = d_model // 2
    small_row = 6 * half
    mq_row = small_row + 2 * GDN_HEADS
    mg_row = mq_row + half
    return ((mg_row, N_BRANCH * d_model), (0, 6 * half), (mq_row, half)), small_row


def _prep_layer_weights(rnn_wx, rnn_wa, gdn_A_log, gdn_dt_bias):
    w_gate = jnp.concatenate([rnn_wx, rnn_wa], axis=-1).astype(BF16)
    lane_pad = (GDN_HEADS, LANES - 2 * GDN_HEADS)
    alog_row = jnp.pad(gdn_A_log, lane_pad).reshape(1, LANES)
    dtb_row = jnp.pad(gdn_dt_bias, lane_pad).reshape(1, LANES)
    return w_gate, alog_row, dtb_row


def _tile_plan(n_seq, seq_len):
    tokens = n_seq * seq_len
    if seq_len > SUBLANES:
        return dict(tm=min(tokens, 1024), proj_tm=min(tokens, 2048), row_tile=min(seq_len, 512),
                    attn_tq=min(seq_len, 1024), attn_seqs=1, gdn_tiles=min(4, seq_len // GDN_CHUNK))
    return dict(tm=min(tokens, 1024), proj_tm=min(tokens, 1024), row_tile=min(n_seq, 16) * SUBLANES,
                attn_tq=seq_len, attn_seqs=min(n_seq, 8), gdn_tiles=max(1, min(2, tokens // GDN_CHUNK)))


def _conv_state_in(buf, carry):
    if carry:
        return jnp.pad(buf, ((0, 0), (SUBLANES - (CONV_W - 1), 0), (0, 0)))
    return jnp.swapaxes(buf, 0, 1)


def _conv_state_out(state, carry):
    return state[:, SUBLANES - (CONV_W - 1):] if carry else jnp.swapaxes(state, 0, 1)


def _group_layer(x, mem_k, col_k, mem_v, col_v, rnn_buf, rnn_h0, gdn_buf, gdn_s0, lw, *, n_seq, seq_len,
                 tm, proj_tm, row_tile, attn_tq, attn_seqs, gdn_tiles, final_g):
    D = x.shape[1]
    half = D // 2
    n_gate_blk = N_BRANCH * D // half
    c_rx, c_rg, c_gq, c_gz, c_mq = (n_gate_blk + i for i in (0, 1, 2, 5, 6))
    carry = seq_len > SUBLANES
    assert carry == (seq_len >= GDN_CHUNK)
    segments, small_row = _in_proj_segments(D)
    proj, small = in_projection(x, lw["norm_mix_g"], lw["w_in_t"], segments, small_row, tm=proj_tm,
                                tn=512 if proj_tm > 1024 else 1024)

    y_rnn, rnn_last, h_last = rglru_branch(
        proj, c_rx, c_rg, _conv_state_in(rnn_buf, carry), rnn_h0.reshape(n_seq, 1, half),
        lw["rnn_conv_w"], lw["rnn_conv_b"], lw["w_gate"], lw["rnn_bx"], lw["rnn_ba"], lw["rnn_L"],
        n_seq=n_seq, seq_len=seq_len, rows=row_tile)

    y_gdn, s_new, gdn_last = gdn_branch(
        proj, c_gq, c_gz, small, _conv_state_in(gdn_buf, carry), lw["gdn_conv_w"], lw["alog_row"],
        lw["dtb_row"],
        lw["gdn_norm_g"], gdn_s0, n_seq=n_seq, seq_len=seq_len, tiles_per_step=gdn_tiles)

    y_mem = mem_attention(proj, c_mq, mem_k, col_k, mem_v, col_v, width=half, n_seq=n_seq, seq_len=seq_len,
                          tq=attn_tq, seq_per_step=attn_seqs)

    merged = merge_branches(y_rnn, y_gdn, y_mem, proj, 0, lw["w_rnn_up"], lw["w_gdn_up"], lw["w_mem_up"],
                            tm=proj_tm, tn=256)
    x1 = matmul_residual(merged, lw["w_out"], x, tm=proj_tm, tn=512)
    x2 = mlp_final_norm(x1, lw["norm_mlp_g"], lw["w_mlp_up"], lw["w_mlp_down"], final_g, tm=tm, tf=512)
    states = (_conv_state_out(rnn_last, carry), h_last.reshape(n_seq, half),
              _conv_state_out(gdn_last, carry), s_new)
    return x2, states


def kernel(x_prompt, x_sample, mem_prompt, cache_mem_k, cache_mem_v, state_rnn_conv, state_rnn_h,
           state_gdn_conv, state_gdn_S, norm_mix_g, w_in, rnn_conv_w, rnn_conv_b, rnn_wx, rnn_bx, rnn_wa,
           rnn_ba, rnn_L, gdn_conv_w, gdn_A_log, gdn_dt_bias, gdn_norm_g, mem_norm_g, w_mem_kv, w_rnn_up,
           w_gdn_up, w_mem_up, w_out, norm_mlp_g, w_mlp_up, w_mlp_down, norm_final_g):
    depth = w_in.shape[0]
    assert depth == 1, "the final norm is fused into the last layer's MLP kernel; one layer supported"
    Bp, T, D = x_prompt.shape
    Bs, Ts, _ = x_sample.shape
    half = D // 2
    mem_len = mem_prompt.shape[1]
    assert Ts == SUBLANES and T % GDN_CHUNK == 0
    l = 0
    w_gate, alog_row, dtb_row = _prep_layer_weights(rnn_wx[l], rnn_wa[l], gdn_A_log[l], gdn_dt_bias[l])
    lw = dict(norm_mix_g=norm_mix_g[l], w_in_t=jnp.swapaxes(w_in[l], 0, 1), w_gate=w_gate,
              rnn_conv_w=rnn_conv_w[l], rnn_conv_b=rnn_conv_b[l], rnn_bx=rnn_bx[l], rnn_ba=rnn_ba[l],
              rnn_L=rnn_L[l], gdn_conv_w=gdn_conv_w[l], alog_row=alog_row, dtb_row=dtb_row,
              gdn_norm_g=gdn_norm_g[l], w_rnn_up=w_rnn_up[l], w_gdn_up=w_gdn_up[l],
              w_mem_up=w_mem_up[l], w_out=w_out[l], norm_mlp_g=norm_mlp_g[l],
              w_mlp_up=w_mlp_up[l], w_mlp_down=w_mlp_down[l])

    kv, k4, v4 = memory_kv(mem_prompt.reshape(Bp * mem_len, D), mem_norm_g[l], w_mem_kv[l],
                           tm=min(Bp * mem_len, 512))
    kv3 = kv.reshape(Bp, mem_len, 2 * half)
    zeros = lambda *s: jnp.zeros(s, F32)
    yp, (rb_p, rh_p, gb_p, gs_p) = _group_layer(
        x_prompt.reshape(Bp * T, D), kv3, 0, kv3, 1,
        zeros(Bp, CONV_W - 1, half), zeros(Bp, half), zeros(Bp, CONV_W - 1, 3 * half),
        zeros(Bp, GDN_HEADS, half // GDN_HEADS, half // GDN_HEADS), lw,
        n_seq=Bp, seq_len=T, final_g=norm_final_g, **_tile_plan(Bp, T))
    mk_p = k4.reshape(1, Bp, mem_len, MEM_HEADS, half // MEM_HEADS)
    mv_p = v4.reshape(1, Bp, mem_len, MEM_HEADS, half // MEM_HEADS)

    ys, (rb_s, rh_s, gb_s, gs_s) = _group_layer(
        x_sample.reshape(Bs * Ts, D), cache_mem_k.reshape((depth * Bs,) + cache_mem_k.shape[2:]), 0,
        cache_mem_v.reshape((depth * Bs,) + cache_mem_v.shape[2:]), 0,
        state_rnn_conv[l], state_rnn_h[l], state_gdn_conv[l], state_gdn_S[l], lw,
        n_seq=Bs, seq_len=Ts, final_g=norm_final_g, **_tile_plan(Bs, Ts))

    return (yp.reshape(Bp, T, D), ys.reshape(Bs, Ts, D), mk_p, mv_p, rb_p[None], rh_p[None], gb_p[None],
            gs_p[None], rb_s[None], rh_s[None], gb_s[None], gs_s[None])
```

```python
import functools

import jax
import jax.numpy as jnp
from jax import lax
from jax.experimental import pallas as pl
from jax.experimental.pallas import tpu as pltpu

F32 = jnp.float32
BF16 = jnp.bfloat16

EPS = 1e-6
RG_C = 8.0
CONV_W = 4
RNN_BLOCKS = 8
GDN_HEADS = 8
GDN_CHUNK = 64
MEM_HEADS = 4
N_BRANCH = 3
MERGE_ROW_CHUNKS = 8

SUBLANES = 8
BF16_SUBLANES = 16
LANES = 128
VMEM_LIMIT_BYTES = 56 * 1024 * 1024


def _lhs_dtype(slab_rows):
    return BF16 if slab_rows % BF16_SUBLANES == 0 else F32


def _params(*sem):
    return pltpu.CompilerParams(dimension_semantics=sem, vmem_limit_bytes=VMEM_LIMIT_BYTES)


def _dot(a, b):
    return jnp.dot(a.astype(BF16), b.astype(BF16), preferred_element_type=F32)


def _dot_nt(a, b):
    return lax.dot_general(a.astype(BF16), b.astype(BF16), (((1,), (1,)), ((), ())),
                           preferred_element_type=F32)


def _dot_tn(a, b):
    return lax.dot_general(a.astype(BF16), b.astype(BF16), (((0,), (0,)), ((), ())),
                           preferred_element_type=F32)


def _rmsnorm(x, g):
    return (x * lax.rsqrt(jnp.mean(x * x, axis=-1, keepdims=True) + EPS)) * g


def _softplus(x):
    return jnp.maximum(x, 0.0) + jnp.log1p(jnp.exp(-jnp.abs(x)))


_sigmoid = jax.nn.sigmoid


def _silu(x):
    return x * _sigmoid(x)


def _memory_kv_kernel(x_ref, g_ref, w_ref, kv_ref, k4_ref, v4_ref, xn_ref):
    j = pl.program_id(1)

    @pl.when(j == 0)
    def _():
        xn_ref[...] = _rmsnorm(x_ref[...], g_ref[...]).astype(BF16)

    acc = _dot(xn_ref[...], w_ref[...])
    kv_ref[...] = acc
    split = acc.reshape(k4_ref.shape)

    @pl.when(j == 0)
    def _():
        k4_ref[...] = split

    @pl.when(j == 1)
    def _():
        v4_ref[...] = split


def memory_kv(x, g, w, *, tm):
    M, K = x.shape
    C = w.shape[1] // 2
    hd = C // MEM_HEADS
    head_spec = pl.BlockSpec((tm, MEM_HEADS, hd), lambda i, j: (i, 0, 0))
    return pl.pallas_call(
        _memory_kv_kernel,
        grid=(M // tm, 2),
        in_specs=[pl.BlockSpec((tm, K), lambda i, j: (i, 0)),
                  pl.BlockSpec((1, K), lambda i, j: (0, 0)),
                  pl.BlockSpec((K, C), lambda i, j: (0, j))],
        out_specs=[pl.BlockSpec((tm, C), lambda i, j: (i, j)), head_spec, head_spec],
        out_shape=[jax.ShapeDtypeStruct((M, 2 * C), F32)] + [jax.ShapeDtypeStruct((M, MEM_HEADS, hd), F32)] * 2,
        scratch_shapes=[pltpu.VMEM((tm, K), BF16)],
        compiler_params=_params("parallel", "arbitrary"),
        name="memory_kv",
    )(x, g.reshape(1, K), w)


def _row_tile_prefetch(x_hbm, x_buf, sem, consume):
    i, j = pl.program_id(0), pl.program_id(1)
    tm = x_buf.shape[0]

    def copy(tile):
        return pltpu.make_async_copy(x_hbm.at[pl.ds(pl.multiple_of(tile * tm, tm), tm), :], x_buf, sem)

    @pl.when((i == 0) & (j == 0))
    def _():
        copy(0).start()

    @pl.when(j == 0)
    def _():
        copy(i).wait()
        consume(x_buf)

    @pl.when((j == 1) & (i + 1 < pl.num_programs(0)))
    def _():
        copy(i + 1).start()


def _in_proj_kernel(x_hbm, g_ref, wt_ref, wst_ref, o_ref, os_ref, xn_ref, x_buf, sem):
    def consume(x_ref):
        xn_ref[...] = _rmsnorm(x_ref[...], g_ref[...]).astype(BF16)
        os_ref[...] = _dot_nt(xn_ref[...], wst_ref[...])

    _row_tile_prefetch(x_hbm, x_buf, sem, consume)
    o_ref[...] = _dot_nt(xn_ref[...], wt_ref[...])


def in_projection(x, g, w_t, segments, small_row, *, tm, tn):
    M, K = x.shape
    n_tiles = [n // tn for _, n in segments]
    assert all(n % tn == 0 and r % SUBLANES == 0 for r, n in segments) and small_row % LANES == 0
    N = tn * sum(n_tiles)

    def w_row(j):
        row, first = None, 0
        for (r0, _), nt in zip(segments, n_tiles):
            cand = r0 // SUBLANES + (tn // SUBLANES) * (j - first)
            row = cand if row is None else jnp.where(j >= first, cand, row)
            first += nt
        return row * SUBLANES

    assert N // tn >= 2
    outs = pl.pallas_call(
        _in_proj_kernel,
        grid=(M // tm, N // tn),
        in_specs=[pl.BlockSpec(memory_space=pl.ANY),
                  pl.BlockSpec((1, K), lambda i, j: (0, 0)),
                  pl.BlockSpec((pl.Element(tn), pl.Element(K)), lambda i, j: (w_row(j), 0)),
                  pl.BlockSpec((LANES, K), lambda i, j: (small_row // LANES, 0))],
        out_specs=[pl.BlockSpec((tm, tn), lambda i, j: (i, j)), pl.BlockSpec((tm, LANES), lambda i, j: (i, 0))],
        out_shape=[jax.ShapeDtypeStruct((M, N), F32), jax.ShapeDtypeStruct((M, LANES), F32)],
        scratch_shapes=[pltpu.VMEM((tm, K), BF16), pltpu.VMEM((tm, K), F32), pltpu.SemaphoreType.DMA(())],
        compiler_params=_params("arbitrary", "arbitrary"),
        name="in_projection",
    )(x, g.reshape(1, K), w_t, w_t)
    return outs


def _causal_conv(x, xprev, w_ref):
    t_idx = lax.broadcasted_iota(jnp.int32, x.shape, 1)
    y = x * w_ref[CONV_W - 1:CONV_W, :][None]
    for k in range(1, CONV_W):
        shifted = jnp.where(t_idx >= k, pltpu.roll(x, k, 1), pltpu.roll(xprev, k, 1))
        y = y + shifted * w_ref[CONV_W - 1 - k:CONV_W - k, :][None]
    return y


def _segment_scan(a, u):
    t_idx = lax.broadcasted_iota(jnp.int32, a.shape, 1)
    s = 1
    while s < SUBLANES:
        keep = t_idx >= s
        u = jnp.where(keep, a * pltpu.roll(u, s, 1) + u, u)
        a = jnp.where(keep, a * pltpu.roll(a, s, 1), a)
        s *= 2
    return a, u


def _state_group(state_ref, cols=slice(None)):
    rows, nb = state_ref.shape[0], state_ref.shape[1]
    first = [state_ref[r, :, cols] for r in range(rows)]
    t_idx = lax.broadcasted_iota(jnp.int32, (nb, SUBLANES, first[0].shape[-1]), 1)
    y = jnp.zeros(t_idx.shape, F32)
    for r, row in enumerate(first):
        y = jnp.where(t_idx == SUBLANES - rows + r, row[:, None, :], y)
    return y


def _store_state(state_ref, x, cols=slice(None)):
    rows = state_ref.shape[0]
    for r in range(rows):
        state_ref[r, :, cols] = x[:, SUBLANES - rows + r, :]


def _conv_state_spec(carry, nb, n_seq, width):
    if carry:
        return pl.BlockSpec((nb, SUBLANES, width), lambda b, t: (b, 0, 0)), (n_seq, SUBLANES, width)
    return pl.BlockSpec((CONV_W - 1, nb, width), lambda b, t: (0, b, 0)), (CONV_W - 1, n_seq, width)


def _stage_groups(x, xprev_ref, xs_ref, first):
    G = x.shape[0]

    @pl.when(first)
    def _():
        xs_ref[G] = xprev_ref[0]

    xs_ref[0] = xs_ref[G]
    xs_ref[1:G + 1] = x
    return xs_ref[0:G]


def _rglru_kernel(rx_ref, rg_ref, xprev_ref, h0_ref, cw_ref, cb_ref, wg_ref, bx_ref, ba_ref, l_ref,
                  y_ref, xlast_ref, hlast_ref, *scratch, carry):
    R, C = rx_ref.shape
    G = R // SUBLANES
    x = rx_ref[...].reshape(G, SUBLANES, C)
    if carry:
        xs_ref, a_ref, u_ref, h_ref, hc_ref = scratch
        first = pl.program_id(1) == 0
        xprev = _stage_groups(x, xprev_ref, xs_ref, first)

        @pl.when(first)
        def _():
            hc_ref[...] = h0_ref[0]
    else:
        xprev = _state_group(xprev_ref)

    xc = (_causal_conv(x, xprev, cw_ref) + cb_ref[...][None]).reshape(R, C)
    xb = xc.astype(BF16)
    bs = C // RNN_BLOCKS
    zi, zr = [], []
    for n in range(RNN_BLOCKS):
        z = jnp.dot(xb[:, n * bs:(n + 1) * bs], wg_ref[n], preferred_element_type=F32)
        zi.append(z[:, :bs])
        zr.append(z[:, bs:])
    gi = _sigmoid(jnp.concatenate(zi, axis=1) + bx_ref[...])
    gr = _sigmoid(jnp.concatenate(zr, axis=1) + ba_ref[...])
    lv = l_ref[...]
    log_sig_l = -_softplus(-lv)
    log_a = RG_C * gr * log_sig_l
    a = jnp.exp(log_a)
    u = jnp.sqrt(-jnp.tanh(log_a) * (a * a + 1.0)) * (gi * xc)
    a_cum, h_loc = _segment_scan(a.reshape(G, SUBLANES, C), u.reshape(G, SUBLANES, C))

    if carry:
        a_ref[...] = a_cum
        u_ref[...] = h_loc

        def body(g, h_prev):
            hg = u_ref[g] + a_ref[g] * h_prev
            h_ref[g] = hg
            return hg[SUBLANES - 1:SUBLANES, :]

        h_last = lax.fori_loop(0, G, body, hc_ref[...])
        hc_ref[...] = h_last
        h = h_ref[...]
        xlast_ref[0] = x[G - 1]
        hlast_ref[0] = h_last
    else:
        h = h_loc + a_cum * h0_ref[...]
        _store_state(xlast_ref, x)
        hlast_ref[...] = h[:, SUBLANES - 1:SUBLANES, :]

    y = h.reshape(R, C) * jax.nn.gelu(rg_ref[...])
    y_ref[...] = y.astype(BF16)


def rglru_branch(proj, col_rx, col_rg, xprev, h0, cw, cb, wg, bx, ba, lam, *, n_seq, seq_len, rows):
    C = cw.shape[1]
    carry = seq_len > SUBLANES
    if carry:
        nt = seq_len // rows
        grid = (n_seq, nt)
        row_map = lambda b, t: (b * nt + t)
        nb = 1
        G = rows // SUBLANES
        scratch = [pltpu.VMEM((G + 1, SUBLANES, C), F32), pltpu.VMEM((G, SUBLANES, C), F32),
                   pltpu.VMEM((G, SUBLANES, C), F32), pltpu.VMEM((G, SUBLANES, C), F32),
                   pltpu.VMEM((1, C), F32)]
    else:
        nb = rows // SUBLANES
        grid = (n_seq // nb, 1)
        row_map = lambda b, t: b
        scratch = []
    conv_spec, conv_shape = _conv_state_spec(carry, nb, n_seq, C)
    const2 = lambda b, t: (0, 0)
    in_specs = [pl.BlockSpec((rows, C), lambda b, t: (row_map(b, t), col_rx)),
                pl.BlockSpec((rows, C), lambda b, t: (row_map(b, t), col_rg)),
                conv_spec,
                pl.BlockSpec((nb, 1, C), lambda b, t: (b, 0, 0)),
                pl.BlockSpec((CONV_W, C), const2),
                pl.BlockSpec((1, C), const2),
                pl.BlockSpec(wg.shape, lambda b, t: (0, 0, 0)),
                pl.BlockSpec((1, C), const2), pl.BlockSpec((1, C), const2), pl.BlockSpec((1, C), const2)]
    tokens = n_seq * seq_len
    out_shape = [jax.ShapeDtypeStruct((tokens, C), BF16),
                 jax.ShapeDtypeStruct(conv_shape, F32),
                 jax.ShapeDtypeStruct((n_seq, 1, C), F32)]
    out_specs = [pl.BlockSpec((rows, C), lambda b, t: (row_map(b, t), 0)),
                 conv_spec,
                 pl.BlockSpec((nb, 1, C), lambda b, t: (b, 0, 0))]
    return pl.pallas_call(
        functools.partial(_rglru_kernel, carry=carry),
        grid=grid, in_specs=in_specs, out_specs=out_specs, out_shape=out_shape,
        scratch_shapes=scratch,
        compiler_params=_params("parallel", "arbitrary"),
        name="rglru",
    )(proj, proj, xprev, h0, cw, cb.reshape(1, C), wg, bx.reshape(1, C), ba.reshape(1, C),
      lam.reshape(1, C))


def _l2norm_heads(x, scale):
    dh = x.shape[1] // GDN_HEADS
    outs = []
    for h in range(GDN_HEADS):
        xh = x[:, h * dh:(h + 1) * dh]
        xh = xh * lax.rsqrt(jnp.sum(xh * xh, axis=-1, keepdims=True) + EPS)
        outs.append(xh * scale if scale != 1.0 else xh)
    return jnp.concatenate(outs, axis=1)


def _gdn_conv_norm(src_refs, xprev_ref, cw_ref, dst_refs, xlast_ref, xs_ref, carry):
    R, C = src_refs[0].shape
    G = R // SUBLANES
    dk = C // GDN_HEADS
    for s, (src, dst) in enumerate(zip(src_refs, dst_refs)):
        x = src[...].reshape(G, SUBLANES, C)
        cols = slice(s * C, (s + 1) * C)
        if carry:
            xprev = _stage_groups(x, xprev_ref.at[:, :, cols], xs_ref.at[s], pl.program_id(1) == 0)
            xlast_ref[0, :, cols] = x[G - 1]
        else:
            xprev = _state_group(xprev_ref, cols)
            _store_state(xlast_ref, x, cols)
        y = _silu(_causal_conv(x, xprev, cw_ref.at[:, cols])).reshape(R, C)
        if s == 0:
            y = _l2norm_heads(y, dk ** -0.5)
        elif s == 1:
            y = _l2norm_heads(y, 1.0)
        dst[...] = y


def _segment_cumsum(x, seg):
    pos = lax.broadcasted_iota(jnp.int32, x.shape, 0) & (seg - 1)
    s = 1
    while s < seg:
        x = x + jnp.where(pos >= s, pltpu.roll(x, s, 0), 0.0)
        s *= 2
    return x


def _segment_last(x, seg):
    n = x.shape[0]
    pos = lax.broadcasted_iota(jnp.int32, x.shape, 0) & (seg - 1)
    s = seg // 2
    while s >= 1:
        x = jnp.where((pos & (2 * s - 1)) < s, pltpu.roll(x, n - s, 0), x)
        s //= 2
    return x


class _TileMasks:
    def __init__(self, rows, seg):
        r = lax.broadcasted_iota(jnp.int32, (rows, rows), 0)
        c = lax.broadcasted_iota(jnp.int32, (rows, rows), 1)
        shift = seg.bit_length() - 1
        same = (r >> shift) == (c >> shift)
        self.incl = same & (r >= c)
        self.strict = same & (r > c)
        self.eye = r == c
        self.levels = []
        s = 1
        while s < seg:
            b = s.bit_length() - 1
            self.levels.append(((r >> (b + 1)) == (c >> (b + 1))) & (((r >> b) & 1) == 1) & (((c >> b) & 1) == 0))
            s *= 2


def _unit_lower_inverses(ms, masks):
    eye = jnp.where(masks.eye, 1.0, 0.0)
    xs = [eye - jnp.where(masks.levels[0], m, 0.0) for m in ms]
    for level in masks.levels[1:]:
        ts = [_dot(jnp.where(level, m, 0.0), x) for m, x in zip(ms, xs)]
        xs = [x - _dot(x, t) for x, t in zip(xs, ts)]
    return xs


def _gdn_tiles_local(chains, masks):
    r, d = chains[0][0].shape
    kbs, decays, kk_qks = [], [], []
    for q, k, v, beta, gc, g_last in chains:
        gc_row = jnp.sum(jnp.where(masks.eye, gc, 0.0), axis=0, keepdims=True)
        decays.append(jnp.where(masks.incl, jnp.exp(gc - gc_row), 0.0))
        kbs.append(k * beta)
        kk_qks.append(_dot_nt(jnp.concatenate([kbs[-1], q], axis=0), k))
    ms = [jnp.where(masks.strict, kq[:r] * dec, 0.0) for kq, dec in zip(kk_qks, decays)]
    tms = _unit_lower_inverses(ms, masks)
    out = []
    for (q, k, v, beta, gc, g_last), kb, dec, kq, tm in zip(chains, kbs, decays, kk_qks, tms):
        eg = jnp.exp(gc)
        vk = _dot(tm, jnp.concatenate([v * beta, kb * eg], axis=1))
        out.append(dict(value=vk[:, :d], kcd=vk[:, d:], attn=kq[r:] * dec, qg=q * eg,
                        kd=k * jnp.exp(g_last - gc), decay_last=jnp.exp(g_last)))
    return out


def _gdn_kernel(gq_ref, gk_ref, gv_ref, z_ref, sm_ref, xprev_ref, cw_ref, alog_ref, dtb_ref, ng_ref, s0_ref,
                o_ref, s_ref, xlast_ref, q_ref, k_ref, v_ref, *scratch, tile, seg, n_tiles, carry):
    dk = q_ref.shape[1] // GDN_HEADS
    n_seg = tile // seg
    masks = _TileMasks(tile, seg)
    neg_a = -jnp.exp(alog_ref[...])
    dtb = dtb_ref[...]
    ng = ng_ref[...]
    if carry:
        @pl.when(pl.program_id(1) == 0)
        def _():
            s_ref[...] = s0_ref[...]
    _gdn_conv_norm((gq_ref, gk_ref, gv_ref), xprev_ref, cw_ref, (q_ref, k_ref, v_ref), xlast_ref,
                   scratch[0] if carry else None, carry)

    chains = []
    for i in range(n_tiles):
        rows = slice(i * tile, (i + 1) * tile)
        sm = sm_ref[rows, :]
        beta_all = _sigmoid(sm)
        gc_all = _segment_cumsum(neg_a * _softplus(sm + dtb), seg)
        gl_all = _segment_last(gc_all, seg)
        for h in range(GDN_HEADS):
            cols = slice(h * dk, (h + 1) * dk)
            lane = slice(GDN_HEADS + h, GDN_HEADS + h + 1)
            chains.append((q_ref[rows, cols], k_ref[rows, cols], v_ref[rows, cols],
                           beta_all[:, h:h + 1], gc_all[:, lane], gl_all[:, lane]))
    local = _gdn_tiles_local(chains, masks)

    heads = range(GDN_HEADS)
    for i in range(n_tiles):
        rows = slice(i * tile, (i + 1) * tile)
        loc = local[i * GDN_HEADS:(i + 1) * GDN_HEADS]
        v_new = [[] for _ in heads]
        o_state = [[] for _ in heads]
        pairs = [(j, h) for j in range(n_seg) for h in heads]
        sl = lambda j: slice(j * seg, (j + 1) * seg)
        seq = lambda j: 0 if carry else i * n_seg + j
        s_prev = {(j, h): s_ref[seq(j), h] if carry else s0_ref[seq(j), h] for j, h in pairs}
        rs = {(j, h): _dot(jnp.concatenate([loc[h]["kcd"][sl(j)], loc[h]["qg"][sl(j)]], axis=0), s_prev[j, h])
              for j, h in pairs}
        vns = {(j, h): loc[h]["value"][sl(j)] - rs[j, h][:seg] for j, h in pairs}
        for j, h in pairs:
            v_new[h].append(vns[j, h])
            o_state[h].append(rs[j, h][seg:])
            s_ref[seq(j), h] = (s_prev[j, h] * loc[h]["decay_last"][j * seg:j * seg + 1, :]
                                + _dot_tn(loc[h]["kd"][sl(j)], vns[j, h]))
        for h in heads:
            cols = slice(h * dk, (h + 1) * dk)
            vn = jnp.concatenate(v_new[h], axis=0) if n_seg > 1 else v_new[h][0]
            os_ = jnp.concatenate(o_state[h], axis=0) if n_seg > 1 else o_state[h][0]
            o = os_ + _dot(loc[h]["attn"], vn)
            o_ref[rows, cols] = (_rmsnorm(o, ng) * _silu(z_ref[rows, cols])).astype(o_ref.dtype)


def gdn_branch(proj, col_q, col_z, small, xprev, cw, alog_row, dtb_row, ng, s0, *, n_seq, seq_len,
               tiles_per_step):
    tokens = proj.shape[0]
    C = cw.shape[1] // 3
    dk = C // GDN_HEADS
    tile = GDN_CHUNK
    carry = seq_len >= GDN_CHUNK
    seg = GDN_CHUNK if carry else seq_len
    rows = tiles_per_step * tile
    if carry:
        nt = seq_len // rows
        seq_per_step = 1
        grid = (n_seq, nt)
    else:
        nt = 1
        seq_per_step = rows // seg
        grid = (n_seq // seq_per_step, 1)
    row_map = lambda b, t: (b * nt + t)
    tok_spec = pl.BlockSpec((rows, C), lambda b, t: (row_map(b, t), 0))
    const2 = lambda b, t: (0, 0)
    s_spec = pl.BlockSpec((seq_per_step, GDN_HEADS, dk, dk), lambda b, t: (b, 0, 0, 0))
    conv_spec, conv_shape = _conv_state_spec(carry, seq_per_step, n_seq, 3 * C)
    in_specs = [pl.BlockSpec((rows, C), lambda b, t, c=c: (row_map(b, t), c))
                for c in (col_q, col_q + 1, col_q + 2, col_z)]
    in_specs += [pl.BlockSpec((rows, LANES), lambda b, t: (row_map(b, t), 0)),
                 conv_spec, pl.BlockSpec((CONV_W, 3 * C), const2),
                 pl.BlockSpec((1, LANES), const2), pl.BlockSpec((1, LANES), const2),
                 pl.BlockSpec((1, dk), const2), s_spec]
    scratch = [pltpu.VMEM((rows, C), F32)] * 3
    if carry:
        scratch.append(pltpu.VMEM((3, rows // SUBLANES + 1, SUBLANES, C), F32))
    return pl.pallas_call(
        functools.partial(_gdn_kernel, tile=tile, seg=seg, n_tiles=tiles_per_step, carry=carry),
        grid=grid, in_specs=in_specs,
        out_specs=[tok_spec, s_spec, conv_spec],
        out_shape=[jax.ShapeDtypeStruct((tokens, C), _lhs_dtype(tile)), jax.ShapeDtypeStruct(s0.shape, F32),
                   jax.ShapeDtypeStruct(conv_shape, F32)],
        scratch_shapes=scratch,
        compiler_params=_params("parallel", "arbitrary"),
        name="gdn_branch",
    )(proj, proj, proj, proj, small, xprev, cw, alog_row, dtb_row, ng.reshape(1, dk), s0)


def _mem_attn_kernel(q_ref, k_ref, v_ref, o_ref, *, n_seq, tq):
    hd = q_ref.shape[1] // MEM_HEADS
    scale = hd ** -0.5
    rows = lambda s: slice(s * tq, (s + 1) * tq)
    cols = lambda h: slice(h * hd, (h + 1) * hd)
    if len(k_ref.shape) == 4:
        mem_rows = k_ref.shape[1] * MEM_HEADS
        r_head = lax.broadcasted_iota(jnp.int32, (MEM_HEADS * tq, mem_rows), 0) // tq
        c_head = lax.broadcasted_iota(jnp.int32, (MEM_HEADS * tq, mem_rows), 1) & (MEM_HEADS - 1)
        own = r_head == c_head
        scores = []
        for s in range(n_seq):
            q_all = jnp.concatenate([q_ref[rows(s), cols(h)] for h in range(MEM_HEADS)], axis=0)
            sc = _dot_nt(q_all, k_ref[s].reshape(mem_rows, hd)) * scale
            scores.append(jnp.where(own, sc, -jnp.inf))
        probs = []
        for sc in scores:
            e = jnp.exp(sc - jnp.max(sc, axis=-1, keepdims=True))
            probs.append(e / jnp.sum(e, axis=-1, keepdims=True))
        for s, p in enumerate(probs):
            o_all = _dot(p, v_ref[s].reshape(mem_rows, hd))
            for h in range(MEM_HEADS):
                o_ref[rows(s), cols(h)] = o_all[h * tq:(h + 1) * tq].astype(o_ref.dtype)
        return
    pairs = [(s, h) for s in range(n_seq) for h in range(MEM_HEADS)]
    mem = lambda ref, s, h: ref[s, :, cols(h)]
    scores = [_dot_nt(q_ref[rows(s), cols(h)], mem(k_ref, s, h)) * scale for s, h in pairs]
    probs = []
    for sc in scores:
        e = jnp.exp(sc - jnp.max(sc, axis=-1, keepdims=True))
        probs.append(e / jnp.sum(e, axis=-1, keepdims=True))
    for (s, h), p in zip(pairs, probs):
        o_ref[rows(s), cols(h)] = _dot(p, mem(v_ref, s, h)).astype(o_ref.dtype)


def mem_attention(proj, col_q, mem_k, col_k, mem_v, col_v, *, width, n_seq, seq_len, tq, seq_per_step):
    tokens = n_seq * seq_len
    mem_len = mem_k.shape[1]
    C = width
    nt = seq_len // tq
    rows = seq_per_step * tq
    grid = (n_seq // seq_per_step, nt)
    row_map = lambda b, t: (b * nt + t)
    if mem_k.ndim == 4:
        blk = (seq_per_step,) + mem_k.shape[1:]
        k_spec = v_spec = pl.BlockSpec(blk, lambda b, t: (b, 0, 0, 0))
    else:
        k_spec = pl.BlockSpec((seq_per_step, mem_len, C), lambda b, t: (b, 0, col_k))
        v_spec = pl.BlockSpec((seq_per_step, mem_len, C), lambda b, t: (b, 0, col_v))
    return pl.pallas_call(
        functools.partial(_mem_attn_kernel, n_seq=seq_per_step, tq=tq),
        grid=grid,
        in_specs=[pl.BlockSpec((rows, C), lambda b, t: (row_map(b, t), col_q)), k_spec, v_spec],
        out_specs=pl.BlockSpec((rows, C), lambda b, t: (row_map(b, t), 0)),
        out_shape=jax.ShapeDtypeStruct((tokens, C), _lhs_dtype(tq)),
        compiler_params=_params("parallel", "arbitrary"),
        name="mem_attention",
    )(proj, mem_k, mem_v)


def _merge_kernel(yr_ref, yg_ref, ym_ref, g0_ref, g1_ref, g2_ref, w0_ref, w1_ref, w2_ref, o_ref):
    weights = [w.astype(BF16) for w in (w0_ref[...], w1_ref[...], w2_ref[...])]
    chunk = o_ref.shape[0] // MERGE_ROW_CHUNKS
    for r in range(MERGE_ROW_CHUNKS):
        rows = slice(r * chunk, (r + 1) * chunk)
        acc = None
        for y_ref, g_ref, w in zip((yr_ref, yg_ref, ym_ref), (g0_ref, g1_ref, g2_ref), weights):
            term = _sigmoid(g_ref[rows, :]) * _dot(y_ref[rows, :], w)
            acc = term if acc is None else acc + term
        o_ref[rows, :] = acc.astype(BF16)


def merge_branches(y_rnn, y_gdn, y_mem, proj, col_gate, w_rnn_up, w_gdn_up, w_mem_up, *, tm, tn):
    M, C = y_rnn.shape
    N = w_rnn_up.shape[1]
    nj = N // tn
    y_spec = pl.BlockSpec((tm, C), lambda i, j: (i, 0))
    w_spec = pl.BlockSpec((C, tn), lambda i, j: (0, j))
    gate_specs = [pl.BlockSpec((tm, tn), lambda i, j, b=b: (i, col_gate + b * nj + j)) for b in range(N_BRANCH)]
    return pl.pallas_call(
        _merge_kernel,
        grid=(M // tm, nj),
        in_specs=[y_spec, y_spec, y_spec] + gate_specs + [w_spec, w_spec, w_spec],
        out_specs=pl.BlockSpec((tm, tn), lambda i, j: (i, j)),
        out_shape=jax.ShapeDtypeStruct((M, N), BF16),
        compiler_params=_params("parallel", "arbitrary"),
        name="merge_branches",
    )(y_rnn, y_gdn, y_mem, proj, proj, proj, w_rnn_up, w_gdn_up, w_mem_up)


def _matmul_residual_kernel(a_ref, w_ref, x_ref, o_ref):
    o_ref[...] = x_ref[...] + _dot(a_ref[...], w_ref[...])


def matmul_residual(a, w, x, *, tm, tn):
    M, K = a.shape
    N = w.shape[1]
    return pl.pallas_call(
        _matmul_residual_kernel,
        grid=(M // tm, N // tn),
        in_specs=[pl.BlockSpec((tm, K), lambda i, j: (i, 0)),
                  pl.BlockSpec((K, tn), lambda i, j: (0, j)),
                  pl.BlockSpec((tm, tn), lambda i, j: (i, j))],
        out_specs=pl.BlockSpec((tm, tn), lambda i, j: (i, j)),
        out_shape=jax.ShapeDtypeStruct((M, N), F32),
        compiler_params=_params("parallel", "arbitrary"),
        name="matmul_residual",
    )(a, w, x)


def _mlp_kernel(x_hbm, g_ref, wu_ref, wd_ref, gf_ref, o_ref, xn_ref, x_buf, sem):
    def consume(x_ref):
        xn_ref[...] = _rmsnorm(x_ref[...], g_ref[...]).astype(BF16)
        o_ref[...] = x_ref[...]

    _row_tile_prefetch(x_hbm, x_buf, sem, consume)
    hid = _dot(xn_ref[...], wu_ref[...])
    act = jnp.square(jnp.maximum(hid, 0.0))
    o_ref[...] += _dot(act, wd_ref[...])

    @pl.when(pl.program_id(1) == pl.num_programs(1) - 1)
    def _():
        o_ref[...] = _rmsnorm(o_ref[...], gf_ref[...])


def mlp_final_norm(x, g, w_up, w_down, g_final, *, tm, tf):
    M, D = x.shape
    FF = w_up.shape[1]
    assert FF // tf >= 2
    return pl.pallas_call(
        _mlp_kernel,
        grid=(M // tm, FF // tf),
        in_specs=[pl.BlockSpec(memory_space=pl.ANY),
                  pl.BlockSpec((1, D), lambda i, j: (0, 0)),
                  pl.BlockSpec((D, tf), lambda i, j: (0, j)),
                  pl.BlockSpec((tf, D), lambda i, j: (j, 0)),
                  pl.BlockSpec((1, D), lambda i, j: (0, 0))],
        out_specs=pl.BlockSpec((tm, D), lambda i, j: (i, 0)),
        out_shape=jax.ShapeDtypeStruct((M, D), F32),
        scratch_shapes=[pltpu.VMEM((tm, D), BF16), pltpu.VMEM((tm, D), F32), pltpu.SemaphoreType.DMA(())],
        compiler_params=_params("arbitrary", "arbitrary"),
        name="mlp_final_norm",
    )(x, g.reshape(1, D), w_up, w_down, g_final.reshape(1, D))


def _in_proj_segments(d_model):
    half = d_model // 2
    small_row = 6 * half
    mq_row = small_row + 2 * GDN_HEADS
    mg_row = mq_row + half
    return ((mg_row, N_BRANCH * d_model), (0, 6 * half), (mq_row, half)), small_row


def _prep_layer_weights(rnn_wx, rnn_wa, gdn_A_log, gdn_dt_bias):
    w_gate = jnp.concatenate([rnn_wx, rnn_wa], axis=-1).astype(BF16)
    lane_pad = (GDN_HEADS, LANES - 2 * GDN_HEADS)
    alog_row = jnp.pad(gdn_A_log, lane_pad).reshape(1, LANES)
    dtb_row = jnp.pad(gdn_dt_bias, lane_pad).reshape(1, LANES)
    return w_gate, alog_row, dtb_row


def _tile_plan(n_seq, seq_len):
    tokens = n_seq * seq_len
    if seq_len > SUBLANES:
        return dict(tm=min(tokens, 1024), proj_tm=min(tokens, 2048), row_tile=min(seq_len, 512),
                    attn_tq=min(seq_len, 1024), attn_seqs=1, gdn_tiles=min(4, seq_len // GDN_CHUNK))
    return dict(tm=min(tokens, 1024), proj_tm=min(tokens, 1024), row_tile=min(n_seq, 16) * SUBLANES,
                attn_tq=seq_len, attn_seqs=min(n_seq, 8), gdn_tiles=max(1, min(2, tokens // GDN_CHUNK)))


def _conv_state_in(buf, carry):
    if carry:
        return jnp.pad(buf, ((0, 0), (SUBLANES - (CONV_W - 1), 0), (0, 0)))
    return jnp.swapaxes(buf, 0, 1)


def _conv_state_out(state, carry):
    return state[:, SUBLANES - (CONV_W - 1):] if carry else jnp.swapaxes(state, 0, 1)


def _group_layer(x, mem_k, col_k, mem_v, col_v, rnn_buf, rnn_h0, gdn_buf, gdn_s0, lw, *, n_seq, seq_len,
                 tm, proj_tm, row_tile, attn_tq, attn_seqs, gdn_tiles, final_g):
    D = x.shape[1]
    half = D // 2
    n_gate_blk = N_BRANCH * D // half
    c_rx, c_rg, c_gq, c_gz, c_mq = (n_gate_blk + i for i in (0, 1, 2, 5, 6))
    carry = seq_len > SUBLANES
    assert carry == (seq_len >= GDN_CHUNK)
    segments, small_row = _in_proj_segments(D)
    proj, small = in_projection(x, lw["norm_mix_g"], lw["w_in_t"], segments, small_row, tm=proj_tm,
                                tn=512 if proj_tm > 1024 else 1024)

    y_rnn, rnn_last, h_last = rglru_branch(
        proj, c_rx, c_rg, _conv_state_in(rnn_buf, carry), rnn_h0.reshape(n_seq, 1, half),
        lw["rnn_conv_w"], lw["rnn_conv_b"], lw["w_gate"], lw["rnn_bx"], lw["rnn_ba"], lw["rnn_L"],
        n_seq=n_seq, seq_len=seq_len, rows=row_tile)

    y_gdn, s_new, gdn_last = gdn_branch(
        proj, c_gq, c_gz, small, _conv_state_in(gdn_buf, carry), lw["gdn_conv_w"], lw["alog_row"],
        lw["dtb_row"],
        lw["gdn_norm_g"], gdn_s0, n_seq=n_seq, seq_len=seq_len, tiles_per_step=gdn_tiles)

    y_mem = mem_attention(proj, c_mq, mem_k, col_k, mem_v, col_v, width=half, n_seq=n_seq, seq_len=seq_len,
                          tq=attn_tq, seq_per_step=attn_seqs)

    merged = merge_branches(y_rnn, y_gdn, y_mem, proj, 0, lw["w_rnn_up"], lw["w_gdn_up"], lw["w_mem_up"],
                            tm=proj_tm, tn=256)
    x1 = matmul_residual(merged, lw["w_out"], x, tm=proj_tm, tn=512)
    x2 = mlp_final_norm(x1, lw["norm_mlp_g"], lw["w_mlp_up"], lw["w_mlp_down"], final_g, tm=tm, tf=512)
    states = (_conv_state_out(rnn_last, carry), h_last.reshape(n_seq, half),
              _conv_state_out(gdn_last, carry), s_new)
    return x2, states


def kernel(x_prompt, x_sample, mem_prompt, cache_mem_k, cache_mem_v, state_rnn_conv, state_rnn_h,
           state_gdn_conv, state_gdn_S, norm_mix_g, w_in, rnn_conv_w, rnn_conv_b, rnn_wx, rnn_bx, rnn_wa,
           rnn_ba, rnn_L, gdn_conv_w, gdn_A_log, gdn_dt_bias, gdn_norm_g, mem_norm_g, w_mem_kv, w_rnn_up,
           w_gdn_up, w_mem_up, w_out, norm_mlp_g, w_mlp_up, w_mlp_down, norm_final_g):
    depth = w_in.shape[0]
    assert depth == 1, "the final norm is fused into the last layer's MLP kernel; one layer supported"
    Bp, T, D = x_prompt.shape
    Bs, Ts, _ = x_sample.shape
    half = D // 2
    mem_len = mem_prompt.shape[1]
    assert Ts == SUBLANES and T % GDN_CHUNK == 0
    l = 0
    w_gate, alog_row, dtb_row = _prep_layer_weights(rnn_wx[l], rnn_wa[l], gdn_A_log[l], gdn_dt_bias[l])
    lw = dict(norm_mix_g=norm_mix_g[l], w_in_t=jnp.swapaxes(w_in[l], 0, 1), w_gate=w_gate,
              rnn_conv_w=rnn_conv_w[l], rnn_conv_b=rnn_conv_b[l], rnn_bx=rnn_bx[l], rnn_ba=rnn_ba[l],
              rnn_L=rnn_L[l], gdn_conv_w=gdn_conv_w[l], alog_row=alog_row, dtb_row=dtb_row,
              gdn_norm_g=gdn_norm_g[l], w_rnn_up=w_rnn_up[l], w_gdn_up=w_gdn_up[l],
              w_mem_up=w_mem_up[l], w_out=w_out[l], norm_mlp_g=norm_mlp_g[l],
              w_mlp_up=w_mlp_up[l], w_mlp_down=w_mlp_down[l])

    kv, k4, v4 = memory_kv(mem_prompt.reshape(Bp * mem_len, D), mem_norm_g[l], w_mem_kv[l],
                           tm=min(Bp * mem_len, 512))
    kv3 = kv.reshape(Bp, mem_len, 2 * half)
    zeros = lambda *s: jnp.zeros(s, F32)
    yp, (rb_p, rh_p, gb_p, gs_p) = _group_layer(
        x_prompt.reshape(Bp * T, D), kv3, 0, kv3, 1,
        zeros(Bp, CONV_W - 1, half), zeros(Bp, half), zeros(Bp, CONV_W - 1, 3 * half),
        zeros(Bp, GDN_HEADS, half // GDN_HEADS, half // GDN_HEADS), lw,
        n_seq=Bp, seq_len=T, final_g=norm_final_g, **_tile_plan(Bp, T))
    mk_p = k4.reshape(1, Bp, mem_len, MEM_HEADS, half // MEM_HEADS)
    mv_p = v4.reshape(1, Bp, mem_len, MEM_HEADS, half // MEM_HEADS)

    ys, (rb_s, rh_s, gb_s, gs_s) = _group_layer(
        x_sample.reshape(Bs * Ts, D), cache_mem_k.reshape((depth * Bs,) + cache_mem_k.shape[2:]), 0,
        cache_mem_v.reshape((depth * Bs,) + cache_mem_v.shape[2:]), 0,
        state_rnn_conv[l], state_rnn_h[l], state_gdn_conv[l], state_gdn_S[l], lw,
        n_seq=Bs, seq_len=Ts, final_g=norm_final_g, **_tile_plan(Bs, Ts))

    return (yp.reshape(Bp, T, D), ys.reshape(Bs, Ts, D), mk_p, mv_p, rb_p[None], rh_p[None], gb_p[None],
            gs_p[None], rb_s[None], rh_s[None], gb_s[None], gs_s[None])
```
